```python
import math
import jax, jax.numpy as jnp
from jax import lax
import numpy as np

D_MODEL = 1024
BATCH = 8
SEQ = 2048
DEPTH = 1
DEC_BATCH = 32
DEC_SEQ = 64
PAST_LEN = 1024

CHUNK = 64
D_MIX = D_MODEL
HEAD_DIM = 64
D_ATTN = D_MIX // 2
D_CMLP = D_MIX - D_ATTN
N_HEADS = D_ATTN // HEAD_DIM
N_KV_HEADS = 2
GQA_GROUP = N_HEADS // N_KV_HEADS
D_KV = N_KV_HEADS * HEAD_DIM
WINDOW = 128
N_PREV_CHUNKS = WINDOW // CHUNK
ROPE_DIM = HEAD_DIM // 4
ROPE_THETA = 500000.0
CMLP_BLOCK = 128
CMLP_GROUP_DIM = 64
CMLP_GROUPS = D_CMLP // CMLP_GROUP_DIM
D_IN = D_ATTN + 2 * D_KV + 2 * D_CMLP
D_FF = ((-(-8 * D_MODEL // 3) + 255) // 256) * 256
ALPHA = (2 * DEPTH) ** 0.25
BETA = (8 * DEPTH) ** -0.25
LN_EPS = 1e-5
NEG_INF = -1e30

kernel_name = "hybrid_swa_sink_chunk_gmlp_stream_step"


def _layernorm(x, g, b):
    xf = x.astype(jnp.float32)
    mu = jnp.mean(xf, -1, keepdims=True)
    var = jnp.mean(jnp.square(xf - mu), -1, keepdims=True)
    return ((xf - mu) * lax.rsqrt(var + LN_EPS) * g.astype(jnp.float32) + b.astype(jnp.float32)).astype(x.dtype)


def _rmsnorm(x, g):
    xf = x.astype(jnp.float32)
    ms = jnp.mean(jnp.square(xf), -1, keepdims=True)
    return (xf * lax.rsqrt(ms + LN_EPS) * g.astype(jnp.float32)).astype(x.dtype)


def _rope(x, pos):
    half = ROPE_DIM // 2
    inv = jnp.power(ROPE_THETA, -jnp.arange(half, dtype=jnp.float32) * (2.0 / ROPE_DIM))
    ang = pos.astype(jnp.float32)[:, None] * inv[None, :]
    cos = jnp.cos(ang)[:, None, :]
    sin = jnp.sin(ang)[:, None, :]
    xr = x[..., :ROPE_DIM].astype(jnp.float32)
    x1, x2 = xr[..., :half], xr[..., half:]
    rot = jnp.concatenate([x1 * cos - x2 * sin, x2 * cos + x1 * sin], -1)
    return jnp.concatenate([rot.astype(x.dtype), x[..., ROPE_DIM:]], -1)


def _sink_attention(q, k, v, valid, sinks):
    s = jnp.einsum('...qkgd,...jkd->...kgqj', q.astype(jnp.float32), k.astype(jnp.float32)) * (HEAD_DIM ** -0.5)
    if valid is not None:
        s = jnp.where(valid, s, NEG_INF)
    sink = sinks.astype(jnp.float32).reshape(N_KV_HEADS, GQA_GROUP)[..., None, None]
    m = jnp.maximum(jnp.max(s, -1, keepdims=True), sink)
    p = jnp.exp(s - m)
    denom = jnp.sum(p, -1, keepdims=True) + jnp.exp(sink - m)
    o = jnp.einsum('...kgqj,...jkd->...qkgd', p / denom, v.astype(jnp.float32))
    return o.astype(q.dtype)


def _project(x, w_in, ln_v_g, ln_v_b):
    B, L = x.shape[:2]
    h = x @ w_in
    q = h[..., :D_ATTN].reshape(B, L, N_HEADS, HEAD_DIM)
    k = h[..., D_ATTN:D_ATTN + D_KV].reshape(B, L, N_KV_HEADS, HEAD_DIM)
    v = h[..., D_ATTN + D_KV:D_ATTN + 2 * D_KV].reshape(B, L, N_KV_HEADS, HEAD_DIM)
    uv = jax.nn.gelu(h[..., D_ATTN + 2 * D_KV:])
    u = uv[..., :D_CMLP].reshape(B, L, CMLP_GROUPS, CMLP_GROUP_DIM)
    vm = _layernorm(uv[..., D_CMLP:], ln_v_g, ln_v_b).reshape(B, L, CMLP_GROUPS, CMLP_GROUP_DIM)
    return q, k, v, u, vm


def _window_attn_prompt(q, k, v, sinks):
    B, S = q.shape[:2]
    n_c = S // CHUNK
    pad = N_PREV_CHUNKS * CHUNK
    pos = jnp.arange(S)
    q = _rope(q, pos)
    k = _rope(k, pos)
    kp = jnp.pad(k, ((0, 0), (pad, 0), (0, 0), (0, 0))).reshape(B, n_c + N_PREV_CHUNKS, CHUNK, N_KV_HEADS, HEAD_DIM)
    vp = jnp.pad(v, ((0, 0), (pad, 0), (0, 0), (0, 0))).reshape(B, n_c + N_PREV_CHUNKS, CHUNK, N_KV_HEADS, HEAD_DIM)
    kb = jnp.concatenate([kp[:, i:i + n_c] for i in range(N_PREV_CHUNKS + 1)], axis=2)
    vb = jnp.concatenate([vp[:, i:i + n_c] for i in range(N_PREV_CHUNKS + 1)], axis=2)
    qb = q.reshape(B, n_c, CHUNK, N_KV_HEADS, GQA_GROUP, HEAD_DIM)
    key_pos = jnp.arange(n_c)[:, None] * CHUNK - pad + jnp.arange((N_PREV_CHUNKS + 1) * CHUNK)[None, :]
    valid = (key_pos >= 0)[:, None, None, None, :]
    o = _sink_attention(qb, kb, vb, valid, sinks)
    return o.reshape(B, S, D_ATTN), k, v


def _window_attn_sample(q, k, v, cache_k, cache_v, sinks):
    Bd, L = q.shape[:2]
    pos = PAST_LEN + jnp.arange(L)
    q = _rope(q, pos)
    k = _rope(k, pos)
    kk = jnp.concatenate([cache_k.astype(k.dtype), k], axis=1)
    vv = jnp.concatenate([cache_v.astype(v.dtype), v], axis=1)
    qg = q.reshape(Bd, L, N_KV_HEADS, GQA_GROUP, HEAD_DIM)
    o = _sink_attention(qg, kk, vv, None, sinks)
    return o.reshape(Bd, L, D_ATTN), k, v


def _spatial_gate(u, vm, w_s, b_s):
    lb = u.shape[2]
    i = jnp.arange(lb)
    mask = (i[None, :] // CHUNK) <= (i[:, None] // CHUNK)
    w = jnp.where(mask[None], w_s[:, :lb, :lb], jnp.zeros((), w_s.dtype))
    s = jnp.einsum('gij,bnjgd->bnigd', w, vm) + jnp.transpose(b_s[:, :lb])[:, :, None]
    return u * s


def _merge(attn_o, cmlp_o, norm_attn_g, norm_cmlp_g, w_out):
    cat = jnp.concatenate([_rmsnorm(attn_o, norm_attn_g), _rmsnorm(cmlp_o, norm_cmlp_g)], -1)
    return cat @ w_out


def _post(x, mix, ln1_g, ln1_b, w_gate_up, w_down, ln2_g, ln2_b):
    h = _layernorm(ALPHA * x + mix, ln1_g, ln1_b)
    gu = h @ w_gate_up
    f = (jax.nn.silu(gu[..., :D_FF]) * gu[..., D_FF:]) @ w_down
    return _layernorm(ALPHA * h + f, ln2_g, ln2_b)


def setup_inputs(seed: int = 0) -> dict:
    key = jax.random.key(seed)
    ks = jax.random.split(key, 20)
    cw = min(WINDOW, PAST_LEN)
    f32 = jnp.float32
    nrm = lambda k, shape, scale: (jax.random.normal(k, shape, f32) * scale)
    return {
        "x_prompt": nrm(ks[0], (BATCH, SEQ, D_MODEL), 1.0),
        "x_sample": nrm(ks[1], (DEC_BATCH, DEC_SEQ, D_MODEL), 1.0),
        "cache_win_k": nrm(ks[2], (DEPTH, DEC_BATCH, cw, N_KV_HEADS, HEAD_DIM), 1.0),
        "cache_win_v": nrm(ks[3], (DEPTH, DEC_BATCH, cw, N_KV_HEADS, HEAD_DIM), 1.0),
        "w_in": nrm(ks[4], (DEPTH, D_MODEL, D_IN), D_MODEL ** -0.5),
        "ln_v_g": 1.0 + nrm(ks[5], (DEPTH, D_CMLP), 0.02),
        "ln_v_b": nrm(ks[6], (DEPTH, D_CMLP), 0.02),
        "attn_sinks": nrm(ks[7], (DEPTH, N_HEADS), 1.0),
        "w_spatial": nrm(ks[8], (DEPTH, CMLP_GROUPS, CMLP_BLOCK, CMLP_BLOCK), CMLP_BLOCK ** -0.5),
        "b_spatial": 1.0 + nrm(ks[9], (DEPTH, CMLP_GROUPS, CMLP_BLOCK), 0.1),
        "norm_attn_g": 1.0 + nrm(ks[10], (DEPTH, D_ATTN), 0.02),
        "norm_cmlp_g": 1.0 + nrm(ks[11], (DEPTH, D_CMLP), 0.02),
        "w_out": nrm(ks[12], (DEPTH, D_MIX, D_MODEL), BETA * D_MIX ** -0.5),
        "ln1_g": 1.0 + nrm(ks[13], (DEPTH, D_MODEL), 0.02),
        "ln1_b": nrm(ks[14], (DEPTH, D_MODEL), 0.02),
        "w_gate_up": nrm(ks[15], (DEPTH, D_MODEL, 2 * D_FF), D_MODEL ** -0.5),
        "w_down": nrm(ks[16], (DEPTH, D_FF, D_MODEL), BETA * D_FF ** -0.5),
        "ln2_g": 1.0 + nrm(ks[17], (DEPTH, D_MODEL), 0.02),
        "ln2_b": nrm(ks[18], (DEPTH, D_MODEL), 0.02),
    }


def reference(x_prompt, x_sample, cache_win_k, cache_win_v, w_in, ln_v_g, ln_v_b, attn_sinks, w_spatial, b_spatial,
              norm_attn_g, norm_cmlp_g, w_out, ln1_g, ln1_b, w_gate_up, w_down, ln2_g, ln2_b):
    yp, ys = x_prompt, x_sample
    B, S = yp.shape[:2]
    Bd, L = ys.shape[:2]
    cw_p = min(WINDOW, S)
    kp_rows, vp_rows, ks_rows, vs_rows, ms_rows = [], [], [], [], []
    for l in range(DEPTH):
        q, k, v, u, vm = _project(yp, w_in[l], ln_v_g[l], ln_v_b[l])
        ao, kr, vr = _window_attn_prompt(q, k, v, attn_sinks[l])
        nb = S // CMLP_BLOCK
        co = _spatial_gate(u.reshape(B, nb, CMLP_BLOCK, CMLP_GROUPS, CMLP_GROUP_DIM),
                           vm.reshape(B, nb, CMLP_BLOCK, CMLP_GROUPS, CMLP_GROUP_DIM),
                           w_spatial[l], b_spatial[l]).reshape(B, S, D_CMLP)
        mix = _merge(ao, co, norm_attn_g[l], norm_cmlp_g[l], w_out[l])
        yp = _post(yp, mix, ln1_g[l], ln1_b[l], w_gate_up[l], w_down[l], ln2_g[l], ln2_b[l])
        kp_rows.append(kr[:, S - cw_p:])
        vp_rows.append(vr[:, S - cw_p:])
        q, k, v, u, vm = _project(ys, w_in[l], ln_v_g[l], ln_v_b[l])
        ao, kr, vr = _window_attn_sample(q, k, v, cache_win_k[l], cache_win_v[l], attn_sinks[l])
        co = _spatial_gate(u[:, None], vm[:, None], w_spatial[l], b_spatial[l]).reshape(Bd, L, D_CMLP)
        mix = _merge(ao, co, norm_attn_g[l], norm_cmlp_g[l], w_out[l])
        ys = _post(ys, mix, ln1_g[l], ln1_b[l], w_gate_up[l], w_down[l], ln2_g[l], ln2_b[l])
        ks_rows.append(kr)
        vs_rows.append(vr)
        ms_rows.append(vm)
    new_win_k_prompt = jnp.stack(kp_rows)
    new_win_v_prompt = jnp.stack(vp_rows)
    new_k_sample = jnp.stack(ks_rows)
    new_v_sample = jnp.stack(vs_rows)
    new_cmlp_v_sample = jnp.stack(ms_rows)
    return (yp, ys, new_win_k_prompt, new_win_v_prompt, new_k_sample, new_v_sample, new_cmlp_v_sample)
```

```python
import functools
import math

import jax
import jax.numpy as jnp
from jax import lax
from jax.experimental import pallas as pl
from jax.experimental.pallas import tpu as pltpu

D_MODEL = 1024
CHUNK = 64
HEAD_DIM = 64
D_ATTN = 512
D_CMLP = 512
N_HEADS = 8
N_KV_HEADS = 2
GQA_GROUP = 4
D_KV = 128
WINDOW = 128
ROPE_DIM = 16
ROPE_THETA = 500000.0
CMLP_BLOCK = 128
CMLP_GROUPS = 8
CMLP_GROUP_DIM = 64
D_IN = 1792
D_FF = 2816
PAST_LEN = 1024
ALPHA = 2.0 ** 0.25
LN_EPS = 1e-5
NEG_INF = -1e30
LANES = 128
WIN_KEYS = WINDOW + CHUNK
FF_CHUNK = 256
VMEM_LIMIT = 56 * 1024 * 1024

F32 = jnp.float32
BF16 = jnp.bfloat16


def _layernorm(x, g, b):
    mu = jnp.mean(x, -1, keepdims=True)
    d = x - mu
    var = jnp.mean(d * d, -1, keepdims=True)
    return d * lax.rsqrt(var + LN_EPS) * g + b


def _rmsnorm(x, g):
    ms = jnp.mean(x * x, -1, keepdims=True)
    return x * lax.rsqrt(ms + LN_EPS) * g


def _dot(a, b):
    return jnp.dot(a, b, preferred_element_type=F32)


def _rope(x, c, sa, sb):
    out = []
    for i in range(x.shape[1] // LANES):
        s = x[:, LANES * i:LANES * (i + 1)]
        up = pltpu.roll(s, LANES - ROPE_DIM // 2, 1)
        dn = pltpu.roll(s, ROPE_DIM // 2, 1)
        out.append(s * c + up * sa + dn * sb)
    return out[0] if len(out) == 1 else jnp.concatenate(out, axis=1)


def _mixer_body(sample, T, LB, *refs):
    if sample:
        (x_ref, c_ref, sa_ref, sb_ref, ck_ref, cv_ref, w_in_ref, lnv_g_ref, lnv_b_ref, sinks_ref, wpair_ref,
         bias_ref, nga_ref, ngc_ref, w_out_ref, ln1g_ref, ln1b_ref,
         h1_ref, k_out_ref, v_out_ref, vm_out_ref) = refs
    else:
        (x_ref, c_ref, sa_ref, sb_ref, w_in_ref, lnv_g_ref, lnv_b_ref, sinks_ref, wpair_ref,
         bias_ref, nga_ref, ngc_ref, w_out_ref, ln1g_ref, ln1b_ref,
         h1_ref, k_out_ref, v_out_ref, hk_ref, hv_ref) = refs
    n_chunks = T // CHUNK
    nb = T // LB

    x = x_ref[0] if not sample else x_ref[...]
    xb = x.astype(BF16)
    q = _dot(xb, w_in_ref[:, 0:D_ATTN])
    kv = _dot(xb, w_in_ref[:, D_ATTN:D_ATTN + 2 * D_KV])
    uv = jax.nn.gelu(_dot(xb, w_in_ref[:, D_ATTN + 2 * D_KV:D_IN]))

    c = c_ref[...]
    sa = sa_ref[...]
    sb = sb_ref[...]
    k = _rope(kv[:, :D_KV], c, sa, sb)
    v = kv[:, D_KV:]
    q = _rope(q, c, sa, sb) * (HEAD_DIM ** -0.5)

    u = uv[:, :D_CMLP]
    vm = _layernorm(uv[:, D_CMLP:], lnv_g_ref[...], lnv_b_ref[...])

    if sample:
        k_out_ref[...] = k
        v_out_ref[...] = v
        vm_out_ref[...] = vm
    else:
        j = pl.program_id(1)

        @pl.when(j == pl.num_programs(1) - 1)
        def _():
            k_out_ref[0] = k[T - WINDOW:, :]
            v_out_ref[0] = v[T - WINDOW:, :]

    lane = lax.broadcasted_iota(jnp.int32, (T, LANES), 1)
    lo_mask = lane < CMLP_GROUP_DIM
    gate_slabs = []
    for p in range(D_CMLP // LANES):
        slab = vm[:, LANES * p:LANES * (p + 1)]
        lo = jnp.where(lo_mask, slab, 0.0).astype(BF16)
        hi = jnp.where(lo_mask, 0.0, slab).astype(BF16)
        rhs = jnp.concatenate(
            [jnp.concatenate([lo[LB * b:LB * (b + 1)], hi[LB * b:LB * (b + 1)]], axis=0) for b in range(nb)], axis=1)
        o = _dot(wpair_ref[p], rhs)
        gate_slabs.append(jnp.concatenate([o[:, LANES * b:LANES * (b + 1)] for b in range(nb)], axis=0))
    bias = bias_ref[...]
    s_gate = jnp.concatenate(gate_slabs, axis=1) + jnp.concatenate([bias] * nb, axis=0)
    co = u * s_gate

    def rep(a):
        ar = pltpu.roll(a, HEAD_DIM, 1)
        first = lax.broadcasted_iota(jnp.int32, a.shape, 1) < HEAD_DIM
        g0 = jnp.where(first, a, ar).astype(BF16)
        g1 = jnp.where(first, ar, a).astype(BF16)
        return [jnp.concatenate([g0, g0], axis=1), jnp.concatenate([g1, g1], axis=1)]

    k_rep, v_rep = rep(k), rep(v)
    if sample:
        ck_rep, cv_rep = rep(ck_ref[...]), rep(cv_ref[...])

        def window(new, cache, g, ci):
            return jnp.concatenate([cache[g][WINDOW * ci:WINDOW * (ci + 1)], new[g][CHUNK * ci:CHUNK * (ci + 1)]], 0)
    else:
        rd = j % 2
        wr = 1 - rd

        @pl.when(j == 0)
        def _():
            hk_ref[0] = jnp.zeros(hk_ref.shape[1:], BF16)
            hv_ref[0] = jnp.zeros(hv_ref.shape[1:], BF16)

        k_all = [jnp.concatenate([hk_ref[rd, g], k_rep[g]], axis=0) for g in range(N_KV_HEADS)]
        v_all = [jnp.concatenate([hv_ref[rd, g], v_rep[g]], axis=0) for g in range(N_KV_HEADS)]
        for g in range(N_KV_HEADS):
            hk_ref[wr, g] = k_rep[g][T - WINDOW:]
            hv_ref[wr, g] = v_rep[g][T - WINDOW:]

    head_lane = lax.broadcasted_iota(jnp.int32, (CHUNK, GQA_GROUP * HEAD_DIM), 1) // HEAD_DIM
    key_idx = lax.broadcasted_iota(jnp.int32, (GQA_GROUP * CHUNK, WIN_KEYS), 1)
    ao_rows = []
    for ci in range(n_chunks):
        ao_g = []
        for g in range(N_KV_HEADS):
            qc = q[CHUNK * ci:CHUNK * (ci + 1), 256 * g:256 * (g + 1)]
            qm = jnp.concatenate([jnp.where(head_lane == h, qc, 0.0) for h in range(GQA_GROUP)], axis=0).astype(BF16)
            if sample:
                kw, vw = window(k_rep, ck_rep, g, ci), window(v_rep, cv_rep, g, ci)
            else:
                kw = k_all[g][CHUNK * ci:CHUNK * ci + WIN_KEYS]
                vw = v_all[g][CHUNK * ci:CHUNK * ci + WIN_KEYS]
            s = lax.dot_general(qm, kw, (((1,), (1,)), ((), ())), preferred_element_type=F32)
            if (not sample) and ci < WINDOW // CHUNK:
                first_valid = jnp.where(j == 0, (WINDOW // CHUNK - ci) * CHUNK, 0)
                s = jnp.where(key_idx >= first_valid, s, NEG_INF)
            ps = []
            for h in range(GQA_GROUP):
                sh = s[CHUNK * h:CHUNK * (h + 1)]
                sink = sinks_ref[GQA_GROUP * g + h]
                m = jnp.maximum(jnp.max(sh, -1, keepdims=True), sink)
                p = jnp.exp(sh - m)
                denom = jnp.sum(p, -1, keepdims=True) + jnp.exp(sink - m)
                ps.append(p * (1.0 / denom))
            pm = jnp.concatenate(ps, axis=0).astype(BF16)
            r = _dot(pm, vw)
            o = jnp.where(head_lane == 0, r[0:CHUNK], 0.0)
            for h in range(1, GQA_GROUP):
                o = o + jnp.where(head_lane == h, r[CHUNK * h:CHUNK * (h + 1)], 0.0)
            ao_g.append(o)
        ao_rows.append(jnp.concatenate(ao_g, axis=1))
    ao = jnp.concatenate(ao_rows, axis=0)

    an = _rmsnorm(ao, nga_ref[...]).astype(BF16)
    cn = _rmsnorm(co, ngc_ref[...]).astype(BF16)
    mix = _dot(an, w_out_ref[0:D_ATTN, :]) + _dot(cn, w_out_ref[D_ATTN:, :])
    h1 = _layernorm(ALPHA * x + mix, ln1g_ref[...], ln1b_ref[...])
    if sample:
        h1_ref[...] = h1
    else:
        h1_ref[0] = h1


def _ffn_body(h_ref, wgu_ref, wd_ref, g_ref, b_ref, y_ref, a_ref):
    h = h_ref[...]
    hb = h.astype(BF16)
    for ci in range(D_FF // FF_CHUNK):
        lo = ci * FF_CHUNK
        g = _dot(hb, wgu_ref[:, lo:lo + FF_CHUNK])
        u = _dot(hb, wgu_ref[:, D_FF + lo:D_FF + lo + FF_CHUNK])
        a_ref[:, lo:lo + FF_CHUNK] = (g * jax.nn.sigmoid(g) * u).astype(BF16)
    f = _dot(a_ref[...], wd_ref[...])
    y_ref[...] = _layernorm(ALPHA * h + f, g_ref[...], b_ref[...])


def _const_spec(shape):
    nd = len(shape)
    return pl.BlockSpec(shape, lambda *_: (0,) * nd)


def _ffn(h, wgu, wd, g, b, tm):
    m = h.shape[0]
    return pl.pallas_call(
        _ffn_body,
        grid=(m // tm,),
        in_specs=[pl.BlockSpec((tm, D_MODEL), lambda i: (i, 0)),
                  _const_spec(wgu.shape), _const_spec(wd.shape), _const_spec(g.shape), _const_spec(b.shape)],
        out_specs=pl.BlockSpec((tm, D_MODEL), lambda i: (i, 0)),
        out_shape=jax.ShapeDtypeStruct((m, D_MODEL), F32),
        scratch_shapes=[pltpu.VMEM((tm, D_FF), BF16)],
        compiler_params=pltpu.CompilerParams(dimension_semantics=("arbitrary",), vmem_limit_bytes=VMEM_LIMIT),
        name="ffn",
    )(h, wgu, wd, g, b)


def _rope_tables(pos):
    half = ROPE_DIM // 2
    inv = jnp.power(ROPE_THETA, -jnp.arange(half, dtype=F32) * (2.0 / ROPE_DIM))
    ang = pos.astype(F32)[:, None] * inv[None, :]
    cos, sin = jnp.cos(ang), jnp.sin(ang)
    n = pos.shape[0]
    pad = HEAD_DIM - ROPE_DIM
    c = jnp.concatenate([cos, cos, jnp.ones((n, pad), F32)], 1)
    sa = jnp.concatenate([-sin, jnp.zeros((n, half + pad), F32)], 1)
    sb = jnp.concatenate([jnp.zeros((n, half), F32), sin, jnp.zeros((n, pad), F32)], 1)
    rep = LANES // HEAD_DIM
    return jnp.tile(c, (1, rep)), jnp.tile(sa, (1, rep)), jnp.tile(sb, (1, rep))


def _gate_params(w_s, b_s, lb):
    i = jnp.arange(lb)
    mask = (i[None, :] // CHUNK) <= (i[:, None] // CHUNK)
    w = jnp.where(mask[None], w_s[:, :lb, :lb], 0.0)
    wpair = jnp.concatenate([w[0::2], w[1::2]], axis=2).astype(BF16)
    bias = jnp.repeat(jnp.transpose(b_s[:, :lb]), D_CMLP // CMLP_GROUPS, axis=1)
    return wpair, bias


def kernel(x_prompt, x_sample, cache_win_k, cache_win_v, w_in, ln_v_g, ln_v_b, attn_sinks, w_spatial, b_spatial,
           norm_attn_g, norm_cmlp_g, w_out, ln1_g, ln1_b, w_gate_up, w_down, ln2_g, ln2_b):
    B, S, _ = x_prompt.shape
    Bd, L, _ = x_sample.shape
    cw = cache_win_k.shape[2]
    assert cw == WINDOW and L == CHUNK and w_in.shape[0] == 1

    w_in_b = w_in[0].astype(BF16)
    w_out_b = w_out[0].astype(BF16)
    wgu_b = w_gate_up[0].astype(BF16)
    wd_b = w_down[0].astype(BF16)
    row = lambda a: a[0].reshape(1, -1)
    lnv_g, lnv_b = row(ln_v_g), row(ln_v_b)
    nga, ngc = row(norm_attn_g), row(norm_cmlp_g)
    l1g, l1b, l2g, l2b = row(ln1_g), row(ln1_b), row(ln2_g), row(ln2_b)
    sinks = attn_sinks[0]
    smem = pl.BlockSpec(memory_space=pltpu.SMEM)

    T = 256
    cp, sap, sbp = _rope_tables(jnp.arange(S))
    wpair_p, bias_p = _gate_params(w_spatial[0], b_spatial[0], CMLP_BLOCK)
    tab = pl.BlockSpec((T, LANES), lambda b, j: (j, 0))
    consts = [w_in_b, lnv_g, lnv_b]
    consts2 = [wpair_p, bias_p, nga, ngc, w_out_b, l1g, l1b]
    h1_p, kwin, vwin = pl.pallas_call(
        functools.partial(_mixer_body, False, T, CMLP_BLOCK),
        grid=(B, S // T),
        in_specs=[pl.BlockSpec((1, T, D_MODEL), lambda b, j: (b, j, 0)), tab, tab, tab]
        + [_const_spec(a.shape) for a in consts] + [smem] + [_const_spec(a.shape) for a in consts2],
        out_specs=[pl.BlockSpec((1, T, D_MODEL), lambda b, j: (b, j, 0)),
                   pl.BlockSpec((1, WINDOW, D_KV), lambda b, j: (b, 0, 0)),
                   pl.BlockSpec((1, WINDOW, D_KV), lambda b, j: (b, 0, 0))],
        out_shape=[jax.ShapeDtypeStruct((B, S, D_MODEL), F32),
                   jax.ShapeDtypeStruct((B, WINDOW, D_KV), F32),
                   jax.ShapeDtypeStruct((B, WINDOW, D_KV), F32)],
        scratch_shapes=[pltpu.VMEM((2, N_KV_HEADS, WINDOW, 2 * LANES), BF16),
                        pltpu.VMEM((2, N_KV_HEADS, WINDOW, 2 * LANES), BF16)],
        compiler_params=pltpu.CompilerParams(dimension_semantics=("arbitrary", "arbitrary"),
                                             vmem_limit_bytes=VMEM_LIMIT),
        name="mixer_prompt",
    )(x_prompt, cp, sap, sbp, *consts, sinks, *consts2)
    y_p = _ffn(h1_p.reshape(B * S, D_MODEL), wgu_b, wd_b, l2g, l2b, 512).reshape(B, S, D_MODEL)

    NS = 4
    Ts = NS * L
    cs, sas, sbs = _rope_tables(PAST_LEN + jnp.arange(L))
    cs, sas, sbs = (jnp.tile(t, (NS, 1)) for t in (cs, sas, sbs))
    wpair_s, bias_s = _gate_params(w_spatial[0], b_spatial[0], L)
    consts2s = [wpair_s, bias_s, nga, ngc, w_out_b, l1g, l1b]
    xs2 = x_sample.reshape(Bd * L, D_MODEL)
    ck2 = cache_win_k[0].reshape(Bd * WINDOW, D_KV)
    cv2 = cache_win_v[0].reshape(Bd * WINDOW, D_KV)
    rows = lambda n, w: pl.BlockSpec((n, w), lambda i: (i, 0))
    h1_s, k_s, v_s, vm_s = pl.pallas_call(
        functools.partial(_mixer_body, True, Ts, L),
        grid=(Bd // NS,),
        in_specs=[rows(Ts, D_MODEL), _const_spec(cs.shape), _const_spec(cs.shape), _const_spec(cs.shape),
                  rows(NS * WINDOW, D_KV), rows(NS * WINDOW, D_KV)]
        + [_const_spec(a.shape) for a in consts] + [smem] + [_const_spec(a.shape) for a in consts2s],
        out_specs=[rows(Ts, D_MODEL), rows(Ts, D_KV), rows(Ts, D_KV), rows(Ts, D_CMLP)],
        out_shape=[jax.ShapeDtypeStruct((Bd * L, D_MODEL), F32),
                   jax.ShapeDtypeStruct((Bd * L, D_KV), F32),
                   jax.ShapeDtypeStruct((Bd * L, D_KV), F32),
                   jax.ShapeDtypeStruct((Bd * L, D_CMLP), F32)],
        compiler_params=pltpu.CompilerParams(dimension_semantics=("arbitrary",), vmem_limit_bytes=VMEM_LIMIT),
        name="mixer_sample",
    )(xs2, cs, sas, sbs, ck2, cv2, *consts, sinks, *consts2s)
    y_s = _ffn(h1_s, wgu_b, wd_b, l2g, l2b, 512).reshape(Bd, L, D_MODEL)

    return (y_p, y_s,
            kwin.reshape(1, B, WINDOW, N_KV_HEADS, HEAD_DIM),
            vwin.reshape(1, B, WINDOW, N_KV_HEADS, HEAD_DIM),
            k_s.reshape(1, Bd, L, N_KV_HEADS, HEAD_DIM),
            v_s.reshape(1, Bd, L, N_KV_HEADS, HEAD_DIM),
            vm_s.reshape(1, Bd, L, CMLP_GROUPS, D_CMLP // CMLP_GROUPS))
```

```python
import collections
import functools

import jax
import jax.numpy as jnp
from jax import lax
from jax.experimental import pallas as pl
from jax.experimental.pallas import tpu as pltpu

D_MODEL = 1024
CHUNK = 64
HEAD_DIM = 64
D_ATTN = 512
D_CMLP = 512
N_HEADS = 8
N_KV_HEADS = 2
GQA_GROUP = 4
D_KV = 128
WINDOW = 128
ROPE_DIM = 16
ROPE_THETA = 500000.0
CMLP_BLOCK = 128
CMLP_GROUPS = 8
CMLP_GROUP_DIM = 64
D_IN = 1792
D_FF = 2816
PAST_LEN = 1024
ALPHA = 2.0 ** 0.25
LN_EPS = 1e-5
NEG_INF = -1e30
LANES = 128
WIN_KEYS = WINDOW + CHUNK
FF_CHUNK = 256
VMEM_LIMIT = 56 * 1024 * 1024
PROMPT_TILE = 512
SAMPLE_SEQS = 4

F32 = jnp.float32
BF16 = jnp.bfloat16


def _layernorm(x, g, b):
    mu = jnp.mean(x, -1, keepdims=True)
    d = x - mu
    var = jnp.mean(d * d, -1, keepdims=True)
    return d * lax.rsqrt(var + LN_EPS) * g + b


def _rmsnorm(x, g):
    ms = jnp.mean(x * x, -1, keepdims=True)
    return x * lax.rsqrt(ms + LN_EPS) * g


def _dot(a, b):
    return jnp.dot(a, b, preferred_element_type=F32)


def _rope(x, c, sa, sb):
    out = []
    for i in range(x.shape[1] // LANES):
        s = x[:, LANES * i:LANES * (i + 1)]
        up = pltpu.roll(s, LANES - ROPE_DIM // 2, 1)
        dn = pltpu.roll(s, ROPE_DIM // 2, 1)
        out.append(s * c + up * sa + dn * sb)
    return out[0] if len(out) == 1 else jnp.concatenate(out, axis=1)


def _rep4(a):
    ar = pltpu.roll(a, HEAD_DIM, 1)
    first = lax.broadcasted_iota(jnp.int32, a.shape, 1) < HEAD_DIM
    g0 = jnp.where(first, a, ar).astype(BF16)
    g1 = jnp.where(first, ar, a).astype(BF16)
    return [jnp.concatenate([g0, g0], axis=1), jnp.concatenate([g1, g1], axis=1)]


def _mixer(x, c, sa, sb, windows, first_valid, LB, w):
    T = x.shape[0]
    n_chunks = T // CHUNK
    nb = T // LB
    xb = x.astype(BF16)
    uv = jax.nn.gelu(_dot(xb, w.w_in[:, D_ATTN + 2 * D_KV:D_IN]))
    yield
    q = _dot(xb, w.w_in[:, 0:D_ATTN])
    kv = _dot(xb, w.w_in[:, D_ATTN:D_ATTN + 2 * D_KV])

    k = _rope(kv[:, :D_KV], c, sa, sb)
    v = kv[:, D_KV:]
    q = _rope(q, c, sa, sb) * (HEAD_DIM ** -0.5)

    yield
    u = uv[:, :D_CMLP]
    vm = _layernorm(uv[:, D_CMLP:], w.lnv_g[...], w.lnv_b[...])

    lo_mask = lax.broadcasted_iota(jnp.int32, (T, LANES), 1) < CMLP_GROUP_DIM
    gate_slabs = []
    for p in range(D_CMLP // LANES):
        slab = vm[:, LANES * p:LANES * (p + 1)]
        lo = jnp.where(lo_mask, slab, 0.0).astype(BF16)
        hi = jnp.where(lo_mask, 0.0, slab).astype(BF16)
        rhs = jnp.concatenate(
            [jnp.concatenate([lo[LB * b:LB * (b + 1)], hi[LB * b:LB * (b + 1)]], axis=0) for b in range(nb)], axis=1)
        o = _dot(w.wpair[p], rhs)
        gate_slabs.append(jnp.concatenate([o[:, LANES * b:LANES * (b + 1)] for b in range(nb)], axis=0))
    s_gate = jnp.concatenate(gate_slabs, axis=1) + jnp.concatenate([w.bias[...]] * nb, axis=0)
    co = u * s_gate
    yield

    k_rep, v_rep = _rep4(k), _rep4(v)
    head_lane = lax.broadcasted_iota(jnp.int32, (CHUNK, GQA_GROUP * HEAD_DIM), 1) // HEAD_DIM
    key_idx = lax.broadcasted_iota(jnp.int32, (GQA_GROUP * CHUNK, WIN_KEYS), 1)
    ao_rows = []
    for ci in range(n_chunks):
        ao_g = []
        for g in range(N_KV_HEADS):
            qc = q[CHUNK * ci:CHUNK * (ci + 1), 256 * g:256 * (g + 1)]
            qm = jnp.concatenate([jnp.where(head_lane == h, qc, 0.0) for h in range(GQA_GROUP)], axis=0).astype(BF16)
            kw, vw = windows(k_rep, v_rep, g, ci)
            s = lax.dot_general(qm, kw, (((1,), (1,)), ((), ())), preferred_element_type=F32)
            fv = first_valid(ci)
            if fv is not None:
                s = jnp.where(key_idx >= fv, s, NEG_INF)
            ps = []
            for h in range(GQA_GROUP):
                sh = s[CHUNK * h:CHUNK * (h + 1)]
                sink = w.sinks[GQA_GROUP * g + h]
                m = jnp.maximum(jnp.max(sh, -1, keepdims=True), sink)
                p = jnp.exp(sh - m)
                denom = jnp.sum(p, -1, keepdims=True) + jnp.exp(sink - m)
                ps.append(p * (1.0 / denom))
            pm = jnp.concatenate(ps, axis=0).astype(BF16)
            r = _dot(pm, vw)
            o = jnp.where(head_lane == 0, r[0:CHUNK], 0.0)
            for h in range(1, GQA_GROUP):
                o = o + jnp.where(head_lane == h, r[CHUNK * h:CHUNK * (h + 1)], 0.0)
            ao_g.append(o)
        ao_rows.append(jnp.concatenate(ao_g, axis=1))
        yield
    ao = jnp.concatenate(ao_rows, axis=0)

    an = _rmsnorm(ao, w.nga[...]).astype(BF16)
    cn = _rmsnorm(co, w.ngc[...]).astype(BF16)
    mix = _dot(an, w.w_out[0:D_ATTN, :]) + _dot(cn, w.w_out[D_ATTN:, :])
    h1 = _layernorm(ALPHA * x + mix, w.ln1g[...], w.ln1b[...])
    return h1, k, v, vm, k_rep, v_rep


def _ffn(h1, w):
    hb = h1.astype(BF16)
    acts = []
    for fi in range(D_FF // FF_CHUNK):
        lo_c = fi * FF_CHUNK
        gt = _dot(hb, w.wgu[:, lo_c:lo_c + FF_CHUNK])
        up = _dot(hb, w.wgu[:, D_FF + lo_c:D_FF + lo_c + FF_CHUNK])
        acts.append((gt * jax.nn.sigmoid(gt) * up).astype(BF16))
        yield
    f = _dot(jnp.concatenate(acts, axis=1), w.wd[...])
    return _layernorm(ALPHA * h1 + f, w.ln2g[...], w.ln2b[...])


def _run(*gens):
    results = [None] * len(gens)
    live = list(range(len(gens)))
    while live:
        for i in list(live):
            try:
                next(gens[i])
            except StopIteration as done:
                results[i] = done.value
                live.remove(i)
    return results


_WEIGHT_NAMES = ("w_in", "lnv_g", "lnv_b", "sinks", "wpair", "bias", "nga", "ngc", "w_out", "ln1g", "ln1b",
                 "wgu", "wd", "ln2g", "ln2b")
_Weights = collections.namedtuple("_Weights", _WEIGHT_NAMES)
N_W = len(_WEIGHT_NAMES)


def _prompt_body(T, tiles_per_seq, n_tiles, *refs):
    x_ref, c_ref, sa_ref, sb_ref = refs[:4]
    w = _Weights(*refs[4:4 + N_W])
    y_ref, k_out_ref, v_out_ref, hk_ref, hv_ref, h1_ref = refs[4 + N_W:]
    step = pl.program_id(0)
    j = jnp.minimum(step, n_tiles - 1) % tiles_per_seq
    rd = j % 2
    wr = 1 - rd
    cur = step % 2
    prev = 1 - cur

    @pl.when(step == 0)
    def _():
        h1_ref[1] = jnp.zeros(h1_ref.shape[1:], F32)

    @pl.when(j == 0)
    def _():
        hk_ref[0] = jnp.zeros(hk_ref.shape[1:], BF16)
        hv_ref[0] = jnp.zeros(hv_ref.shape[1:], BF16)

    hist_k = [hk_ref[rd, g] for g in range(N_KV_HEADS)]
    hist_v = [hv_ref[rd, g] for g in range(N_KV_HEADS)]

    def windows(k_rep, v_rep, g, ci):
        k_all = jnp.concatenate([hist_k[g], k_rep[g]], axis=0)
        v_all = jnp.concatenate([hist_v[g], v_rep[g]], axis=0)
        return k_all[CHUNK * ci:CHUNK * ci + WIN_KEYS], v_all[CHUNK * ci:CHUNK * ci + WIN_KEYS]

    def first_valid(ci):
        if ci >= WINDOW // CHUNK:
            return None
        return jnp.where(j == 0, (WINDOW // CHUNK - ci) * CHUNK, 0)

    (h1, k, v, _, k_rep, v_rep), y_prev = _run(
        _mixer(x_ref[...], c_ref[...], sa_ref[...], sb_ref[...], windows, first_valid, CMLP_BLOCK, w),
        _ffn(h1_ref[prev], w))
    y_ref[...] = y_prev
    h1_ref[cur] = h1
    k_out_ref[0] = k[T - WINDOW:, :]
    v_out_ref[0] = v[T - WINDOW:, :]
    for g in range(N_KV_HEADS):
        hk_ref[wr, g] = k_rep[g][T - WINDOW:]
        hv_ref[wr, g] = v_rep[g][T - WINDOW:]


def _sample_body(*refs):
    x_ref, c_ref, sa_ref, sb_ref, ck_ref, cv_ref = refs[:6]
    w = _Weights(*refs[6:6 + N_W])
    y_ref, k_out_ref, v_out_ref, vm_out_ref = refs[6 + N_W:]
    ck_rep, cv_rep = _rep4(ck_ref[...]), _rep4(cv_ref[...])

    def windows(k_rep, v_rep, g, ci):
        cat = lambda cache, new: jnp.concatenate(
            [cache[g][WINDOW * ci:WINDOW * (ci + 1)], new[g][CHUNK * ci:CHUNK * (ci + 1)]], axis=0)
        return cat(ck_rep, k_rep), cat(cv_rep, v_rep)

    (h1, k, v, vm, _, _), = _run(_mixer(x_ref[...], c_ref[...], sa_ref[...], sb_ref[...], windows, lambda ci: None,
                                        CHUNK, w))
    y_ref[...] = _run(_ffn(h1, w))[0]
    k_out_ref[...] = k
    v_out_ref[...] = v
    vm_out_ref[...] = vm


def _const_spec(shape):
    nd = len(shape)
    return pl.BlockSpec(shape, lambda *_: (0,) * nd)


def _weight_specs(ws):
    smem = pl.BlockSpec(memory_space=pltpu.SMEM)
    return [smem if name == "sinks" else _const_spec(a.shape) for name, a in zip(_WEIGHT_NAMES, ws)]


def _rope_tables(pos):
    half = ROPE_DIM // 2
    inv = jnp.power(ROPE_THETA, -jnp.arange(half, dtype=F32) * (2.0 / ROPE_DIM))
    ang = pos.astype(F32)[:, None] * inv[None, :]
    cos, sin = jnp.cos(ang), jnp.sin(ang)
    n = pos.shape[0]
    pad = HEAD_DIM - ROPE_DIM
    c = jnp.concatenate([cos, cos, jnp.ones((n, pad), F32)], 1)
    sa = jnp.concatenate([-sin, jnp.zeros((n, half + pad), F32)], 1)
    sb = jnp.concatenate([jnp.zeros((n, half), F32), sin, jnp.zeros((n, pad), F32)], 1)
    rep = LANES // HEAD_DIM
    return jnp.tile(c, (1, rep)), jnp.tile(sa, (1, rep)), jnp.tile(sb, (1, rep))


def _gate_params(w_s, b_s, lb):
    i = jnp.arange(lb)
    mask = (i[None, :] // CHUNK) <= (i[:, None] // CHUNK)
    w = jnp.where(mask[None], w_s[:, :lb, :lb], 0.0)
    wpair = jnp.concatenate([w[0::2], w[1::2]], axis=2).astype(BF16)
    bias = jnp.repeat(jnp.transpose(b_s[:, :lb]), CMLP_GROUP_DIM, axis=1)
    return wpair, bias


def kernel(x_prompt, x_sample, cache_win_k, cache_win_v, w_in, ln_v_g, ln_v_b, attn_sinks, w_spatial, b_spatial,
           norm_attn_g, norm_cmlp_g, w_out, ln1_g, ln1_b, w_gate_up, w_down, ln2_g, ln2_b):
    B, S, _ = x_prompt.shape
    Bd, L, _ = x_sample.shape
    assert cache_win_k.shape[2] == WINDOW and L == CHUNK and w_in.shape[0] == 1

    row = lambda a: a[0].reshape(1, -1)

    def weights(lb):
        wpair, bias = _gate_params(w_spatial[0], b_spatial[0], lb)
        return _Weights(w_in=w_in[0].astype(BF16), lnv_g=row(ln_v_g), lnv_b=row(ln_v_b), sinks=attn_sinks[0],
                        wpair=wpair, bias=bias, nga=row(norm_attn_g), ngc=row(norm_cmlp_g),
                        w_out=w_out[0].astype(BF16), ln1g=row(ln1_g), ln1b=row(ln1_b),
                        wgu=w_gate_up[0].astype(BF16), wd=w_down[0].astype(BF16), ln2g=row(ln2_g), ln2b=row(ln2_b))

    T = PROMPT_TILE
    tps = S // T
    n_tiles = B * tps
    wp = weights(CMLP_BLOCK)
    tables = _rope_tables(jnp.arange(S))
    mix_tile = lambda s: jnp.minimum(s, n_tiles - 1)
    ffn_tile = lambda s: jnp.maximum(s - 1, 0)
    tab = pl.BlockSpec((T, LANES), lambda s: (mix_tile(s) % tps, 0))
    win = pl.BlockSpec((1, WINDOW, D_KV), lambda s: (mix_tile(s) // tps, 0, 0))
    y_p, kwin, vwin = pl.pallas_call(
        functools.partial(_prompt_body, T, tps, n_tiles),
        grid=(n_tiles + 1,),
        in_specs=[pl.BlockSpec((T, D_MODEL), lambda s: (mix_tile(s), 0)), tab, tab, tab] + _weight_specs(wp),
        out_specs=[pl.BlockSpec((T, D_MODEL), lambda s: (ffn_tile(s), 0)), win, win],
        out_shape=[jax.ShapeDtypeStruct((B * S, D_MODEL), F32),
                   jax.ShapeDtypeStruct((B, WINDOW, D_KV), F32),
                   jax.ShapeDtypeStruct((B, WINDOW, D_KV), F32)],
        scratch_shapes=[pltpu.VMEM((2, N_KV_HEADS, WINDOW, 2 * LANES), BF16),
                        pltpu.VMEM((2, N_KV_HEADS, WINDOW, 2 * LANES), BF16),
                        pltpu.VMEM((2, T, D_MODEL), F32)],
        compiler_params=pltpu.CompilerParams(dimension_semantics=("arbitrary",), vmem_limit_bytes=VMEM_LIMIT),
        name="layer_prompt",
    )(x_prompt.reshape(B * S, D_MODEL), *tables, *wp)
    y_p = y_p.reshape(B, S, D_MODEL)

    NS = SAMPLE_SEQS
    Ts = NS * L
    ws = weights(L)
    tables_s = [jnp.tile(t, (NS, 1)) for t in _rope_tables(PAST_LEN + jnp.arange(L))]
    rows = lambda n, width: pl.BlockSpec((n, width), lambda i: (i, 0))
    y_s, k_s, v_s, vm_s = pl.pallas_call(
        _sample_body,
        grid=(Bd // NS,),
        in_specs=[rows(Ts, D_MODEL)] + [_const_spec(t.shape) for t in tables_s]
        + [rows(NS * WINDOW, D_KV), rows(NS * WINDOW, D_KV)] + _weight_specs(ws),
        out_specs=[rows(Ts, D_MODEL), rows(Ts, D_KV), rows(Ts, D_KV), rows(Ts, D_CMLP)],
        out_shape=[jax.ShapeDtypeStruct((Bd * L, D_MODEL), F32),
                   jax.ShapeDtypeStruct((Bd * L, D_KV), F32),
                   jax.ShapeDtypeStruct((Bd * L, D_KV), F32),
                   jax.ShapeDtypeStruct((Bd * L, D_CMLP), F32)],
        compiler_params=pltpu.CompilerParams(dimension_semantics=("arbitrary",), vmem_limit_bytes=VMEM_LIMIT),
        name="layer_sample",
    )(x_sample.reshape(Bd * L, D_MODEL), *tables_s,
      cache_win_k[0].reshape(Bd * WINDOW, D_KV), cache_win_v[0].reshape(Bd * WINDOW, D_KV), *ws)

    return (y_p, y_s.reshape(Bd, L, D_MODEL),
            kwin.reshape(1, B, WINDOW, N_KV_HEADS, HEAD_DIM),
            vwin.reshape(1, B, WINDOW, N_KV_HEADS, HEAD_DIM),
            k_s.reshape(1, Bd, L, N_KV_HEADS, HEAD_DIM),
            v_s.reshape(1, Bd, L, N_KV_HEADS, HEAD_DIM),
            vm_s.reshape(1, Bd, L, CMLP_GROUPS, CMLP_GROUP_DIM))
```

```python
import collections
import functools

import jax
import jax.numpy as jnp
from jax import lax
from jax.experimental import pallas as pl
from jax.experimental.pallas import tpu as pltpu

D_MODEL = 1024
CHUNK = 64
HEAD_DIM = 64
D_ATTN = 512
D_CMLP = 512
N_HEADS = 8
N_KV_HEADS = 2
GQA_GROUP = 4
D_KV = 128
WINDOW = 128
ROPE_DIM = 16
ROPE_THETA = 500000.0
CMLP_BLOCK = 128
CMLP_GROUPS = 8
CMLP_GROUP_DIM = 64
D_IN = 1792
D_FF = 2816
PAST_LEN = 1024
ALPHA = 2.0 ** 0.25
LN_EPS = 1e-5
NEG_INF = -1e30
LANES = 128
WIN_KEYS = WINDOW + CHUNK
FF_CHUNK = 256
VMEM_LIMIT = 56 * 1024 * 1024
PROMPT_TILE = 512
SAMPLE_SEQS = 4

F32 = jnp.float32
BF16 = jnp.bfloat16


def _layernorm(x, g, b):
    mu = jnp.mean(x, -1, keepdims=True)
    d = x - mu
    var = jnp.mean(d * d, -1, keepdims=True)
    return d * lax.rsqrt(var + LN_EPS) * g + b


def _rmsnorm(x, g):
    ms = jnp.mean(x * x, -1, keepdims=True)
    return x * lax.rsqrt(ms + LN_EPS) * g


def _dot(a, b):
    return jnp.dot(a, b, preferred_element_type=F32)


def _rope(x, c, sa, sb):
    out = []
    for i in range(x.shape[1] // LANES):
        s = x[:, LANES * i:LANES * (i + 1)]
        up = pltpu.roll(s, LANES - ROPE_DIM // 2, 1)
        dn = pltpu.roll(s, ROPE_DIM // 2, 1)
        out.append(s * c + up * sa + dn * sb)
    return out[0] if len(out) == 1 else jnp.concatenate(out, axis=1)


def _row_halves(n):
    return (slice(0, n // 2), slice(n // 2, n))


def _rep4(a):
    ar = pltpu.roll(a, HEAD_DIM, 1)
    first = lax.broadcasted_iota(jnp.int32, a.shape, 1) < HEAD_DIM
    g0 = jnp.where(first, a, ar).astype(BF16)
    g1 = jnp.where(first, ar, a).astype(BF16)
    return [jnp.concatenate([g0, g0], axis=1), jnp.concatenate([g1, g1], axis=1)]


def _mixer(x, c, sa, sb, windows, first_valid, LB, w):
    T = x.shape[0]
    n_chunks = T // CHUNK
    nb = T // LB
    xb = x.astype(BF16)
    uv = jax.nn.gelu(_dot(xb, w.w_in[:, D_ATTN + 2 * D_KV:D_IN]))
    yield
    q = _dot(xb, w.w_in[:, 0:D_ATTN])
    kv = _dot(xb, w.w_in[:, D_ATTN:D_ATTN + 2 * D_KV])

    k = _rope(kv[:, :D_KV], c, sa, sb)
    v = kv[:, D_KV:]
    q = _rope(q, c, sa, sb) * (HEAD_DIM ** -0.5)

    yield
    u = uv[:, :D_CMLP]
    vm = _layernorm(uv[:, D_CMLP:], w.lnv_g[...], w.lnv_b[...])

    lo_mask = lax.broadcasted_iota(jnp.int32, (T, LANES), 1) < CMLP_GROUP_DIM
    gate_slabs = []
    for p in range(D_CMLP // LANES):
        slab = vm[:, LANES * p:LANES * (p + 1)]
        lo = jnp.where(lo_mask, slab, 0.0).astype(BF16)
        hi = jnp.where(lo_mask, 0.0, slab).astype(BF16)
        rhs = jnp.concatenate(
            [jnp.concatenate([lo[LB * b:LB * (b + 1)], hi[LB * b:LB * (b + 1)]], axis=0) for b in range(nb)], axis=1)
        o = _dot(w.wpair[p], rhs)
        gate_slabs.append(jnp.concatenate([o[:, LANES * b:LANES * (b + 1)] for b in range(nb)], axis=0))
    s_gate = jnp.concatenate(gate_slabs, axis=1) + jnp.concatenate([w.bias[...]] * nb, axis=0)
    co = u * s_gate
    yield

    k_rep, v_rep = _rep4(k), _rep4(v)
    head_lane = lax.broadcasted_iota(jnp.int32, (CHUNK, GQA_GROUP * HEAD_DIM), 1) // HEAD_DIM
    key_idx = lax.broadcasted_iota(jnp.int32, (GQA_GROUP * CHUNK, WIN_KEYS), 1)
    ao_rows = []
    for ci in range(n_chunks):
        ao_g = []
        for g in range(N_KV_HEADS):
            qc = q[CHUNK * ci:CHUNK * (ci + 1), 256 * g:256 * (g + 1)]
            qm = jnp.concatenate([jnp.where(head_lane == h, qc, 0.0) for h in range(GQA_GROUP)], axis=0).astype(BF16)
            kw, vw = windows(k_rep, v_rep, g, ci)
            s = lax.dot_general(qm, kw, (((1,), (1,)), ((), ())), preferred_element_type=F32)
            fv = first_valid(ci)
            if fv is not None:
                s = jnp.where(key_idx >= fv, s, NEG_INF)
            ps = []
            for h in range(GQA_GROUP):
                sh = s[CHUNK * h:CHUNK * (h + 1)]
                sink = w.sinks[GQA_GROUP * g + h]
                m = jnp.maximum(jnp.max(sh, -1, keepdims=True), sink)
                p = jnp.exp(sh - m)
                denom = jnp.sum(p, -1, keepdims=True) + jnp.exp(sink - m)
                ps.append(p * (1.0 / denom))
            pm = jnp.concatenate(ps, axis=0).astype(BF16)
            r = _dot(pm, vw)
            o = jnp.where(head_lane == 0, r[0:CHUNK], 0.0)
            for h in range(1, GQA_GROUP):
                o = o + jnp.where(head_lane == h, r[CHUNK * h:CHUNK * (h + 1)], 0.0)
            ao_g.append(o)
        ao_rows.append(jnp.concatenate(ao_g, axis=1))
        yield
    ao = jnp.concatenate(ao_rows, axis=0)

    an = _rmsnorm(ao, w.nga[...]).astype(BF16)
    cn = _rmsnorm(co, w.ngc[...]).astype(BF16)
    h1 = []
    for r in _row_halves(T):
        mix = _dot(an[r], w.w_out[0:D_ATTN, :]) + _dot(cn[r], w.w_out[D_ATTN:, :])
        yield
        h1.append(_layernorm(ALPHA * x[r] + mix, w.ln1g[...], w.ln1b[...]))
    return jnp.concatenate(h1, axis=0), k, v, vm, k_rep, v_rep


def _ffn(h1, w):
    hb = h1.astype(BF16)
    acts = []
    for lo_c in range(0, D_FF, FF_CHUNK):
        hi_c = min(lo_c + FF_CHUNK, D_FF)
        gt = _dot(hb, w.wgu[:, lo_c:hi_c])
        up = _dot(hb, w.wgu[:, D_FF + lo_c:D_FF + hi_c])
        acts.append((gt * jax.nn.sigmoid(gt) * up).astype(BF16))
        yield
    act = jnp.concatenate(acts, axis=1)
    y = []
    for r in _row_halves(h1.shape[0]):
        f = _dot(act[r], w.wd[...])
        yield
        y.append(_layernorm(ALPHA * h1[r] + f, w.ln2g[...], w.ln2b[...]))
    return jnp.concatenate(y, axis=0)


def _run(*gens):
    results = [None] * len(gens)
    live = list(range(len(gens)))
    while live:
        for i in list(live):
            try:
                next(gens[i])
            except StopIteration as done:
                results[i] = done.value
                live.remove(i)
    return results


_WEIGHT_NAMES = ("w_in", "lnv_g", "lnv_b", "sinks", "wpair", "bias", "nga", "ngc", "w_out", "ln1g", "ln1b",
                 "wgu", "wd", "ln2g", "ln2b")
_Weights = collections.namedtuple("_Weights", _WEIGHT_NAMES)
N_W = len(_WEIGHT_NAMES)


def _prompt_body(T, tiles_per_seq, n_tiles, *refs):
    x_ref, c_ref, sa_ref, sb_ref = refs[:4]
    w = _Weights(*refs[4:4 + N_W])
    y_ref, k_out_ref, v_out_ref, hk_ref, hv_ref, h1_ref = refs[4 + N_W:]
    step = pl.program_id(0)
    j = jnp.minimum(step, n_tiles - 1) % tiles_per_seq
    rd = j % 2
    wr = 1 - rd
    cur = step % 2
    prev = 1 - cur

    @pl.when(step == 0)
    def _():
        h1_ref[1] = jnp.zeros(h1_ref.shape[1:], F32)

    @pl.when(j == 0)
    def _():
        hk_ref[0] = jnp.zeros(hk_ref.shape[1:], BF16)
        hv_ref[0] = jnp.zeros(hv_ref.shape[1:], BF16)

    hist_k = [hk_ref[rd, g] for g in range(N_KV_HEADS)]
    hist_v = [hv_ref[rd, g] for g in range(N_KV_HEADS)]

    def windows(k_rep, v_rep, g, ci):
        k_all = jnp.concatenate([hist_k[g], k_rep[g]], axis=0)
        v_all = jnp.concatenate([hist_v[g], v_rep[g]], axis=0)
        return k_all[CHUNK * ci:CHUNK * ci + WIN_KEYS], v_all[CHUNK * ci:CHUNK * ci + WIN_KEYS]

    def first_valid(ci):
        if ci >= WINDOW // CHUNK:
            return None
        return jnp.where(j == 0, (WINDOW // CHUNK - ci) * CHUNK, 0)

    (h1, k, v, _, k_rep, v_rep), y_prev = _run(
        _mixer(x_ref[...], c_ref[...], sa_ref[...], sb_ref[...], windows, first_valid, CMLP_BLOCK, w),
        _ffn(h1_ref[prev], w))
    y_ref[...] = y_prev
    h1_ref[cur] = h1
    k_out_ref[0] = k[T - WINDOW:, :]
    v_out_ref[0] = v[T - WINDOW:, :]
    for g in range(N_KV_HEADS):
        hk_ref[wr, g] = k_rep[g][T - WINDOW:]
        hv_ref[wr, g] = v_rep[g][T - WINDOW:]


def _cache_rep4(ref):
    n = ref.shape[0] // N_KV_HEADS
    out = []
    for g in range(N_KV_HEADS):
        a = ref[pl.ds(g, n, stride=N_KV_HEADS), :]
        a2 = jnp.concatenate([a, a], axis=1).astype(BF16)
        out.append(jnp.concatenate([a2, a2], axis=1))
    return out


def _sample_body(*refs):
    x_ref, c_ref, sa_ref, sb_ref, ck_ref, cv_ref = refs[:6]
    w = _Weights(*refs[6:6 + N_W])
    y_ref, k_out_ref, v_out_ref, vm_out_ref, h1_ref = refs[6 + N_W:]
    step = pl.program_id(0)
    cur = step % 2
    prev = 1 - cur

    @pl.when(step == 0)
    def _():
        h1_ref[1] = jnp.zeros(h1_ref.shape[1:], F32)

    ck_rep, cv_rep = _cache_rep4(ck_ref), _cache_rep4(cv_ref)

    def windows(k_rep, v_rep, g, ci):
        cat = lambda cache, new: jnp.concatenate(
            [cache[g][WINDOW * ci:WINDOW * (ci + 1)], new[g][CHUNK * ci:CHUNK * (ci + 1)]], axis=0)
        return cat(ck_rep, k_rep), cat(cv_rep, v_rep)

    (h1, k, v, vm, _, _), y_prev = _run(
        _mixer(x_ref[...], c_ref[...], sa_ref[...], sb_ref[...], windows, lambda ci: None, CHUNK, w),
        _ffn(h1_ref[prev], w))
    y_ref[...] = y_prev
    h1_ref[cur] = h1
    k_out_ref[...] = k
    v_out_ref[...] = v
    vm_out_ref[...] = vm


def _const_spec(shape):
    nd = len(shape)
    return pl.BlockSpec(shape, lambda *_: (0,) * nd)


def _weight_specs(ws):
    smem = pl.BlockSpec(memory_space=pltpu.SMEM)
    return [smem if name == "sinks" else _const_spec(a.shape) for name, a in zip(_WEIGHT_NAMES, ws)]


def _rope_tables(pos):
    half = ROPE_DIM // 2
    inv = jnp.power(ROPE_THETA, -jnp.arange(half, dtype=F32) * (2.0 / ROPE_DIM))
    ang = pos.astype(F32)[:, None] * inv[None, :]
    cos, sin = jnp.cos(ang), jnp.sin(ang)
    n = pos.shape[0]
    pad = HEAD_DIM - ROPE_DIM
    c = jnp.concatenate([cos, cos, jnp.ones((n, pad), F32)], 1)
    sa = jnp.concatenate([-sin, jnp.zeros((n, half + pad), F32)], 1)
    sb = jnp.concatenate([jnp.zeros((n, half), F32), sin, jnp.zeros((n, pad), F32)], 1)
    rep = LANES // HEAD_DIM
    return jnp.tile(c, (1, rep)), jnp.tile(sa, (1, rep)), jnp.tile(sb, (1, rep))


def _gate_params(w_s, b_s, lb):
    i = jnp.arange(lb)
    mask = (i[None, :] // CHUNK) <= (i[:, None] // CHUNK)
    w = jnp.where(mask[None], w_s[:, :lb, :lb], 0.0)
    wpair = jnp.concatenate([w[0::2], w[1::2]], axis=2).astype(BF16)
    bias = jnp.repeat(jnp.transpose(b_s[:, :lb]), CMLP_GROUP_DIM, axis=1)
    return wpair, bias


def kernel(x_prompt, x_sample, cache_win_k, cache_win_v, w_in, ln_v_g, ln_v_b, attn_sinks, w_spatial, b_spatial,
           norm_attn_g, norm_cmlp_g, w_out, ln1_g, ln1_b, w_gate_up, w_down, ln2_g, ln2_b):
    B, S, _ = x_prompt.shape
    Bd, L, _ = x_sample.shape
    assert cache_win_k.shape[2] == WINDOW and L == CHUNK and w_in.shape[0] == 1

    row = lambda a: a[0].reshape(1, -1)

    def weights(lb):
        wpair, bias = _gate_params(w_spatial[0], b_spatial[0], lb)
        return _Weights(w_in=w_in[0].astype(BF16), lnv_g=row(ln_v_g), lnv_b=row(ln_v_b), sinks=attn_sinks[0],
                        wpair=wpair, bias=bias, nga=row(norm_attn_g), ngc=row(norm_cmlp_g),
                        w_out=w_out[0].astype(BF16), ln1g=row(ln1_g), ln1b=row(ln1_b),
                        wgu=w_gate_up[0].astype(BF16), wd=w_down[0].astype(BF16), ln2g=row(ln2_g), ln2b=row(ln2_b))

    T = PROMPT_TILE
    tps = S // T
    n_tiles = B * tps
    wp = weights(CMLP_BLOCK)
    tables = _rope_tables(jnp.arange(S))
    mix_tile = lambda s: jnp.minimum(s, n_tiles - 1)
    ffn_tile = lambda s: jnp.maximum(s - 1, 0)
    tab = pl.BlockSpec((T, LANES), lambda s: (mix_tile(s) % tps, 0))
    win = pl.BlockSpec((1, WINDOW, D_KV), lambda s: (mix_tile(s) // tps, 0, 0))
    y_p, kwin, vwin = pl.pallas_call(
        functools.partial(_prompt_body, T, tps, n_tiles),
        grid=(n_tiles + 1,),
        in_specs=[pl.BlockSpec((T, D_MODEL), lambda s: (mix_tile(s), 0)), tab, tab, tab] + _weight_specs(wp),
        out_specs=[pl.BlockSpec((T, D_MODEL), lambda s: (ffn_tile(s), 0)), win, win],
        out_shape=[jax.ShapeDtypeStruct((B * S, D_MODEL), F32),
                   jax.ShapeDtypeStruct((B, WINDOW, D_KV), F32),
                   jax.ShapeDtypeStruct((B, WINDOW, D_KV), F32)],
        scratch_shapes=[pltpu.VMEM((2, N_KV_HEADS, WINDOW, 2 * LANES), BF16),
                        pltpu.VMEM((2, N_KV_HEADS, WINDOW, 2 * LANES), BF16),
                        pltpu.VMEM((2, T, D_MODEL), F32)],
        compiler_params=pltpu.CompilerParams(dimension_semantics=("arbitrary",), vmem_limit_bytes=VMEM_LIMIT),
        name="layer_prompt",
    )(x_prompt.reshape(B * S, D_MODEL), *tables, *wp)
    y_p = y_p.reshape(B, S, D_MODEL)

    NS = SAMPLE_SEQS
    Ts = NS * L
    ws = weights(L)
    tables_s = [jnp.tile(t, (NS, 1)) for t in _rope_tables(PAST_LEN + jnp.arange(L))]
    n_st = Bd // NS
    mix_rows = lambda n, width: pl.BlockSpec((n, width), lambda s: (jnp.minimum(s, n_st - 1), 0))
    cache_rows = NS * WINDOW * N_KV_HEADS
    y_s, k_s, v_s, vm_s = pl.pallas_call(
        _sample_body,
        grid=(n_st + 1,),
        in_specs=[mix_rows(Ts, D_MODEL)] + [_const_spec(t.shape) for t in tables_s]
        + [mix_rows(cache_rows, HEAD_DIM), mix_rows(cache_rows, HEAD_DIM)] + _weight_specs(ws),
        out_specs=[pl.BlockSpec((Ts, D_MODEL), lambda s: (jnp.maximum(s - 1, 0), 0)),
                   mix_rows(Ts, D_KV), mix_rows(Ts, D_KV), mix_rows(Ts, D_CMLP)],
        out_shape=[jax.ShapeDtypeStruct((Bd * L, D_MODEL), F32),
                   jax.ShapeDtypeStruct((Bd * L, D_KV), F32),
                   jax.ShapeDtypeStruct((Bd * L, D_KV), F32),
                   jax.ShapeDtypeStruct((Bd * L, D_CMLP), F32)],
        scratch_shapes=[pltpu.VMEM((2, Ts, D_MODEL), F32)],
        compiler_params=pltpu.CompilerParams(dimension_semantics=("arbitrary",), vmem_limit_bytes=VMEM_LIMIT),
        name="layer_sample",
    )(x_sample.reshape(Bd * L, D_MODEL), *tables_s,
      cache_win_k.reshape(Bd * WINDOW * N_KV_HEADS, HEAD_DIM),
      cache_win_v.reshape(Bd * WINDOW * N_KV_HEADS, HEAD_DIM), *ws)

    return (y_p, y_s.reshape(Bd, L, D_MODEL),
            kwin.reshape(1, B, WINDOW, N_KV_HEADS, HEAD_DIM),
            vwin.reshape(1, B, WINDOW, N_KV_HEADS, HEAD_DIM),
            k_s.reshape(1, Bd, L, N_KV_HEADS, HEAD_DIM),
            v_s.reshape(1, Bd, L, N_KV_HEADS, HEAD_DIM),
            vm_s.reshape(1, Bd, L, CMLP_GROUPS, CMLP_GROUP_DIM))
```

```python
import collections
import functools

import jax
import jax.numpy as jnp
import numpy as np
from jax import lax
from jax.experimental import pallas as pl
from jax.experimental.pallas import tpu as pltpu

D_MODEL = 1024
CHUNK = 64
HEAD_DIM = 64
D_ATTN = 512
D_CMLP = 512
N_HEADS = 8
N_KV_HEADS = 2
GQA_GROUP = 4
D_KV = 128
WINDOW = 128
ROPE_DIM = 16
ROPE_THETA = 500000.0
CMLP_BLOCK = 128
CMLP_GROUPS = 8
CMLP_GROUP_DIM = 64
D_IN = 1792
D_FF = 2816
PAST_LEN = 1024
ALPHA = 2.0 ** 0.25
LN_EPS = 1e-5
NEG_INF = -1e30
LANES = 128
WIN_KEYS = WINDOW + CHUNK
FF_CHUNK = 256
VMEM_LIMIT = 56 * 1024 * 1024
PROMPT_TILE = 512
SAMPLE_SEQS = 4

F32 = jnp.float32
BF16 = jnp.bfloat16


def _layernorm(x, g, b):
    mu = jnp.mean(x, -1, keepdims=True)
    d = x - mu
    var = jnp.mean(d * d, -1, keepdims=True)
    return d * lax.rsqrt(var + LN_EPS) * g + b


def _rmsnorm(x, g):
    ms = jnp.mean(x * x, -1, keepdims=True)
    return x * lax.rsqrt(ms + LN_EPS) * g


def _dot(a, b):
    return jnp.dot(a, b, preferred_element_type=F32)


def _rope(x, c, sa, sb):
    out = []
    for i in range(x.shape[1] // LANES):
        s = x[:, LANES * i:LANES * (i + 1)]
        up = pltpu.roll(s, LANES - ROPE_DIM // 2, 1)
        dn = pltpu.roll(s, ROPE_DIM // 2, 1)
        out.append(s * c + up * sa + dn * sb)
    return out[0] if len(out) == 1 else jnp.concatenate(out, axis=1)


def _row_halves(n):
    return (slice(0, n // 2), slice(n // 2, n))


def _rep4(a):
    ar = pltpu.roll(a, HEAD_DIM, 1)
    first = lax.broadcasted_iota(jnp.int32, a.shape, 1) < HEAD_DIM
    g0 = jnp.where(first, a, ar).astype(BF16)
    g1 = jnp.where(first, ar, a).astype(BF16)
    return [jnp.concatenate([g0, g0], axis=1), jnp.concatenate([g1, g1], axis=1)]


def _mixer(x, c, sa, sb, windows, first_valid, LB, w):
    T = x.shape[0]
    n_chunks = T // CHUNK
    nb = T // LB
    xb = x.astype(BF16)
    uv = jax.nn.gelu(_dot(xb, w.w_in[:, D_ATTN + 2 * D_KV:D_IN]))
    yield
    q = _dot(xb, w.w_in[:, 0:D_ATTN])
    kv = _dot(xb, w.w_in[:, D_ATTN:D_ATTN + 2 * D_KV])

    k = _rope(kv[:, :D_KV], c, sa, sb)
    v = kv[:, D_KV:]
    q = _rope(q, c, sa, sb) * (HEAD_DIM ** -0.5)

    yield
    u = uv[:, :D_CMLP]
    vm = _layernorm(uv[:, D_CMLP:], w.lnv_g[...], w.lnv_b[...])

    lo_mask = lax.broadcasted_iota(jnp.int32, (T, LANES), 1) < CMLP_GROUP_DIM
    gate_slabs = []
    for p in range(D_CMLP // LANES):
        slab = vm[:, LANES * p:LANES * (p + 1)]
        lo = jnp.where(lo_mask, slab, 0.0).astype(BF16)
        hi = jnp.where(lo_mask, 0.0, slab).astype(BF16)
        rhs = jnp.concatenate(
            [jnp.concatenate([lo[LB * b:LB * (b + 1)], hi[LB * b:LB * (b + 1)]], axis=0) for b in range(nb)], axis=1)
        o = _dot(w.wpair[p], rhs)
        gate_slabs.append(jnp.concatenate([o[:, LANES * b:LANES * (b + 1)] for b in range(nb)], axis=0))
    s_gate = jnp.concatenate(gate_slabs, axis=1) + jnp.concatenate([w.bias[...]] * nb, axis=0)
    co = u * s_gate
    yield

    k_rep, v_rep = _rep4(k), _rep4(v)
    head_lane = lax.broadcasted_iota(jnp.int32, (CHUNK, GQA_GROUP * HEAD_DIM), 1) // HEAD_DIM
    key_idx = lax.broadcasted_iota(jnp.int32, (GQA_GROUP * CHUNK, WIN_KEYS), 1)
    ao_rows = []
    for ci in range(n_chunks):
        ao_g = []
        for g in range(N_KV_HEADS):
            qc = q[CHUNK * ci:CHUNK * (ci + 1), 256 * g:256 * (g + 1)]
            qm = jnp.concatenate([jnp.where(head_lane == h, qc, 0.0) for h in range(GQA_GROUP)], axis=0).astype(BF16)
            kw, vw = windows(k_rep, v_rep, g, ci)
            s = lax.dot_general(qm, kw, (((1,), (1,)), ((), ())), preferred_element_type=F32)
            fv = first_valid(ci)
            if fv is not None:
                s = jnp.where(key_idx >= fv, s, NEG_INF)
            ps = []
            for h in range(GQA_GROUP):
                sh = s[CHUNK * h:CHUNK * (h + 1)]
                sink = w.sinks[GQA_GROUP * g + h]
                m = jnp.maximum(jnp.max(sh, -1, keepdims=True), sink)
                p = jnp.exp(sh - m)
                denom = jnp.sum(p, -1, keepdims=True) + jnp.exp(sink - m)
                ps.append(p * (1.0 / denom))
            pm = jnp.concatenate(ps, axis=0).astype(BF16)
            r = _dot(pm, vw)
            o = jnp.where(head_lane == 0, r[0:CHUNK], 0.0)
            for h in range(1, GQA_GROUP):
                o = o + jnp.where(head_lane == h, r[CHUNK * h:CHUNK * (h + 1)], 0.0)
            ao_g.append(o)
        ao_rows.append(jnp.concatenate(ao_g, axis=1))
        yield
    ao = jnp.concatenate(ao_rows, axis=0)

    an = _rmsnorm(ao, w.nga[...]).astype(BF16)
    cn = _rmsnorm(co, w.ngc[...]).astype(BF16)
    h1 = []
    for r in _row_halves(T):
        mix = _dot(an[r], w.w_out[0:D_ATTN, :]) + _dot(cn[r], w.w_out[D_ATTN:, :])
        yield
        h1.append(_layernorm(ALPHA * x[r] + mix, w.ln1g[...], w.ln1b[...]))
    return jnp.concatenate(h1, axis=0), k, v, vm, k_rep, v_rep


def _ffn(h1, w):
    hb = h1.astype(BF16)
    acts = []
    for lo_c in range(0, D_FF, FF_CHUNK):
        hi_c = min(lo_c + FF_CHUNK, D_FF)
        gt = _dot(hb, w.wgu[:, lo_c:hi_c])
        up = _dot(hb, w.wgu[:, D_FF + lo_c:D_FF + hi_c])
        acts.append((gt * jax.nn.sigmoid(gt) * up).astype(BF16))
        yield
    act = jnp.concatenate(acts, axis=1)
    y = []
    for r in _row_halves(h1.shape[0]):
        f = _dot(act[r], w.wd[...])
        yield
        y.append(_layernorm(ALPHA * h1[r] + f, w.ln2g[...], w.ln2b[...]))
    return jnp.concatenate(y, axis=0)


def _run(*gens):
    results = [None] * len(gens)
    live = list(range(len(gens)))
    while live:
        for i in list(live):
            try:
                next(gens[i])
            except StopIteration as done:
                results[i] = done.value
                live.remove(i)
    return results


_WEIGHT_NAMES = ("w_in", "lnv_g", "lnv_b", "sinks", "wpair", "bias", "nga", "ngc", "w_out", "ln1g", "ln1b",
                 "wgu", "wd", "ln2g", "ln2b")
_Weights = collections.namedtuple("_Weights", _WEIGHT_NAMES)
N_W = len(_WEIGHT_NAMES)


def _prompt_body(T, tiles_per_seq, n_tiles, *refs):
    x_ref, c_ref, sa_ref, sb_ref = refs[:4]
    w = _Weights(*refs[4:4 + N_W])
    y_ref, k_out_ref, v_out_ref, hk_ref, hv_ref, h1_ref = refs[4 + N_W:]
    step = pl.program_id(0)
    j = jnp.minimum(step, n_tiles - 1) % tiles_per_seq
    rd = j % 2
    wr = 1 - rd
    cur = step % 2
    prev = 1 - cur

    @pl.when(step == 0)
    def _():
        h1_ref[1] = jnp.zeros(h1_ref.shape[1:], F32)

    @pl.when(j == 0)
    def _():
        hk_ref[0] = jnp.zeros(hk_ref.shape[1:], BF16)
        hv_ref[0] = jnp.zeros(hv_ref.shape[1:], BF16)

    hist_k = [hk_ref[rd, g] for g in range(N_KV_HEADS)]
    hist_v = [hv_ref[rd, g] for g in range(N_KV_HEADS)]

    def windows(k_rep, v_rep, g, ci):
        k_all = jnp.concatenate([hist_k[g], k_rep[g]], axis=0)
        v_all = jnp.concatenate([hist_v[g], v_rep[g]], axis=0)
        return k_all[CHUNK * ci:CHUNK * ci + WIN_KEYS], v_all[CHUNK * ci:CHUNK * ci + WIN_KEYS]

    def first_valid(ci):
        if ci >= WINDOW // CHUNK:
            return None
        return jnp.where(j == 0, (WINDOW // CHUNK - ci) * CHUNK, 0)

    y_prev, (h1, k, v, _, k_rep, v_rep) = _run(
        _ffn(h1_ref[prev], w),
        _mixer(x_ref[...], c_ref[...], sa_ref[...], sb_ref[...], windows, first_valid, CMLP_BLOCK, w))
    y_ref[...] = y_prev
    h1_ref[cur] = h1
    k_out_ref[0] = k[T - WINDOW:, :]
    v_out_ref[0] = v[T - WINDOW:, :]
    for g in range(N_KV_HEADS):
        hk_ref[wr, g] = k_rep[g][T - WINDOW:]
        hv_ref[wr, g] = v_rep[g][T - WINDOW:]


def _cache_rep4(ref):
    n = ref.shape[0] // N_KV_HEADS
    out = []
    for g in range(N_KV_HEADS):
        a = ref[pl.ds(g, n, stride=N_KV_HEADS), :]
        a2 = jnp.concatenate([a, a], axis=1).astype(BF16)
        out.append(jnp.concatenate([a2, a2], axis=1))
    return out


def _sample_body(*refs):
    x_ref, c_ref, sa_ref, sb_ref, ck_ref, cv_ref = refs[:6]
    w = _Weights(*refs[6:6 + N_W])
    y_ref, k_out_ref, v_out_ref, vm_out_ref, h1_ref = refs[6 + N_W:]
    step = pl.program_id(0)
    cur = step % 2
    prev = 1 - cur

    @pl.when(step == 0)
    def _():
        h1_ref[1] = jnp.zeros(h1_ref.shape[1:], F32)

    ck_rep, cv_rep = _cache_rep4(ck_ref), _cache_rep4(cv_ref)

    def windows(k_rep, v_rep, g, ci):
        cat = lambda cache, new: jnp.concatenate(
            [cache[g][WINDOW * ci:WINDOW * (ci + 1)], new[g][CHUNK * ci:CHUNK * (ci + 1)]], axis=0)
        return cat(ck_rep, k_rep), cat(cv_rep, v_rep)

    y_prev, (h1, k, v, vm, _, _) = _run(
        _ffn(h1_ref[prev], w),
        _mixer(x_ref[...], c_ref[...], sa_ref[...], sb_ref[...], windows, lambda ci: None, CHUNK, w))
    y_ref[...] = y_prev
    h1_ref[cur] = h1
    k_out_ref[...] = k
    v_out_ref[...] = v
    vm_out_ref[...] = vm


def _const_spec(shape):
    nd = len(shape)
    return pl.BlockSpec(shape, lambda *_: (0,) * nd)


def _weight_specs(ws):
    smem = pl.BlockSpec(memory_space=pltpu.SMEM)
    return [smem if name == "sinks" else _const_spec(a.shape) for name, a in zip(_WEIGHT_NAMES, ws)]


def _rope_tables(pos, reps=1):
    half = ROPE_DIM // 2
    inv = np.power(ROPE_THETA, -np.arange(half, dtype=np.float64) * (2.0 / ROPE_DIM))
    ang = np.asarray(pos, np.float64)[:, None] * inv[None, :]
    cos, sin = np.cos(ang), np.sin(ang)
    n = ang.shape[0]
    pad = HEAD_DIM - ROPE_DIM
    c = np.concatenate([cos, cos, np.ones((n, pad))], 1)
    sa = np.concatenate([-sin, np.zeros((n, half + pad))], 1)
    sb = np.concatenate([np.zeros((n, half)), sin, np.zeros((n, pad))], 1)
    return [jnp.asarray(np.tile(t, (reps, LANES // HEAD_DIM)), F32) for t in (c, sa, sb)]


def _gate_params(w_s, b_s, lb):
    i = jnp.arange(lb)
    mask = (i[None, :] // CHUNK) <= (i[:, None] // CHUNK)
    w = jnp.where(mask[None], w_s[:, :lb, :lb], 0.0)
    wpair = jnp.concatenate([w[0::2], w[1::2]], axis=2).astype(BF16)
    bias = jnp.repeat(jnp.transpose(b_s[:, :lb]), CMLP_GROUP_DIM, axis=1)
    return wpair, bias


def kernel(x_prompt, x_sample, cache_win_k, cache_win_v, w_in, ln_v_g, ln_v_b, attn_sinks, w_spatial, b_spatial,
           norm_attn_g, norm_cmlp_g, w_out, ln1_g, ln1_b, w_gate_up, w_down, ln2_g, ln2_b):
    B, S, _ = x_prompt.shape
    Bd, L, _ = x_sample.shape
    assert cache_win_k.shape[2] == WINDOW and L == CHUNK and w_in.shape[0] == 1

    row = lambda a: a[0].reshape(1, -1)

    def weights(lb):
        wpair, bias = _gate_params(w_spatial[0], b_spatial[0], lb)
        return _Weights(w_in=w_in[0].astype(BF16), lnv_g=row(ln_v_g), lnv_b=row(ln_v_b), sinks=attn_sinks[0],
                        wpair=wpair, bias=bias, nga=row(norm_attn_g), ngc=row(norm_cmlp_g),
                        w_out=w_out[0].astype(BF16), ln1g=row(ln1_g), ln1b=row(ln1_b),
                        wgu=w_gate_up[0].astype(BF16), wd=w_down[0].astype(BF16), ln2g=row(ln2_g), ln2b=row(ln2_b))

    T = PROMPT_TILE
    tps = S // T
    n_tiles = B * tps
    wp = weights(CMLP_BLOCK)
    tables = _rope_tables(np.arange(S))
    mix_tile = lambda s: jnp.minimum(s, n_tiles - 1)
    ffn_tile = lambda s: jnp.maximum(s - 1, 0)
    tab = pl.BlockSpec((T, LANES), lambda s: (mix_tile(s) % tps, 0))
    win = pl.BlockSpec((1, WINDOW, D_KV), lambda s: (mix_tile(s) // tps, 0, 0))
    y_p, kwin, vwin = pl.pallas_call(
        functools.partial(_prompt_body, T, tps, n_tiles),
        grid=(n_tiles + 1,),
        in_specs=[pl.BlockSpec((T, D_MODEL), lambda s: (mix_tile(s), 0)), tab, tab, tab] + _weight_specs(wp),
        out_specs=[pl.BlockSpec((T, D_MODEL), lambda s: (ffn_tile(s), 0)), win, win],
        out_shape=[jax.ShapeDtypeStruct((B * S, D_MODEL), F32),
                   jax.ShapeDtypeStruct((B, WINDOW, D_KV), F32),
                   jax.ShapeDtypeStruct((B, WINDOW, D_KV), F32)],
        scratch_shapes=[pltpu.VMEM((2, N_KV_HEADS, WINDOW, 2 * LANES), BF16),
                        pltpu.VMEM((2, N_KV_HEADS, WINDOW, 2 * LANES), BF16),
                        pltpu.VMEM((2, T, D_MODEL), F32)],
        compiler_params=pltpu.CompilerParams(dimension_semantics=("arbitrary",), vmem_limit_bytes=VMEM_LIMIT),
        name="layer_prompt",
    )(x_prompt.reshape(B * S, D_MODEL), *tables, *wp)
    y_p = y_p.reshape(B, S, D_MODEL)

    NS = SAMPLE_SEQS
    Ts = NS * L
    ws = weights(L)
    tables_s = _rope_tables(PAST_LEN + np.arange(L), reps=NS)
    n_st = Bd // NS
    mix_rows = lambda n, width: pl.BlockSpec((n, width), lambda s: (jnp.minimum(s, n_st - 1), 0))
    cache_rows = NS * WINDOW * N_KV_HEADS
    y_s, k_s, v_s, vm_s = pl.pallas_call(
        _sample_body,
        grid=(n_st + 1,),
        in_specs=[mix_rows(Ts, D_MODEL)] + [_const_spec(t.shape) for t in tables_s]
        + [mix_rows(cache_rows, HEAD_DIM), mix_rows(cache_rows, HEAD_DIM)] + _weight_specs(ws),
        out_specs=[pl.BlockSpec((Ts, D_MODEL), lambda s: (jnp.maximum(s - 1, 0), 0)),
                   mix_rows(Ts, D_KV), mix_rows(Ts, D_KV), mix_rows(Ts, D_CMLP)],
        out_shape=[jax.ShapeDtypeStruct((Bd * L, D_MODEL), F32),
                   jax.ShapeDtypeStruct((Bd * L, D_KV), F32),
                   jax.ShapeDtypeStruct((Bd * L, D_KV), F32),
                   jax.ShapeDtypeStruct((Bd * L, D_CMLP), F32)],
        scratch_shapes=[pltpu.VMEM((2, Ts, D_MODEL), F32)],
        compiler_params=pltpu.CompilerParams(dimension_semantics=("arbitrary",), vmem_limit_bytes=VMEM_LIMIT),
        name="layer_sample",
    )(x_sample.reshape(Bd * L, D_MODEL), *tables_s,
      cache_win_k.reshape(Bd * WINDOW * N_KV_HEADS, HEAD_DIM),
      cache_win_v.reshape(Bd * WINDOW * N_KV_HEADS, HEAD_DIM), *ws)

    return (y_p, y_s.reshape(Bd, L, D_MODEL),
            kwin.reshape(1, B, WINDOW, N_KV_HEADS, HEAD_DIM),
            vwin.reshape(1, B, WINDOW, N_KV_HEADS, HEAD_DIM),
            k_s.reshape(1, Bd, L, N_KV_HEADS, HEAD_DIM),
            v_s.reshape(1, Bd, L, N_KV_HEADS, HEAD_DIM),
            vm_s.reshape(1, Bd, L, CMLP_GROUPS, CMLP_GROUP_DIM))
```

```python
import collections
import functools

import jax
import jax.numpy as jnp
import numpy as np
from jax import lax
from jax.experimental import pallas as pl
from jax.experimental.pallas import tpu as pltpu

D_MODEL = 1024
CHUNK = 64
HEAD_DIM = 64
D_ATTN = 512
D_CMLP = 512
N_HEADS = 8
N_KV_HEADS = 2
GQA_GROUP = 4
D_KV = 128
WINDOW = 128
ROPE_DIM = 16
ROPE_THETA = 500000.0
CMLP_BLOCK = 128
CMLP_GROUPS = 8
CMLP_GROUP_DIM = 64
D_IN = 1792
D_FF = 2816
PAST_LEN = 1024
ALPHA = 2.0 ** 0.25
LN_EPS = 1e-5
NEG_INF = -1e30
LANES = 128
WIN_KEYS = WINDOW + CHUNK
FF_CHUNK = 256
VMEM_LIMIT = 56 * 1024 * 1024
PROMPT_TILE = 512
SAMPLE_SEQS = 4

F32 = jnp.float32
BF16 = jnp.bfloat16


def _layernorm(x, g, b):
    mu = jnp.mean(x, -1, keepdims=True)
    d = x - mu
    var = jnp.mean(d * d, -1, keepdims=True)
    return d * lax.rsqrt(var + LN_EPS) * g + b


def _rmsnorm(x, g):
    ms = jnp.mean(x * x, -1, keepdims=True)
    return x * lax.rsqrt(ms + LN_EPS) * g


def _dot(a, b):
    return jnp.dot(a, b, preferred_element_type=F32)


def _rope(x, c, sa, sb):
    out = []
    for i in range(x.shape[1] // LANES):
        s = x[:, LANES * i:LANES * (i + 1)]
        up = pltpu.roll(s, LANES - ROPE_DIM // 2, 1)
        dn = pltpu.roll(s, ROPE_DIM // 2, 1)
        out.append(s * c + up * sa + dn * sb)
    return out[0] if len(out) == 1 else jnp.concatenate(out, axis=1)


def _row_halves(n):
    return (slice(0, n // 2), slice(n // 2, n))


def _rep4(a):
    ar = pltpu.roll(a, HEAD_DIM, 1)
    first = lax.broadcasted_iota(jnp.int32, a.shape, 1) < HEAD_DIM
    g0 = jnp.where(first, a, ar).astype(BF16)
    g1 = jnp.where(first, ar, a).astype(BF16)
    return [jnp.concatenate([g0, g0], axis=1), jnp.concatenate([g1, g1], axis=1)]


def _mixer(x, c, sa, sb, windows, first_valid, LB, w):
    T = x.shape[0]
    n_chunks = T // CHUNK
    nb = T // LB
    xb = x.astype(BF16)
    uv = jax.nn.gelu(_dot(xb, w.w_in[:, D_ATTN + 2 * D_KV:D_IN]))
    yield
    q = _dot(xb, w.w_in[:, 0:D_ATTN])
    kv = _dot(xb, w.w_in[:, D_ATTN:D_ATTN + 2 * D_KV])

    k = _rope(kv[:, :D_KV], c, sa, sb)
    v = kv[:, D_KV:]
    q = _rope(q, c, sa, sb) * (HEAD_DIM ** -0.5)

    yield
    u = uv[:, :D_CMLP]
    vm = _layernorm(uv[:, D_CMLP:], w.lnv_g[...], w.lnv_b[...])

    lo_mask = lax.broadcasted_iota(jnp.int32, (T, LANES), 1) < CMLP_GROUP_DIM
    gate_slabs = []
    for p in range(D_CMLP // LANES):
        slab = vm[:, LANES * p:LANES * (p + 1)]
        lo = jnp.where(lo_mask, slab, 0.0).astype(BF16)
        hi = jnp.where(lo_mask, 0.0, slab).astype(BF16)
        rhs = jnp.concatenate(
            [jnp.concatenate([lo[LB * b:LB * (b + 1)], hi[LB * b:LB * (b + 1)]], axis=0) for b in range(nb)], axis=1)
        o = _dot(w.wpair[p], rhs)
        gate_slabs.append(jnp.concatenate([o[:, LANES * b:LANES * (b + 1)] for b in range(nb)], axis=0))
    s_gate = jnp.concatenate(gate_slabs, axis=1) + jnp.concatenate([w.bias[...]] * nb, axis=0)
    co = u * s_gate
    yield

    k_rep, v_rep = _rep4(k), _rep4(v)
    head_lane = lax.broadcasted_iota(jnp.int32, (CHUNK, GQA_GROUP * HEAD_DIM), 1) // HEAD_DIM
    key_idx = lax.broadcasted_iota(jnp.int32, (GQA_GROUP * CHUNK, WIN_KEYS), 1)
    ao_rows = []
    for ci in range(n_chunks):
        ao_g = []
        for g in range(N_KV_HEADS):
            qc = q[CHUNK * ci:CHUNK * (ci + 1), 256 * g:256 * (g + 1)]
            qm = jnp.concatenate([jnp.where(head_lane == h, qc, 0.0) for h in range(GQA_GROUP)], axis=0).astype(BF16)
            kw, vw = windows(k_rep, v_rep, g, ci)
            s = lax.dot_general(qm, kw, (((1,), (1,)), ((), ())), preferred_element_type=F32)
            fv = first_valid(ci)
            if fv is not None:
                s = jnp.where(key_idx >= fv, s, NEG_INF)
            ps = []
            for h in range(GQA_GROUP):
                sh = s[CHUNK * h:CHUNK * (h + 1)]
                sink = w.sinks[GQA_GROUP * g + h]
                m = jnp.maximum(jnp.max(sh, -1, keepdims=True), sink)
                p = jnp.exp(sh - m)
                denom = jnp.sum(p, -1, keepdims=True) + jnp.exp(sink - m)
                ps.append(p * (1.0 / denom))
            pm = jnp.concatenate(ps, axis=0).astype(BF16)
            r = _dot(pm, vw)
            o = jnp.where(head_lane == 0, r[0:CHUNK], 0.0)
            for h in range(1, GQA_GROUP):
                o = o + jnp.where(head_lane == h, r[CHUNK * h:CHUNK * (h + 1)], 0.0)
            ao_g.append(o)
        ao_rows.append(jnp.concatenate(ao_g, axis=1))
        yield
    ao = jnp.concatenate(ao_rows, axis=0)

    an = _rmsnorm(ao, w.nga[...]).astype(BF16)
    cn = _rmsnorm(co, w.ngc[...]).astype(BF16)
    h1 = []
    for r in _row_halves(T):
        mix = _dot(an[r], w.w_out[0:D_ATTN, :]) + _dot(cn[r], w.w_out[D_ATTN:, :])
        yield
        h1.append(_layernorm(ALPHA * x[r] + mix, w.ln1g[...], w.ln1b[...]))
    return jnp.concatenate(h1, axis=0), k, v, vm, k_rep, v_rep


def _ffn(h1, w):
    hb = h1.astype(BF16)
    acts = []
    for lo_c in range(0, D_FF, FF_CHUNK):
        hi_c = min(lo_c + FF_CHUNK, D_FF)
        gt = _dot(hb, w.wgu[:, lo_c:hi_c])
        up = _dot(hb, w.wgu[:, D_FF + lo_c:D_FF + hi_c])
        acts.append((gt * jax.nn.sigmoid(gt) * up).astype(BF16))
        yield
    act = jnp.concatenate(acts, axis=1)
    y = []
    for r in _row_halves(h1.shape[0]):
        f = _dot(act[r], w.wd[...])
        yield
        y.append(_layernorm(ALPHA * h1[r] + f, w.ln2g[...], w.ln2b[...]))
    return jnp.concatenate(y, axis=0)


def _run(*gens):
    results = [None] * len(gens)
    live = list(range(len(gens)))
    while live:
        for i in list(live):
            try:
                next(gens[i])
            except StopIteration as done:
                results[i] = done.value
                live.remove(i)
    return results


_WEIGHT_NAMES = ("w_in", "lnv_g", "lnv_b", "sinks", "wpair", "bias", "nga", "ngc", "w_out", "ln1g", "ln1b",
                 "wgu", "wd", "ln2g", "ln2b")
_Weights = collections.namedtuple("_Weights", _WEIGHT_NAMES)
N_W = len(_WEIGHT_NAMES)


def _pipelined_step(step, n_tiles, h1_ref, w, make_mixer, store_y, store_mixer):
    cur = step % 2
    prev = 1 - cur

    def phase(do_mixer, do_ffn):
        gens = ([_ffn(h1_ref[prev], w)] if do_ffn else []) + ([make_mixer()] if do_mixer else [])
        res = _run(*gens)
        if do_ffn:
            store_y(res[0])
        if do_mixer:
            h1_ref[cur] = res[-1][0]
            store_mixer(*res[-1][1:])

    pl.when(step == 0)(functools.partial(phase, True, False))
    pl.when(jnp.logical_and(step > 0, step < n_tiles))(functools.partial(phase, True, True))
    pl.when(step == n_tiles)(functools.partial(phase, False, True))


def _prompt_body(T, tiles_per_seq, n_tiles, *refs):
    x_ref, c_ref, sa_ref, sb_ref = refs[:4]
    w = _Weights(*refs[4:4 + N_W])
    y_ref, k_out_ref, v_out_ref, hk_ref, hv_ref, h1_ref = refs[4 + N_W:]
    step = pl.program_id(0)
    j = step % tiles_per_seq
    rd = j % 2
    wr = 1 - rd

    @pl.when(j == 0)
    def _():
        hk_ref[0] = jnp.zeros(hk_ref.shape[1:], BF16)
        hv_ref[0] = jnp.zeros(hv_ref.shape[1:], BF16)

    def first_valid(ci):
        if ci >= WINDOW // CHUNK:
            return None
        return jnp.where(j == 0, (WINDOW // CHUNK - ci) * CHUNK, 0)

    def make_mixer():
        hist_k = [hk_ref[rd, g] for g in range(N_KV_HEADS)]
        hist_v = [hv_ref[rd, g] for g in range(N_KV_HEADS)]

        def windows(k_rep, v_rep, g, ci):
            k_all = jnp.concatenate([hist_k[g], k_rep[g]], axis=0)
            v_all = jnp.concatenate([hist_v[g], v_rep[g]], axis=0)
            return k_all[CHUNK * ci:CHUNK * ci + WIN_KEYS], v_all[CHUNK * ci:CHUNK * ci + WIN_KEYS]

        return _mixer(x_ref[...], c_ref[...], sa_ref[...], sb_ref[...], windows, first_valid, CMLP_BLOCK, w)

    def store_y(y):
        y_ref[...] = y

    def store_mixer(k, v, _, k_rep, v_rep):
        k_out_ref[0] = k[T - WINDOW:, :]
        v_out_ref[0] = v[T - WINDOW:, :]
        for g in range(N_KV_HEADS):
            hk_ref[wr, g] = k_rep[g][T - WINDOW:]
            hv_ref[wr, g] = v_rep[g][T - WINDOW:]

    _pipelined_step(step, n_tiles, h1_ref, w, make_mixer, store_y, store_mixer)


def _cache_rep4(ref):
    n = ref.shape[0] // N_KV_HEADS
    out = []
    for g in range(N_KV_HEADS):
        a = ref[pl.ds(g, n, stride=N_KV_HEADS), :]
        a2 = jnp.concatenate([a, a], axis=1).astype(BF16)
        out.append(jnp.concatenate([a2, a2], axis=1))
    return out


def _sample_body(n_tiles, *refs):
    x_ref, c_ref, sa_ref, sb_ref, ck_ref, cv_ref = refs[:6]
    w = _Weights(*refs[6:6 + N_W])
    y_ref, k_out_ref, v_out_ref, vm_out_ref, h1_ref = refs[6 + N_W:]

    def make_mixer():
        ck_rep, cv_rep = _cache_rep4(ck_ref), _cache_rep4(cv_ref)

        def windows(k_rep, v_rep, g, ci):
            cat = lambda cache, new: jnp.concatenate(
                [cache[g][WINDOW * ci:WINDOW * (ci + 1)], new[g][CHUNK * ci:CHUNK * (ci + 1)]], axis=0)
            return cat(ck_rep, k_rep), cat(cv_rep, v_rep)

        return _mixer(x_ref[...], c_ref[...], sa_ref[...], sb_ref[...], windows, lambda ci: None, CHUNK, w)

    def store_y(y):
        y_ref[...] = y

    def store_mixer(k, v, vm, *_):
        k_out_ref[...] = k
        v_out_ref[...] = v
        vm_out_ref[...] = vm

    _pipelined_step(pl.program_id(0), n_tiles, h1_ref, w, make_mixer, store_y, store_mixer)


def _const_spec(shape):
    nd = len(shape)
    return pl.BlockSpec(shape, lambda *_: (0,) * nd)


def _weight_specs(ws):
    smem = pl.BlockSpec(memory_space=pltpu.SMEM)
    return [smem if name == "sinks" else _const_spec(a.shape) for name, a in zip(_WEIGHT_NAMES, ws)]


def _rope_tables(pos, reps=1):
    half = ROPE_DIM // 2
    inv = np.power(ROPE_THETA, -np.arange(half, dtype=np.float64) * (2.0 / ROPE_DIM))
    ang = np.asarray(pos, np.float64)[:, None] * inv[None, :]
    cos, sin = np.cos(ang), np.sin(ang)
    n = ang.shape[0]
    pad = HEAD_DIM - ROPE_DIM
    c = np.concatenate([cos, cos, np.ones((n, pad))], 1)
    sa = np.concatenate([-sin, np.zeros((n, half + pad))], 1)
    sb = np.concatenate([np.zeros((n, half)), sin, np.zeros((n, pad))], 1)
    return [jnp.asarray(np.tile(t, (reps, LANES // HEAD_DIM)), F32) for t in (c, sa, sb)]


def _gate_params(w_s, b_s, lb):
    i = jnp.arange(lb)
    mask = (i[None, :] // CHUNK) <= (i[:, None] // CHUNK)
    w = jnp.where(mask[None], w_s[:, :lb, :lb], 0.0)
    wpair = jnp.concatenate([w[0::2], w[1::2]], axis=2).astype(BF16)
    bias = jnp.repeat(jnp.transpose(b_s[:, :lb]), CMLP_GROUP_DIM, axis=1)
    return wpair, bias


def kernel(x_prompt, x_sample, cache_win_k, cache_win_v, w_in, ln_v_g, ln_v_b, attn_sinks, w_spatial, b_spatial,
           norm_attn_g, norm_cmlp_g, w_out, ln1_g, ln1_b, w_gate_up, w_down, ln2_g, ln2_b):
    B, S, _ = x_prompt.shape
    Bd, L, _ = x_sample.shape
    assert cache_win_k.shape[2] == WINDOW and L == CHUNK and w_in.shape[0] == 1

    row = lambda a: a[0].reshape(1, -1)

    def weights(lb):
        wpair, bias = _gate_params(w_spatial[0], b_spatial[0], lb)
        return _Weights(w_in=w_in[0].astype(BF16), lnv_g=row(ln_v_g), lnv_b=row(ln_v_b), sinks=attn_sinks[0],
                        wpair=wpair, bias=bias, nga=row(norm_attn_g), ngc=row(norm_cmlp_g),
                        w_out=w_out[0].astype(BF16), ln1g=row(ln1_g), ln1b=row(ln1_b),
                        wgu=w_gate_up[0].astype(BF16), wd=w_down[0].astype(BF16), ln2g=row(ln2_g), ln2b=row(ln2_b))

    T = PROMPT_TILE
    tps = S // T
    n_tiles = B * tps
    wp = weights(CMLP_BLOCK)
    tables = _rope_tables(np.arange(S))
    mix_tile = lambda s: jnp.minimum(s, n_tiles - 1)
    ffn_tile = lambda s: jnp.maximum(s - 1, 0)
    tab = pl.BlockSpec((T, LANES), lambda s: (mix_tile(s) % tps, 0))
    win = pl.BlockSpec((1, WINDOW, D_KV), lambda s: (mix_tile(s) // tps, 0, 0))
    y_p, kwin, vwin = pl.pallas_call(
        functools.partial(_prompt_body, T, tps, n_tiles),
        grid=(n_tiles + 1,),
        in_specs=[pl.BlockSpec((T, D_MODEL), lambda s: (mix_tile(s), 0)), tab, tab, tab] + _weight_specs(wp),
        out_specs=[pl.BlockSpec((T, D_MODEL), lambda s: (ffn_tile(s), 0)), win, win],
        out_shape=[jax.ShapeDtypeStruct((B * S, D_MODEL), F32),
                   jax.ShapeDtypeStruct((B, WINDOW, D_KV), F32),
                   jax.ShapeDtypeStruct((B, WINDOW, D_KV), F32)],
        scratch_shapes=[pltpu.VMEM((2, N_KV_HEADS, WINDOW, 2 * LANES), BF16),
                        pltpu.VMEM((2, N_KV_HEADS, WINDOW, 2 * LANES), BF16),
                        pltpu.VMEM((2, T, D_MODEL), F32)],
        compiler_params=pltpu.CompilerParams(dimension_semantics=("arbitrary",), vmem_limit_bytes=VMEM_LIMIT),
        name="layer_prompt",
    )(x_prompt.reshape(B * S, D_MODEL), *tables, *wp)
    y_p = y_p.reshape(B, S, D_MODEL)

    NS = SAMPLE_SEQS
    Ts = NS * L
    ws = weights(L)
    tables_s = _rope_tables(PAST_LEN + np.arange(L), reps=NS)
    n_st = Bd // NS
    mix_rows = lambda n, width: pl.BlockSpec((n, width), lambda s: (jnp.minimum(s, n_st - 1), 0))
    cache_rows = NS * WINDOW * N_KV_HEADS
    y_s, k_s, v_s, vm_s = pl.pallas_call(
        functools.partial(_sample_body, n_st),
        grid=(n_st + 1,),
        in_specs=[mix_rows(Ts, D_MODEL)] + [_const_spec(t.shape) for t in tables_s]
        + [mix_rows(cache_rows, HEAD_DIM), mix_rows(cache_rows, HEAD_DIM)] + _weight_specs(ws),
        out_specs=[pl.BlockSpec((Ts, D_MODEL), lambda s: (jnp.maximum(s - 1, 0), 0)),
                   mix_rows(Ts, D_KV), mix_rows(Ts, D_KV), mix_rows(Ts, D_CMLP)],
        out_shape=[jax.ShapeDtypeStruct((Bd * L, D_MODEL), F32),
                   jax.ShapeDtypeStruct((Bd * L, D_KV), F32),
                   jax.ShapeDtypeStruct((Bd * L, D_KV), F32),
                   jax.ShapeDtypeStruct((Bd * L, D_CMLP), F32)],
        scratch_shapes=[pltpu.VMEM((2, Ts, D_MODEL), F32)],
        compiler_params=pltpu.CompilerParams(dimension_semantics=("arbitrary",), vmem_limit_bytes=VMEM_LIMIT),
        name="layer_sample",
    )(x_sample.reshape(Bd * L, D_MODEL), *tables_s,
      cache_win_k.reshape(Bd * WINDOW * N_KV_HEADS, HEAD_DIM),
      cache_win_v.reshape(Bd * WINDOW * N_KV_HEADS, HEAD_DIM), *ws)

    return (y_p, y_s.reshape(Bd, L, D_MODEL),
            kwin.reshape(1, B, WINDOW, N_KV_HEADS, HEAD_DIM),
            vwin.reshape(1, B, WINDOW, N_KV_HEADS, HEAD_DIM),
            k_s.reshape(1, Bd, L, N_KV_HEADS, HEAD_DIM),
            v_s.reshape(1, Bd, L, N_KV_HEADS, HEAD_DIM),
            vm_s.reshape(1, Bd, L, CMLP_GROUPS, CMLP_GROUP_DIM))
```

```python
import collections
import functools

import jax
import jax.numpy as jnp
import numpy as np
from jax import lax
from jax.experimental import pallas as pl
from jax.experimental.pallas import tpu as pltpu

D_MODEL = 1024
CHUNK = 64
HEAD_DIM = 64
D_ATTN = 512
D_CMLP = 512
N_HEADS = 8
N_KV_HEADS = 2
GQA_GROUP = 4
D_KV = 128
WINDOW = 128
ROPE_DIM = 16
ROPE_THETA = 500000.0
CMLP_BLOCK = 128
CMLP_GROUPS = 8
CMLP_GROUP_DIM = 64
D_IN = 1792
D_FF = 2816
PAST_LEN = 1024
ALPHA = 2.0 ** 0.25
LN_EPS = 1e-5
NEG_INF = -1e30
LANES = 128
WIN_KEYS = WINDOW + CHUNK
FF_CHUNK = 256
VMEM_LIMIT = 56 * 1024 * 1024
PROMPT_TILE = 512
W_CHUNK = 256
SAMPLE_SEQS = 4

F32 = jnp.float32
BF16 = jnp.bfloat16


def _layernorm(x, g, b):
    mu = jnp.mean(x, -1, keepdims=True)
    d = x - mu
    var = jnp.mean(d * d, -1, keepdims=True)
    return d * lax.rsqrt(var + LN_EPS) * g + b


def _rmsnorm(x, g):
    ms = jnp.mean(x * x, -1, keepdims=True)
    return x * lax.rsqrt(ms + LN_EPS) * g


def _dot(a, b):
    return jnp.dot(a, b, preferred_element_type=F32)


def _rope(x, c, sa, sb):
    out = []
    for i in range(x.shape[1] // LANES):
        s = x[:, LANES * i:LANES * (i + 1)]
        up = pltpu.roll(s, LANES - ROPE_DIM // 2, 1)
        dn = pltpu.roll(s, ROPE_DIM // 2, 1)
        out.append(s * c + up * sa + dn * sb)
    return out[0] if len(out) == 1 else jnp.concatenate(out, axis=1)


def _row_halves(n):
    return (slice(0, n // 2), slice(n // 2, n))


def _rep4(a):
    ar = pltpu.roll(a, HEAD_DIM, 1)
    first = lax.broadcasted_iota(jnp.int32, a.shape, 1) < HEAD_DIM
    g0 = jnp.where(first, a, ar).astype(BF16)
    g1 = jnp.where(first, ar, a).astype(BF16)
    return [jnp.concatenate([g0, g0], axis=1), jnp.concatenate([g1, g1], axis=1)]


def _mixer(x, c, sa, sb, windows, first_valid, LB, w):
    T = x.shape[0]
    n_chunks = T // CHUNK
    nb = T // LB
    xb = x.astype(BF16)
    uv = jax.nn.gelu(_dot(xb, w.w_in[:, D_ATTN + 2 * D_KV:D_IN]))
    yield
    q = _dot(xb, w.w_in[:, 0:D_ATTN])
    kv = _dot(xb, w.w_in[:, D_ATTN:D_ATTN + 2 * D_KV])

    k = _rope(kv[:, :D_KV], c, sa, sb)
    v = kv[:, D_KV:]
    q = _rope(q, c, sa, sb) * (HEAD_DIM ** -0.5)

    yield
    u = uv[:, :D_CMLP]
    vm = _layernorm(uv[:, D_CMLP:], w.lnv_g[...], w.lnv_b[...])

    lo_mask = lax.broadcasted_iota(jnp.int32, (T, LANES), 1) < CMLP_GROUP_DIM
    gate_slabs = []
    for p in range(D_CMLP // LANES):
        slab = vm[:, LANES * p:LANES * (p + 1)]
        lo = jnp.where(lo_mask, slab, 0.0).astype(BF16)
        hi = jnp.where(lo_mask, 0.0, slab).astype(BF16)
        rhs = jnp.concatenate(
            [jnp.concatenate([lo[LB * b:LB * (b + 1)], hi[LB * b:LB * (b + 1)]], axis=0) for b in range(nb)], axis=1)
        o = _dot(w.wpair[p], rhs)
        gate_slabs.append(jnp.concatenate([o[:, LANES * b:LANES * (b + 1)] for b in range(nb)], axis=0))
    s_gate = jnp.concatenate(gate_slabs, axis=1) + jnp.concatenate([w.bias[...]] * nb, axis=0)
    co = u * s_gate
    yield

    k_rep, v_rep = _rep4(k), _rep4(v)
    head_lane = lax.broadcasted_iota(jnp.int32, (CHUNK, GQA_GROUP * HEAD_DIM), 1) // HEAD_DIM
    key_idx = lax.broadcasted_iota(jnp.int32, (GQA_GROUP * CHUNK, WIN_KEYS), 1)
    ao_rows = []
    for ci in range(n_chunks):
        ao_g = []
        for g in range(N_KV_HEADS):
            qc = q[CHUNK * ci:CHUNK * (ci + 1), 256 * g:256 * (g + 1)]
            qm = jnp.concatenate([jnp.where(head_lane == h, qc, 0.0) for h in range(GQA_GROUP)], axis=0).astype(BF16)
            kw, vw = windows(k_rep, v_rep, g, ci)
            s = lax.dot_general(qm, kw, (((1,), (1,)), ((), ())), preferred_element_type=F32)
            fv = first_valid(ci)
            if fv is not None:
                s = jnp.where(key_idx >= fv, s, NEG_INF)
            ps = []
            for h in range(GQA_GROUP):
                sh = s[CHUNK * h:CHUNK * (h + 1)]
                sink = w.sinks[GQA_GROUP * g + h]
                m = jnp.maximum(jnp.max(sh, -1, keepdims=True), sink)
                p = jnp.exp(sh - m)
                denom = jnp.sum(p, -1, keepdims=True) + jnp.exp(sink - m)
                ps.append(p * (1.0 / denom))
            pm = jnp.concatenate(ps, axis=0).astype(BF16)
            r = _dot(pm, vw)
            o = jnp.where(head_lane == 0, r[0:CHUNK], 0.0)
            for h in range(1, GQA_GROUP):
                o = o + jnp.where(head_lane == h, r[CHUNK * h:CHUNK * (h + 1)], 0.0)
            ao_g.append(o)
        ao_rows.append(jnp.concatenate(ao_g, axis=1))
        yield
    ao = jnp.concatenate(ao_rows, axis=0)

    an = _rmsnorm(ao, w.nga[...]).astype(BF16)
    cn = _rmsnorm(co, w.ngc[...]).astype(BF16)
    h1 = []
    for r in _row_halves(T):
        mix = _dot(an[r], w.w_out[0:D_ATTN, :]) + _dot(cn[r], w.w_out[D_ATTN:, :])
        yield
        h1.append(_layernorm(ALPHA * x[r] + mix, w.ln1g[...], w.ln1b[...]))
    return jnp.concatenate(h1, axis=0), k, v, vm, k_rep, v_rep


def _ffn(h1, w):
    hb = h1.astype(BF16)
    acts = []
    for lo_c in range(0, D_FF, FF_CHUNK):
        hi_c = min(lo_c + FF_CHUNK, D_FF)
        gt = _dot(hb, w.wgu[:, lo_c:hi_c])
        up = _dot(hb, w.wgu[:, D_FF + lo_c:D_FF + hi_c])
        acts.append((gt * jax.nn.sigmoid(gt) * up).astype(BF16))
        yield
    act = jnp.concatenate(acts, axis=1)
    y = []
    for r in _row_halves(h1.shape[0]):
        f = _dot(act[r], w.wd[...])
        yield
        y.append(_layernorm(ALPHA * h1[r] + f, w.ln2g[...], w.ln2b[...]))
    return jnp.concatenate(y, axis=0)


def _run(*gens):
    results = [None] * len(gens)
    live = list(range(len(gens)))
    while live:
        for i in list(live):
            try:
                next(gens[i])
            except StopIteration as done:
                results[i] = done.value
                live.remove(i)
    return results


_WEIGHT_NAMES = ("w_in", "lnv_g", "lnv_b", "sinks", "wpair", "bias", "nga", "ngc", "w_out", "ln1g", "ln1b",
                 "wgu", "wd", "ln2g", "ln2b")
_Weights = collections.namedtuple("_Weights", _WEIGHT_NAMES)
N_W = len(_WEIGHT_NAMES)


def _pipelined_step(step, n_tiles, h1_ref, w, make_mixer, store_y, store_mixer):
    cur = step % 2
    prev = 1 - cur

    def phase(do_mixer, do_ffn):
        gens = ([_ffn(h1_ref[prev], w)] if do_ffn else []) + ([make_mixer()] if do_mixer else [])
        res = _run(*gens)
        if do_ffn:
            store_y(res[0])
        if do_mixer:
            h1_ref[cur] = res[-1][0]
            store_mixer(*res[-1][1:])

    pl.when(step == 0)(functools.partial(phase, True, False))
    pl.when(jnp.logical_and(step > 0, step < n_tiles))(functools.partial(phase, True, True))
    pl.when(step == n_tiles)(functools.partial(phase, False, True))


def _load_convert(src, dst, stage, sem, axis):
    n = src.shape[axis] // W_CHUNK

    def copy(c):
        sl = pl.ds(c * W_CHUNK, W_CHUNK)
        window = src.at[sl, :] if axis == 0 else src.at[:, sl]
        return pltpu.make_async_copy(window, stage.at[c % 2], sem.at[c % 2])

    copy(0).start()
    for c in range(n):
        if c + 1 < n:
            copy(c + 1).start()
        copy(c).wait()
        lo = c * W_CHUNK
        if axis == 0:
            dst[lo:lo + W_CHUNK, :] = stage[c % 2].astype(BF16)
        else:
            dst[:, lo:lo + W_CHUNK] = stage[c % 2].astype(BF16)


def _prompt_body(T, tiles_per_seq, n_tiles, *refs):
    x_ref, c_ref, sa_ref, sb_ref = refs[:4]
    w = _Weights(*refs[4:4 + N_W])
    (y_ref, k_out_ref, v_out_ref, wgu_out, wd_out, hk_ref, hv_ref, h1_ref,
     wgu_v, wd_v, stage_gu, stage_d, sem_gu, sem_d, sem_out) = refs[4 + N_W:]
    step = pl.program_id(0)

    out_copies = (pltpu.make_async_copy(wgu_v, wgu_out, sem_out.at[0]),
                  pltpu.make_async_copy(wd_v, wd_out, sem_out.at[1]))

    @pl.when(step == 0)
    def _():
        _load_convert(w.wgu, wgu_v, stage_gu, sem_gu, axis=1)
        _load_convert(w.wd, wd_v, stage_d, sem_d, axis=0)
        for cp in out_copies:
            cp.start()

    @pl.when(step == n_tiles)
    def _():
        for cp in out_copies:
            cp.wait()

    w = w._replace(wgu=wgu_v, wd=wd_v)
    j = step % tiles_per_seq
    rd = j % 2
    wr = 1 - rd

    @pl.when(j == 0)
    def _():
        hk_ref[0] = jnp.zeros(hk_ref.shape[1:], BF16)
        hv_ref[0] = jnp.zeros(hv_ref.shape[1:], BF16)

    def first_valid(ci):
        if ci >= WINDOW // CHUNK:
            return None
        return jnp.where(j == 0, (WINDOW // CHUNK - ci) * CHUNK, 0)

    def make_mixer():
        hist_k = [hk_ref[rd, g] for g in range(N_KV_HEADS)]
        hist_v = [hv_ref[rd, g] for g in range(N_KV_HEADS)]

        def windows(k_rep, v_rep, g, ci):
            k_all = jnp.concatenate([hist_k[g], k_rep[g]], axis=0)
            v_all = jnp.concatenate([hist_v[g], v_rep[g]], axis=0)
            return k_all[CHUNK * ci:CHUNK * ci + WIN_KEYS], v_all[CHUNK * ci:CHUNK * ci + WIN_KEYS]

        return _mixer(x_ref[...], c_ref[...], sa_ref[...], sb_ref[...], windows, first_valid, CMLP_BLOCK, w)

    def store_y(y):
        y_ref[...] = y

    def store_mixer(k, v, _, k_rep, v_rep):
        k_out_ref[0] = k[T - WINDOW:, :]
        v_out_ref[0] = v[T - WINDOW:, :]
        for g in range(N_KV_HEADS):
            hk_ref[wr, g] = k_rep[g][T - WINDOW:]
            hv_ref[wr, g] = v_rep[g][T - WINDOW:]

    _pipelined_step(step, n_tiles, h1_ref, w, make_mixer, store_y, store_mixer)


def _cache_rep4(ref):
    n = ref.shape[0] // N_KV_HEADS
    out = []
    for g in range(N_KV_HEADS):
        a = ref[pl.ds(g, n, stride=N_KV_HEADS), :]
        a2 = jnp.concatenate([a, a], axis=1).astype(BF16)
        out.append(jnp.concatenate([a2, a2], axis=1))
    return out


def _sample_body(n_tiles, *refs):
    x_ref, c_ref, sa_ref, sb_ref, ck_ref, cv_ref = refs[:6]
    w = _Weights(*refs[6:6 + N_W])
    y_ref, k_out_ref, v_out_ref, vm_out_ref, h1_ref = refs[6 + N_W:]

    def make_mixer():
        ck_rep, cv_rep = _cache_rep4(ck_ref), _cache_rep4(cv_ref)

        def windows(k_rep, v_rep, g, ci):
            cat = lambda cache, new: jnp.concatenate(
                [cache[g][WINDOW * ci:WINDOW * (ci + 1)], new[g][CHUNK * ci:CHUNK * (ci + 1)]], axis=0)
            return cat(ck_rep, k_rep), cat(cv_rep, v_rep)

        return _mixer(x_ref[...], c_ref[...], sa_ref[...], sb_ref[...], windows, lambda ci: None, CHUNK, w)

    def store_y(y):
        y_ref[...] = y

    def store_mixer(k, v, vm, *_):
        k_out_ref[...] = k
        v_out_ref[...] = v
        vm_out_ref[...] = vm

    _pipelined_step(pl.program_id(0), n_tiles, h1_ref, w, make_mixer, store_y, store_mixer)


def _const_spec(shape):
    nd = len(shape)
    return pl.BlockSpec(shape, lambda *_: (0,) * nd)


def _weight_specs(ws, in_hbm=()):
    spec = lambda name, a: (pl.BlockSpec(memory_space=pltpu.SMEM) if name == "sinks" else
                            pl.BlockSpec(memory_space=pl.ANY) if name in in_hbm else _const_spec(a.shape))
    return [spec(name, a) for name, a in zip(_WEIGHT_NAMES, ws)]


def _rope_tables(pos, reps=1):
    half = ROPE_DIM // 2
    inv = np.power(ROPE_THETA, -np.arange(half, dtype=np.float64) * (2.0 / ROPE_DIM))
    ang = np.asarray(pos, np.float64)[:, None] * inv[None, :]
    cos, sin = np.cos(ang), np.sin(ang)
    n = ang.shape[0]
    pad = HEAD_DIM - ROPE_DIM
    c = np.concatenate([cos, cos, np.ones((n, pad))], 1)
    sa = np.concatenate([-sin, np.zeros((n, half + pad))], 1)
    sb = np.concatenate([np.zeros((n, half)), sin, np.zeros((n, pad))], 1)
    return [jnp.asarray(np.tile(t, (reps, LANES // HEAD_DIM)), F32) for t in (c, sa, sb)]


def _gate_params(w_s, b_s, lb):
    i = jnp.arange(lb)
    mask = (i[None, :] // CHUNK) <= (i[:, None] // CHUNK)
    w = jnp.where(mask[None], w_s[:, :lb, :lb], 0.0)
    wpair = jnp.concatenate([w[0::2], w[1::2]], axis=2).astype(BF16)
    bias = jnp.repeat(jnp.transpose(b_s[:, :lb]), CMLP_GROUP_DIM, axis=1)
    return wpair, bias


def kernel(x_prompt, x_sample, cache_win_k, cache_win_v, w_in, ln_v_g, ln_v_b, attn_sinks, w_spatial, b_spatial,
           norm_attn_g, norm_cmlp_g, w_out, ln1_g, ln1_b, w_gate_up, w_down, ln2_g, ln2_b):
    B, S, _ = x_prompt.shape
    Bd, L, _ = x_sample.shape
    assert cache_win_k.shape[2] == WINDOW and L == CHUNK and w_in.shape[0] == 1

    row = lambda a: a[0].reshape(1, -1)

    w_in_b, w_out_b = w_in[0].astype(BF16), w_out[0].astype(BF16)

    def weights(lb, wgu, wd):
        wpair, bias = _gate_params(w_spatial[0], b_spatial[0], lb)
        return _Weights(w_in=w_in_b, lnv_g=row(ln_v_g), lnv_b=row(ln_v_b), sinks=attn_sinks[0],
                        wpair=wpair, bias=bias, nga=row(norm_attn_g), ngc=row(norm_cmlp_g),
                        w_out=w_out_b, ln1g=row(ln1_g), ln1b=row(ln1_b),
                        wgu=wgu, wd=wd, ln2g=row(ln2_g), ln2b=row(ln2_b))

    T = PROMPT_TILE
    tps = S // T
    n_tiles = B * tps
    wp = weights(CMLP_BLOCK, w_gate_up[0], w_down[0])
    tables = _rope_tables(np.arange(S))
    mix_tile = lambda s: jnp.minimum(s, n_tiles - 1)
    ffn_tile = lambda s: jnp.maximum(s - 1, 0)
    tab = pl.BlockSpec((T, LANES), lambda s: (mix_tile(s) % tps, 0))
    win = pl.BlockSpec((1, WINDOW, D_KV), lambda s: (mix_tile(s) // tps, 0, 0))
    hbm = pl.BlockSpec(memory_space=pl.ANY)
    y_p, kwin, vwin, wgu_b, wd_b = pl.pallas_call(
        functools.partial(_prompt_body, T, tps, n_tiles),
        grid=(n_tiles + 1,),
        in_specs=[pl.BlockSpec((T, D_MODEL), lambda s: (mix_tile(s), 0)), tab, tab, tab]
        + _weight_specs(wp, in_hbm=("wgu", "wd")),
        out_specs=[pl.BlockSpec((T, D_MODEL), lambda s: (ffn_tile(s), 0)), win, win, hbm, hbm],
        out_shape=[jax.ShapeDtypeStruct((B * S, D_MODEL), F32),
                   jax.ShapeDtypeStruct((B, WINDOW, D_KV), F32),
                   jax.ShapeDtypeStruct((B, WINDOW, D_KV), F32),
                   jax.ShapeDtypeStruct((D_MODEL, 2 * D_FF), BF16),
                   jax.ShapeDtypeStruct((D_FF, D_MODEL), BF16)],
        scratch_shapes=[pltpu.VMEM((2, N_KV_HEADS, WINDOW, 2 * LANES), BF16),
                        pltpu.VMEM((2, N_KV_HEADS, WINDOW, 2 * LANES), BF16),
                        pltpu.VMEM((2, T, D_MODEL), F32),
                        pltpu.VMEM((D_MODEL, 2 * D_FF), BF16),
                        pltpu.VMEM((D_FF, D_MODEL), BF16),
                        pltpu.VMEM((2, D_MODEL, W_CHUNK), F32),
                        pltpu.VMEM((2, W_CHUNK, D_MODEL), F32),
                        pltpu.SemaphoreType.DMA((2,)),
                        pltpu.SemaphoreType.DMA((2,)),
                        pltpu.SemaphoreType.DMA((2,))],
        compiler_params=pltpu.CompilerParams(dimension_semantics=("arbitrary",), vmem_limit_bytes=VMEM_LIMIT),
        name="layer_prompt",
    )(x_prompt.reshape(B * S, D_MODEL), *tables, *wp)
    y_p = y_p.reshape(B, S, D_MODEL)

    NS = SAMPLE_SEQS
    Ts = NS * L
    ws = weights(L, wgu_b, wd_b)
    tables_s = _rope_tables(PAST_LEN + np.arange(L), reps=NS)
    n_st = Bd // NS
    mix_rows = lambda n, width: pl.BlockSpec((n, width), lambda s: (jnp.minimum(s, n_st - 1), 0))
    cache_rows = NS * WINDOW * N_KV_HEADS
    y_s, k_s, v_s, vm_s = pl.pallas_call(
        functools.partial(_sample_body, n_st),
        grid=(n_st + 1,),
        in_specs=[mix_rows(Ts, D_MODEL)] + [_const_spec(t.shape) for t in tables_s]
        + [mix_rows(cache_rows, HEAD_DIM), mix_rows(cache_rows, HEAD_DIM)] + _weight_specs(ws),
        out_specs=[pl.BlockSpec((Ts, D_MODEL), lambda s: (jnp.maximum(s - 1, 0), 0)),
                   mix_rows(Ts, D_KV), mix_rows(Ts, D_KV), mix_rows(Ts, D_CMLP)],
        out_shape=[jax.ShapeDtypeStruct((Bd * L, D_MODEL), F32),
                   jax.ShapeDtypeStruct((Bd * L, D_KV), F32),
                   jax.ShapeDtypeStruct((Bd * L, D_KV), F32),
                   jax.ShapeDtypeStruct((Bd * L, D_CMLP), F32)],
        scratch_shapes=[pltpu.VMEM((2, Ts, D_MODEL), F32)],
        compiler_params=pltpu.CompilerParams(dimension_semantics=("arbitrary",), vmem_limit_bytes=VMEM_LIMIT),
        name="layer_sample",
    )(x_sample.reshape(Bd * L, D_MODEL), *tables_s,
      cache_win_k.reshape(Bd * WINDOW * N_KV_HEADS, HEAD_DIM),
      cache_win_v.reshape(Bd * WINDOW * N_KV_HEADS, HEAD_DIM), *ws)

    return (y_p, y_s.reshape(Bd, L, D_MODEL),
            kwin.reshape(1, B, WINDOW, N_KV_HEADS, HEAD_DIM),
            vwin.reshape(1, B, WINDOW, N_KV_HEADS, HEAD_DIM),
            k_s.reshape(1, Bd, L, N_KV_HEADS, HEAD_DIM),
            v_s.reshape(1, Bd, L, N_KV_HEADS, HEAD_DIM),
            vm_s.reshape(1, Bd, L, CMLP_GROUPS, CMLP_GROUP_DIM))
```

```python
import collections
import functools

import jax
import jax.numpy as jnp
import numpy as np
from jax import lax
from jax.experimental import pallas as pl
from jax.experimental.pallas import tpu as pltpu

D_MODEL = 1024
CHUNK = 64
HEAD_DIM = 64
D_ATTN = 512
D_CMLP = 512
N_HEADS = 8
N_KV_HEADS = 2
GQA_GROUP = 4
D_KV = 128
WINDOW = 128
ROPE_DIM = 16
ROPE_THETA = 500000.0
CMLP_BLOCK = 128
CMLP_GROUPS = 8
CMLP_GROUP_DIM = 64
D_IN = 1792
D_FF = 2816
PAST_LEN = 1024
ALPHA = 2.0 ** 0.25
LN_EPS = 1e-5
NEG_INF = -1e30
LANES = 128
WIN_KEYS = WINDOW + CHUNK
FF_CHUNK = 256
VMEM_LIMIT = 56 * 1024 * 1024
PROMPT_TILE = 512
WGU_ROWS, WD_ROWS = 32, 128
W_SLOTS = 4
W_CHUNKS_PER_PHASE = 4
SAMPLE_SEQS = 4

F32 = jnp.float32
BF16 = jnp.bfloat16


def _layernorm(x, g, b):
    mu = jnp.mean(x, -1, keepdims=True)
    d = x - mu
    var = jnp.mean(d * d, -1, keepdims=True)
    return d * lax.rsqrt(var + LN_EPS) * g + b


def _rmsnorm(x, g):
    ms = jnp.mean(x * x, -1, keepdims=True)
    return x * lax.rsqrt(ms + LN_EPS) * g


def _dot(a, b):
    return jnp.dot(a, b, preferred_element_type=F32)


def _rope(x, c, sa, sb):
    out = []
    for i in range(x.shape[1] // LANES):
        s = x[:, LANES * i:LANES * (i + 1)]
        up = pltpu.roll(s, LANES - ROPE_DIM // 2, 1)
        dn = pltpu.roll(s, ROPE_DIM // 2, 1)
        out.append(s * c + up * sa + dn * sb)
    return out[0] if len(out) == 1 else jnp.concatenate(out, axis=1)


def _row_halves(n):
    return (slice(0, n // 2), slice(n // 2, n))


def _rep4(a):
    ar = pltpu.roll(a, HEAD_DIM, 1)
    first = lax.broadcasted_iota(jnp.int32, a.shape, 1) < HEAD_DIM
    g0 = jnp.where(first, a, ar).astype(BF16)
    g1 = jnp.where(first, ar, a).astype(BF16)
    return [jnp.concatenate([g0, g0], axis=1), jnp.concatenate([g1, g1], axis=1)]


def _mixer(x, c, sa, sb, windows, first_valid, LB, w):
    T = x.shape[0]
    n_chunks = T // CHUNK
    nb = T // LB
    xb = x.astype(BF16)
    uv = jax.nn.gelu(_dot(xb, w.w_in[:, D_ATTN + 2 * D_KV:D_IN]))
    yield
    q = _dot(xb, w.w_in[:, 0:D_ATTN])
    kv = _dot(xb, w.w_in[:, D_ATTN:D_ATTN + 2 * D_KV])

    k = _rope(kv[:, :D_KV], c, sa, sb)
    v = kv[:, D_KV:]
    q = _rope(q, c, sa, sb) * (HEAD_DIM ** -0.5)

    yield
    u = uv[:, :D_CMLP]
    vm = _layernorm(uv[:, D_CMLP:], w.lnv_g[...], w.lnv_b[...])

    lo_mask = lax.broadcasted_iota(jnp.int32, (T, LANES), 1) < CMLP_GROUP_DIM
    gate_slabs = []
    for p in range(D_CMLP // LANES):
        slab = vm[:, LANES * p:LANES * (p + 1)]
        lo = jnp.where(lo_mask, slab, 0.0).astype(BF16)
        hi = jnp.where(lo_mask, 0.0, slab).astype(BF16)
        rhs = jnp.concatenate(
            [jnp.concatenate([lo[LB * b:LB * (b + 1)], hi[LB * b:LB * (b + 1)]], axis=0) for b in range(nb)], axis=1)
        o = _dot(w.wpair[p], rhs)
        gate_slabs.append(jnp.concatenate([o[:, LANES * b:LANES * (b + 1)] for b in range(nb)], axis=0))
    s_gate = jnp.concatenate(gate_slabs, axis=1) + jnp.concatenate([w.bias[...]] * nb, axis=0)
    co = u * s_gate
    yield

    k_rep, v_rep = _rep4(k), _rep4(v)
    head_lane = lax.broadcasted_iota(jnp.int32, (CHUNK, GQA_GROUP * HEAD_DIM), 1) // HEAD_DIM
    key_idx = lax.broadcasted_iota(jnp.int32, (GQA_GROUP * CHUNK, WIN_KEYS), 1)
    ao_rows = []
    for ci in range(n_chunks):
        ao_g = []
        for g in range(N_KV_HEADS):
            qc = q[CHUNK * ci:CHUNK * (ci + 1), 256 * g:256 * (g + 1)]
            qm = jnp.concatenate([jnp.where(head_lane == h, qc, 0.0) for h in range(GQA_GROUP)], axis=0).astype(BF16)
            kw, vw = windows(k_rep, v_rep, g, ci)
            s = lax.dot_general(qm, kw, (((1,), (1,)), ((), ())), preferred_element_type=F32)
            fv = first_valid(ci)
            if fv is not None:
                s = jnp.where(key_idx >= fv, s, NEG_INF)
            ps = []
            for h in range(GQA_GROUP):
                sh = s[CHUNK * h:CHUNK * (h + 1)]
                sink = w.sinks[GQA_GROUP * g + h]
                m = jnp.maximum(jnp.max(sh, -1, keepdims=True), sink)
                p = jnp.exp(sh - m)
                denom = jnp.sum(p, -1, keepdims=True) + jnp.exp(sink - m)
                ps.append(p * (1.0 / denom))
            pm = jnp.concatenate(ps, axis=0).astype(BF16)
            r = _dot(pm, vw)
            o = jnp.where(head_lane == 0, r[0:CHUNK], 0.0)
            for h in range(1, GQA_GROUP):
                o = o + jnp.where(head_lane == h, r[CHUNK * h:CHUNK * (h + 1)], 0.0)
            ao_g.append(o)
        ao_rows.append(jnp.concatenate(ao_g, axis=1))
        yield
    ao = jnp.concatenate(ao_rows, axis=0)

    an = _rmsnorm(ao, w.nga[...]).astype(BF16)
    cn = _rmsnorm(co, w.ngc[...]).astype(BF16)
    h1 = []
    for r in _row_halves(T):
        mix = _dot(an[r], w.w_out[0:D_ATTN, :]) + _dot(cn[r], w.w_out[D_ATTN:, :])
        yield
        h1.append(_layernorm(ALPHA * x[r] + mix, w.ln1g[...], w.ln1b[...]))
    return jnp.concatenate(h1, axis=0), k, v, vm, k_rep, v_rep


def _ffn(h1, w):
    hb = h1.astype(BF16)
    acts = []
    for lo_c in range(0, D_FF, FF_CHUNK):
        hi_c = min(lo_c + FF_CHUNK, D_FF)
        gt = _dot(hb, w.wgu[:, lo_c:hi_c])
        up = _dot(hb, w.wgu[:, D_FF + lo_c:D_FF + hi_c])
        acts.append((gt * jax.nn.sigmoid(gt) * up).astype(BF16))
        yield
    act = jnp.concatenate(acts, axis=1)
    y = []
    for r in _row_halves(h1.shape[0]):
        f = _dot(act[r], w.wd[...])
        yield
        y.append(_layernorm(ALPHA * h1[r] + f, w.ln2g[...], w.ln2b[...]))
    return jnp.concatenate(y, axis=0)


def _run(*gens):
    results = [None] * len(gens)
    live = list(range(len(gens)))
    while live:
        for i in list(live):
            try:
                next(gens[i])
            except StopIteration as done:
                results[i] = done.value
                live.remove(i)
    return results


_WEIGHT_NAMES = ("w_in", "lnv_g", "lnv_b", "sinks", "wpair", "bias", "nga", "ngc", "w_out", "ln1g", "ln1b",
                 "wgu", "wd", "ln2g", "ln2b")
_Weights = collections.namedtuple("_Weights", _WEIGHT_NAMES)
N_W = len(_WEIGHT_NAMES)


def _pipelined_step(step, n_tiles, h1_ref, w, make_mixer, store_y, store_mixer, fill_extra=None):
    cur = step % 2
    prev = 1 - cur

    def phase(do_mixer, do_ffn, extra=None):
        gens = (([_ffn(h1_ref[prev], w)] if do_ffn else []) + ([extra()] if extra else [])
                + ([make_mixer()] if do_mixer else []))
        res = _run(*gens)
        if do_ffn:
            store_y(res[0])
        if do_mixer:
            h1_ref[cur] = res[-1][0]
            store_mixer(*res[-1][1:])

    pl.when(step == 0)(functools.partial(phase, True, False, fill_extra))
    pl.when(jnp.logical_and(step > 0, step < n_tiles))(functools.partial(phase, True, True))
    pl.when(step == n_tiles)(functools.partial(phase, False, True))


def _load_convert(jobs, then):
    plan = [(job, c) for job in jobs for c in range(job[0].shape[0] // job[4])]
    depth = W_SLOTS - 1

    def copy(i):
        (src, _, stage, sem, rows), c = plan[i]
        return pltpu.make_async_copy(src.at[pl.ds(c * rows, rows), :], stage.at[c % W_SLOTS], sem.at[c % W_SLOTS])

    for i in range(min(depth, len(plan))):
        copy(i).start()
    for i, ((_, dst, stage, _, rows), c) in enumerate(plan):
        copy(i).wait()
        dst[c * rows:(c + 1) * rows, :] = stage[c % W_SLOTS].astype(BF16)
        if i + depth < len(plan):
            copy(i + depth).start()
        if (i + 1) % W_CHUNKS_PER_PHASE == 0:
            yield
    then()


def _prompt_body(T, tiles_per_seq, n_tiles, *refs):
    x_ref, c_ref, sa_ref, sb_ref = refs[:4]
    w = _Weights(*refs[4:4 + N_W])
    (y_ref, k_out_ref, v_out_ref, wgu_out, wd_out, hk_ref, hv_ref, h1_ref,
     wgu_v, wd_v, stage_gu, stage_d, sem_gu, sem_d, sem_out) = refs[4 + N_W:]
    step = pl.program_id(0)

    out_copies = (pltpu.make_async_copy(wgu_v, wgu_out, sem_out.at[0]),
                  pltpu.make_async_copy(wd_v, wd_out, sem_out.at[1]))

    def start_out_copies():
        for cp in out_copies:
            cp.start()

    def convert_weights(w_hbm=w):
        jobs = [(w_hbm.wgu, wgu_v, stage_gu, sem_gu, WGU_ROWS), (w_hbm.wd, wd_v, stage_d, sem_d, WD_ROWS)]
        return _load_convert(jobs, start_out_copies)

    @pl.when(step == n_tiles)
    def _():
        for cp in out_copies:
            cp.wait()

    w = w._replace(wgu=wgu_v, wd=wd_v)
    j = step % tiles_per_seq
    rd = j % 2
    wr = 1 - rd

    @pl.when(j == 0)
    def _():
        hk_ref[0] = jnp.zeros(hk_ref.shape[1:], BF16)
        hv_ref[0] = jnp.zeros(hv_ref.shape[1:], BF16)

    def first_valid(ci):
        if ci >= WINDOW // CHUNK:
            return None
        return jnp.where(j == 0, (WINDOW // CHUNK - ci) * CHUNK, 0)

    def make_mixer():
        hist_k = [hk_ref[rd, g] for g in range(N_KV_HEADS)]
        hist_v = [hv_ref[rd, g] for g in range(N_KV_HEADS)]

        def windows(k_rep, v_rep, g, ci):
            k_all = jnp.concatenate([hist_k[g], k_rep[g]], axis=0)
            v_all = jnp.concatenate([hist_v[g], v_rep[g]], axis=0)
            return k_all[CHUNK * ci:CHUNK * ci + WIN_KEYS], v_all[CHUNK * ci:CHUNK * ci + WIN_KEYS]

        return _mixer(x_ref[...], c_ref[...], sa_ref[...], sb_ref[...], windows, first_valid, CMLP_BLOCK, w)

    def store_y(y):
        y_ref[...] = y

    def store_mixer(k, v, _, k_rep, v_rep):
        k_out_ref[0] = k[T - WINDOW:, :]
        v_out_ref[0] = v[T - WINDOW:, :]
        for g in range(N_KV_HEADS):
            hk_ref[wr, g] = k_rep[g][T - WINDOW:]
            hv_ref[wr, g] = v_rep[g][T - WINDOW:]

    _pipelined_step(step, n_tiles, h1_ref, w, make_mixer, store_y, store_mixer, fill_extra=convert_weights)


def _cache_rep4(ref):
    n = ref.shape[0] // N_KV_HEADS
    out = []
    for g in range(N_KV_HEADS):
        a = ref[pl.ds(g, n, stride=N_KV_HEADS), :]
        a2 = jnp.concatenate([a, a], axis=1).astype(BF16)
        out.append(jnp.concatenate([a2, a2], axis=1))
    return out


def _sample_body(n_tiles, *refs):
    x_ref, c_ref, sa_ref, sb_ref, ck_ref, cv_ref = refs[:6]
    w = _Weights(*refs[6:6 + N_W])
    y_ref, k_out_ref, v_out_ref, vm_out_ref, h1_ref = refs[6 + N_W:]

    def make_mixer():
        ck_rep, cv_rep = _cache_rep4(ck_ref), _cache_rep4(cv_ref)

        def windows(k_rep, v_rep, g, ci):
            cat = lambda cache, new: jnp.concatenate(
                [cache[g][WINDOW * ci:WINDOW * (ci + 1)], new[g][CHUNK * ci:CHUNK * (ci + 1)]], axis=0)
            return cat(ck_rep, k_rep), cat(cv_rep, v_rep)

        return _mixer(x_ref[...], c_ref[...], sa_ref[...], sb_ref[...], windows, lambda ci: None, CHUNK, w)

    def store_y(y):
        y_ref[...] = y

    def store_mixer(k, v, vm, *_):
        k_out_ref[...] = k
        v_out_ref[...] = v
        vm_out_ref[...] = vm

    _pipelined_step(pl.program_id(0), n_tiles, h1_ref, w, make_mixer, store_y, store_mixer)


def _const_spec(shape):
    nd = len(shape)
    return pl.BlockSpec(shape, lambda *_: (0,) * nd)


def _weight_specs(ws, in_hbm=()):
    spec = lambda name, a: (pl.BlockSpec(memory_space=pltpu.SMEM) if name == "sinks" else
                            pl.BlockSpec(memory_space=pl.ANY) if name in in_hbm else _const_spec(a.shape))
    return [spec(name, a) for name, a in zip(_WEIGHT_NAMES, ws)]


def _rope_tables(pos, reps=1):
    half = ROPE_DIM // 2
    inv = np.power(ROPE_THETA, -np.arange(half, dtype=np.float64) * (2.0 / ROPE_DIM))
    ang = np.asarray(pos, np.float64)[:, None] * inv[None, :]
    cos, sin = np.cos(ang), np.sin(ang)
    n = ang.shape[0]
    pad = HEAD_DIM - ROPE_DIM
    c = np.concatenate([cos, cos, np.ones((n, pad))], 1)
    sa = np.concatenate([-sin, np.zeros((n, half + pad))], 1)
    sb = np.concatenate([np.zeros((n, half)), sin, np.zeros((n, pad))], 1)
    return [jnp.asarray(np.tile(t, (reps, LANES // HEAD_DIM)), F32) for t in (c, sa, sb)]


def _gate_params(w_s, b_s, lb):
    i = jnp.arange(lb)
    mask = (i[None, :] // CHUNK) <= (i[:, None] // CHUNK)
    w = jnp.where(mask[None], w_s[:, :lb, :lb], 0.0)
    wpair = jnp.concatenate([w[0::2], w[1::2]], axis=2).astype(BF16)
    bias = jnp.repeat(jnp.transpose(b_s[:, :lb]), CMLP_GROUP_DIM, axis=1)
    return wpair, bias


def kernel(x_prompt, x_sample, cache_win_k, cache_win_v, w_in, ln_v_g, ln_v_b, attn_sinks, w_spatial, b_spatial,
           norm_attn_g, norm_cmlp_g, w_out, ln1_g, ln1_b, w_gate_up, w_down, ln2_g, ln2_b):
    B, S, _ = x_prompt.shape
    Bd, L, _ = x_sample.shape
    assert cache_win_k.shape[2] == WINDOW and L == CHUNK and w_in.shape[0] == 1

    row = lambda a: a[0].reshape(1, -1)

    w_in_b, w_out_b = w_in[0].astype(BF16), w_out[0].astype(BF16)

    def weights(lb, wgu, wd):
        wpair, bias = _gate_params(w_spatial[0], b_spatial[0], lb)
        return _Weights(w_in=w_in_b, lnv_g=row(ln_v_g), lnv_b=row(ln_v_b), sinks=attn_sinks[0],
                        wpair=wpair, bias=bias, nga=row(norm_attn_g), ngc=row(norm_cmlp_g),
                        w_out=w_out_b, ln1g=row(ln1_g), ln1b=row(ln1_b),
                        wgu=wgu, wd=wd, ln2g=row(ln2_g), ln2b=row(ln2_b))

    T = PROMPT_TILE
    tps = S // T
    n_tiles = B * tps
    wp = weights(CMLP_BLOCK, w_gate_up[0], w_down[0])
    tables = _rope_tables(np.arange(S))
    mix_tile = lambda s: jnp.minimum(s, n_tiles - 1)
    ffn_tile = lambda s: jnp.maximum(s - 1, 0)
    tab = pl.BlockSpec((T, LANES), lambda s: (mix_tile(s) % tps, 0))
    win = pl.BlockSpec((1, WINDOW, D_KV), lambda s: (mix_tile(s) // tps, 0, 0))
    hbm = pl.BlockSpec(memory_space=pl.ANY)
    y_p, kwin, vwin, wgu_b, wd_b = pl.pallas_call(
        functools.partial(_prompt_body, T, tps, n_tiles),
        grid=(n_tiles + 1,),
        in_specs=[pl.BlockSpec((T, D_MODEL), lambda s: (mix_tile(s), 0)), tab, tab, tab]
        + _weight_specs(wp, in_hbm=("wgu", "wd")),
        out_specs=[pl.BlockSpec((T, D_MODEL), lambda s: (ffn_tile(s), 0)), win, win, hbm, hbm],
        out_shape=[jax.ShapeDtypeStruct((B * S, D_MODEL), F32),
                   jax.ShapeDtypeStruct((B, WINDOW, D_KV), F32),
                   jax.ShapeDtypeStruct((B, WINDOW, D_KV), F32),
                   jax.ShapeDtypeStruct((D_MODEL, 2 * D_FF), BF16),
                   jax.ShapeDtypeStruct((D_FF, D_MODEL), BF16)],
        scratch_shapes=[pltpu.VMEM((2, N_KV_HEADS, WINDOW, 2 * LANES), BF16),
                        pltpu.VMEM((2, N_KV_HEADS, WINDOW, 2 * LANES), BF16),
                        pltpu.VMEM((2, T, D_MODEL), F32),
                        pltpu.VMEM((D_MODEL, 2 * D_FF), BF16),
                        pltpu.VMEM((D_FF, D_MODEL), BF16),
                        pltpu.VMEM((W_SLOTS, WGU_ROWS, 2 * D_FF), F32),
                        pltpu.VMEM((W_SLOTS, WD_ROWS, D_MODEL), F32),
                        pltpu.SemaphoreType.DMA((W_SLOTS,)),
                        pltpu.SemaphoreType.DMA((W_SLOTS,)),
                        pltpu.SemaphoreType.DMA((2,))],
        compiler_params=pltpu.CompilerParams(dimension_semantics=("arbitrary",), vmem_limit_bytes=VMEM_LIMIT),
        name="layer_prompt",
    )(x_prompt.reshape(B * S, D_MODEL), *tables, *wp)
    y_p = y_p.reshape(B, S, D_MODEL)

    NS = SAMPLE_SEQS
    Ts = NS * L
    ws = weights(L, wgu_b, wd_b)
    tables_s = _rope_tables(PAST_LEN + np.arange(L), reps=NS)
    n_st = Bd // NS
    mix_rows = lambda n, width: pl.BlockSpec((n, width), lambda s: (jnp.minimum(s, n_st - 1), 0))
    cache_rows = NS * WINDOW * N_KV_HEADS
    y_s, k_s, v_s, vm_s = pl.pallas_call(
        functools.partial(_sample_body, n_st),
        grid=(n_st + 1,),
        in_specs=[mix_rows(Ts, D_MODEL)] + [_const_spec(t.shape) for t in tables_s]
        + [mix_rows(cache_rows, HEAD_DIM), mix_rows(cache_rows, HEAD_DIM)] + _weight_specs(ws),
        out_specs=[pl.BlockSpec((Ts, D_MODEL), lambda s: (jnp.maximum(s - 1, 0), 0)),
                   mix_rows(Ts, D_KV), mix_rows(Ts, D_KV), mix_rows(Ts, D_CMLP)],
        out_shape=[jax.ShapeDtypeStruct((Bd * L, D_MODEL), F32),
                   jax.ShapeDtypeStruct((Bd * L, D_KV), F32),
                   jax.ShapeDtypeStruct((Bd * L, D_KV), F32),
                   jax.ShapeDtypeStruct((Bd * L, D_CMLP), F32)],
        scratch_shapes=[pltpu.VMEM((2, Ts, D_MODEL), F32)],
        compiler_params=pltpu.CompilerParams(dimension_semantics=("arbitrary",), vmem_limit_bytes=VMEM_LIMIT),
        name="layer_sample",
    )(x_sample.reshape(Bd * L, D_MODEL), *tables_s,
      cache_win_k.reshape(Bd * WINDOW * N_KV_HEADS, HEAD_DIM),
      cache_win_v.reshape(Bd * WINDOW * N_KV_HEADS, HEAD_DIM), *ws)

    return (y_p, y_s.reshape(Bd, L, D_MODEL),
            kwin.reshape(1, B, WINDOW, N_KV_HEADS, HEAD_DIM),
            vwin.reshape(1, B, WINDOW, N_KV_HEADS, HEAD_DIM),
            k_s.reshape(1, Bd, L, N_KV_HEADS, HEAD_DIM),
            v_s.reshape(1, Bd, L, N_KV_HEADS, HEAD_DIM),
            vm_s.reshape(1, Bd, L, CMLP_GROUPS, CMLP_GROUP_DIM))
```

```python
import collections
import functools

import jax
import jax.numpy as jnp
import numpy as np
from jax import lax
from jax.experimental import pallas as pl
from jax.experimental.pallas import tpu as pltpu

D_MODEL = 1024
CHUNK = 64
HEAD_DIM = 64
D_ATTN = 512
D_CMLP = 512
N_HEADS = 8
N_KV_HEADS = 2
GQA_GROUP = 4
D_KV = 128
WINDOW = 128
ROPE_DIM = 16
ROPE_THETA = 500000.0
CMLP_BLOCK = 128
CMLP_GROUPS = 8
CMLP_GROUP_DIM = 64
D_IN = 1792
D_FF = 2816
PAST_LEN = 1024
ALPHA = 2.0 ** 0.25
LN_EPS = 1e-5
NEG_INF = -1e30
LANES = 128
FF_CHUNK = 256
VMEM_LIMIT = 56 * 1024 * 1024
PROMPT_TILE = 512
SAMPLE_SEQS = 4

F32 = jnp.float32
BF16 = jnp.bfloat16


def _layernorm(x, g, b):
    mu = jnp.mean(x, -1, keepdims=True)
    d = x - mu
    var = jnp.mean(d * d, -1, keepdims=True)
    return d * lax.rsqrt(var + LN_EPS) * g + b


def _rmsnorm(x, g):
    ms = jnp.mean(x * x, -1, keepdims=True)
    return x * lax.rsqrt(ms + LN_EPS) * g


def _dot(a, b):
    return jnp.dot(a, b, preferred_element_type=F32)


def _rope(x, c, sa, sb):
    out = []
    for i in range(x.shape[1] // LANES):
        s = x[:, LANES * i:LANES * (i + 1)]
        up = pltpu.roll(s, LANES - ROPE_DIM // 2, 1)
        dn = pltpu.roll(s, ROPE_DIM // 2, 1)
        out.append(s * c + up * sa + dn * sb)
    return out[0] if len(out) == 1 else jnp.concatenate(out, axis=1)


def _row_halves(n):
    return (slice(0, n // 2), slice(n // 2, n))


def _rep4(a):
    ar = pltpu.roll(a, HEAD_DIM, 1)
    first = lax.broadcasted_iota(jnp.int32, a.shape, 1) < HEAD_DIM
    g0 = jnp.where(first, a, ar).astype(BF16)
    g1 = jnp.where(first, ar, a).astype(BF16)
    return [jnp.concatenate([g0, g0], axis=1), jnp.concatenate([g1, g1], axis=1)]


def _attend_windows(windows, first_valid, sinks, q, k, v):
    T = q.shape[0]
    k_rep, v_rep = _rep4(k), _rep4(v)
    head_lane = lax.broadcasted_iota(jnp.int32, (CHUNK, GQA_GROUP * HEAD_DIM), 1) // HEAD_DIM
    key_idx = lax.broadcasted_iota(jnp.int32, (GQA_GROUP * CHUNK, WINDOW + CHUNK), 1)
    ao_rows = []
    for ci in range(T // CHUNK):
        ao_g = []
        for g in range(N_KV_HEADS):
            qc = q[CHUNK * ci:CHUNK * (ci + 1), 256 * g:256 * (g + 1)]
            qm = jnp.concatenate([jnp.where(head_lane == h, qc, 0.0) for h in range(GQA_GROUP)], axis=0).astype(BF16)
            kw, vw = windows(k_rep, v_rep, g, ci)
            s = lax.dot_general(qm, kw, (((1,), (1,)), ((), ())), preferred_element_type=F32)
            fv = first_valid(ci)
            if fv is not None:
                s = jnp.where(key_idx >= fv, s, NEG_INF)
            ps = []
            for h in range(GQA_GROUP):
                sh = s[CHUNK * h:CHUNK * (h + 1)]
                sink = sinks[GQA_GROUP * g + h]
                m = jnp.maximum(jnp.max(sh, -1, keepdims=True), sink)
                p = jnp.exp(sh - m)
                denom = jnp.sum(p, -1, keepdims=True) + jnp.exp(sink - m)
                ps.append(p * (1.0 / denom))
            pm = jnp.concatenate(ps, axis=0).astype(BF16)
            r = _dot(pm, vw)
            o = jnp.where(head_lane == 0, r[0:CHUNK], 0.0)
            for h in range(1, GQA_GROUP):
                o = o + jnp.where(head_lane == h, r[CHUNK * h:CHUNK * (h + 1)], 0.0)
            ao_g.append(o)
        ao_rows.append(jnp.concatenate(ao_g, axis=1))
        yield
    return jnp.concatenate(ao_rows, axis=0), (k_rep, v_rep)


def _mixer(x, c, sa, sb, attend, LB, w):
    T = x.shape[0]
    nb = T // LB
    xb = x.astype(BF16)
    uv = jax.nn.gelu(_dot(xb, w.w_in[:, D_ATTN + 2 * D_KV:D_IN]))
    yield
    q = _dot(xb, w.w_in[:, 0:D_ATTN])
    kv = _dot(xb, w.w_in[:, D_ATTN:D_ATTN + 2 * D_KV])

    k = _rope(kv[:, :D_KV], c, sa, sb)
    v = kv[:, D_KV:]
    q = _rope(q, c, sa, sb) * (HEAD_DIM ** -0.5)

    yield
    u = uv[:, :D_CMLP]
    vm = _layernorm(uv[:, D_CMLP:], w.lnv_g[...], w.lnv_b[...])

    lo_mask = lax.broadcasted_iota(jnp.int32, (T, LANES), 1) < CMLP_GROUP_DIM
    gate_slabs = []
    for p in range(D_CMLP // LANES):
        slab = vm[:, LANES * p:LANES * (p + 1)]
        lo = jnp.where(lo_mask, slab, 0.0).astype(BF16)
        hi = jnp.where(lo_mask, 0.0, slab).astype(BF16)
        rhs = jnp.concatenate(
            [jnp.concatenate([lo[LB * b:LB * (b + 1)], hi[LB * b:LB * (b + 1)]], axis=0) for b in range(nb)], axis=1)
        o = _dot(w.wpair[p], rhs)
        gate_slabs.append(jnp.concatenate([o[:, LANES * b:LANES * (b + 1)] for b in range(nb)], axis=0))
    s_gate = jnp.concatenate(gate_slabs, axis=1) + jnp.concatenate([w.bias[...]] * nb, axis=0)
    co = u * s_gate
    yield

    ao, carry = yield from attend(q, k, v)

    an = _rmsnorm(ao, w.nga[...]).astype(BF16)
    cn = _rmsnorm(co, w.ngc[...]).astype(BF16)
    h1 = []
    for r in _row_halves(T):
        mix = _dot(an[r], w.w_out[0:D_ATTN, :]) + _dot(cn[r], w.w_out[D_ATTN:, :])
        yield
        h1.append(_layernorm(ALPHA * x[r] + mix, w.ln1g[...], w.ln1b[...]))
    return jnp.concatenate(h1, axis=0), k, v, vm, carry


def _ffn(h1, w):
    hb = h1.astype(BF16)
    acts = []
    for lo_c in range(0, D_FF, FF_CHUNK):
        hi_c = min(lo_c + FF_CHUNK, D_FF)
        gt = _dot(hb, w.wgu[:, lo_c:hi_c])
        up = _dot(hb, w.wgu[:, D_FF + lo_c:D_FF + hi_c])
        acts.append((gt * jax.nn.sigmoid(gt) * up).astype(BF16))
        yield
    act = jnp.concatenate(acts, axis=1)
    y = []
    for r in _row_halves(h1.shape[0]):
        f = _dot(act[r], w.wd[...])
        yield
        y.append(_layernorm(ALPHA * h1[r] + f, w.ln2g[...], w.ln2b[...]))
    return jnp.concatenate(y, axis=0)


def _run(*gens):
    results = [None] * len(gens)
    live = list(range(len(gens)))
    while live:
        for i in list(live):
            try:
                next(gens[i])
            except StopIteration as done:
                results[i] = done.value
                live.remove(i)
    return results


_WEIGHT_NAMES = ("w_in", "lnv_g", "lnv_b", "sinks", "wpair", "bias", "nga", "ngc", "w_out", "ln1g", "ln1b",
                 "wgu", "wd", "ln2g", "ln2b")
_Weights = collections.namedtuple("_Weights", _WEIGHT_NAMES)
N_W = len(_WEIGHT_NAMES)


def _pipelined_step(step, n_tiles, h1_ref, w, make_mixer, store_y, store_mixer):
    cur = step % 2
    prev = 1 - cur

    def phase(do_mixer, do_ffn):
        gens = ([_ffn(h1_ref[prev], w)] if do_ffn else []) + ([make_mixer()] if do_mixer else [])
        res = _run(*gens)
        if do_ffn:
            store_y(res[0])
        if do_mixer:
            h1_ref[cur] = res[-1][0]
            store_mixer(*res[-1][1:])

    pl.when(step == 0)(functools.partial(phase, True, False))
    pl.when(jnp.logical_and(step > 0, step < n_tiles))(functools.partial(phase, True, True))
    pl.when(step == n_tiles)(functools.partial(phase, False, True))


def _prompt_body(T, tiles_per_seq, n_tiles, *refs):
    x_ref, c_ref, sa_ref, sb_ref = refs[:4]
    w = _Weights(*refs[4:4 + N_W])
    y_ref, k_out_ref, v_out_ref, hk_ref, hv_ref, h1_ref = refs[4 + N_W:]
    step = pl.program_id(0)
    j = step % tiles_per_seq
    rd = j % 2
    wr = 1 - rd

    @pl.when(j == 0)
    def _():
        hk_ref[0] = jnp.zeros(hk_ref.shape[1:], BF16)
        hv_ref[0] = jnp.zeros(hv_ref.shape[1:], BF16)

    def first_valid(ci):
        if ci >= WINDOW // CHUNK:
            return None
        return jnp.where(j == 0, (WINDOW // CHUNK - ci) * CHUNK, 0)

    def make_mixer():
        hist_k = [hk_ref[rd, g] for g in range(N_KV_HEADS)]
        hist_v = [hv_ref[rd, g] for g in range(N_KV_HEADS)]

        def windows(k_rep, v_rep, g, ci):
            k_all = jnp.concatenate([hist_k[g], k_rep[g]], axis=0)
            v_all = jnp.concatenate([hist_v[g], v_rep[g]], axis=0)
            lo = CHUNK * ci
            return k_all[lo:lo + WINDOW + CHUNK], v_all[lo:lo + WINDOW + CHUNK]

        attend = functools.partial(_attend_windows, windows, first_valid, w.sinks)
        return _mixer(x_ref[...], c_ref[...], sa_ref[...], sb_ref[...], attend, CMLP_BLOCK, w)

    def store_y(y):
        y_ref[...] = y

    def store_mixer(k, v, _, carry):
        k_out_ref[0] = k[T - WINDOW:, :]
        v_out_ref[0] = v[T - WINDOW:, :]
        k_rep, v_rep = carry
        for g in range(N_KV_HEADS):
            hk_ref[wr, g] = k_rep[g][T - WINDOW:]
            hv_ref[wr, g] = v_rep[g][T - WINDOW:]

    _pipelined_step(step, n_tiles, h1_ref, w, make_mixer, store_y, store_mixer)


def _cache_rep4(ref):
    n = ref.shape[0] // N_KV_HEADS
    out = []
    for g in range(N_KV_HEADS):
        a = ref[pl.ds(g, n, stride=N_KV_HEADS), :]
        a2 = jnp.concatenate([a, a], axis=1).astype(BF16)
        out.append(jnp.concatenate([a2, a2], axis=1))
    return out


def _sample_body(n_tiles, *refs):
    x_ref, c_ref, sa_ref, sb_ref, ck_ref, cv_ref = refs[:6]
    w = _Weights(*refs[6:6 + N_W])
    y_ref, k_out_ref, v_out_ref, vm_out_ref, h1_ref = refs[6 + N_W:]

    def make_mixer():
        ck_rep, cv_rep = _cache_rep4(ck_ref), _cache_rep4(cv_ref)

        def windows(k_rep, v_rep, g, ci):
            cat = lambda cache, new: jnp.concatenate(
                [cache[g][WINDOW * ci:WINDOW * (ci + 1)], new[g][CHUNK * ci:CHUNK * (ci + 1)]], axis=0)
            return cat(ck_rep, k_rep), cat(cv_rep, v_rep)

        attend = functools.partial(_attend_windows, windows, lambda ci: None, w.sinks)
        return _mixer(x_ref[...], c_ref[...], sa_ref[...], sb_ref[...], attend, CHUNK, w)

    def store_y(y):
        y_ref[...] = y

    def scatter_heads(out_ref, a):
        n = a.shape[1] // HEAD_DIM
        for i in range(n):
            out_ref[pl.ds(i, a.shape[0], stride=n), :] = a[:, HEAD_DIM * i:HEAD_DIM * (i + 1)]

    def store_mixer(k, v, vm, *_):
        scatter_heads(k_out_ref, k)
        scatter_heads(v_out_ref, v)
        scatter_heads(vm_out_ref, vm)

    _pipelined_step(pl.program_id(0), n_tiles, h1_ref, w, make_mixer, store_y, store_mixer)


def _const_spec(shape):
    nd = len(shape)
    return pl.BlockSpec(shape, lambda *_: (0,) * nd)


def _weight_specs(ws):
    smem = pl.BlockSpec(memory_space=pltpu.SMEM)
    return [smem if name == "sinks" else _const_spec(a.shape) for name, a in zip(_WEIGHT_NAMES, ws)]


def _rope_tables(pos, reps=1):
    half = ROPE_DIM // 2
    inv = np.power(ROPE_THETA, -np.arange(half, dtype=np.float64) * (2.0 / ROPE_DIM))
    ang = np.asarray(pos, np.float64)[:, None] * inv[None, :]
    cos, sin = np.cos(ang), np.sin(ang)
    n = ang.shape[0]
    pad = HEAD_DIM - ROPE_DIM
    c = np.concatenate([cos, cos, np.ones((n, pad))], 1)
    sa = np.concatenate([-sin, np.zeros((n, half + pad))], 1)
    sb = np.concatenate([np.zeros((n, half)), sin, np.zeros((n, pad))], 1)
    return [jnp.asarray(np.tile(t, (reps, LANES // HEAD_DIM)), F32) for t in (c, sa, sb)]


def _gate_params(w_s, b_s, lb):
    i = jnp.arange(lb)
    mask = (i[None, :] // CHUNK) <= (i[:, None] // CHUNK)
    w = jnp.where(mask[None], w_s[:, :lb, :lb], 0.0)
    wpair = jnp.concatenate([w[0::2], w[1::2]], axis=2).astype(BF16)
    bias = jnp.repeat(jnp.transpose(b_s[:, :lb]), CMLP_GROUP_DIM, axis=1)
    return wpair, bias


def kernel(x_prompt, x_sample, cache_win_k, cache_win_v, w_in, ln_v_g, ln_v_b, attn_sinks, w_spatial, b_spatial,
           norm_attn_g, norm_cmlp_g, w_out, ln1_g, ln1_b, w_gate_up, w_down, ln2_g, ln2_b):
    B, S, _ = x_prompt.shape
    Bd, L, _ = x_sample.shape
    assert cache_win_k.shape[2] == WINDOW and L == CHUNK and w_in.shape[0] == 1

    row = lambda a: a[0].reshape(1, -1)

    def weights(lb):
        wpair, bias = _gate_params(w_spatial[0], b_spatial[0], lb)
        return _Weights(w_in=w_in[0].astype(BF16), lnv_g=row(ln_v_g), lnv_b=row(ln_v_b), sinks=attn_sinks[0],
                        wpair=wpair, bias=bias, nga=row(norm_attn_g), ngc=row(norm_cmlp_g),
                        w_out=w_out[0].astype(BF16), ln1g=row(ln1_g), ln1b=row(ln1_b),
                        wgu=w_gate_up[0].astype(BF16), wd=w_down[0].astype(BF16), ln2g=row(ln2_g), ln2b=row(ln2_b))

    T = PROMPT_TILE
    tps = S // T
    n_tiles = B * tps
    wp = weights(CMLP_BLOCK)
    tables = _rope_tables(np.arange(S))
    mix_tile = lambda s: jnp.minimum(s, n_tiles - 1)
    ffn_tile = lambda s: jnp.maximum(s - 1, 0)
    tab = pl.BlockSpec((T, LANES), lambda s: (mix_tile(s) % tps, 0))
    win = pl.BlockSpec((1, WINDOW, D_KV), lambda s: (mix_tile(s) // tps, 0, 0))
    y_p, kwin, vwin = pl.pallas_call(
        functools.partial(_prompt_body, T, tps, n_tiles),
        grid=(n_tiles + 1,),
        in_specs=[pl.BlockSpec((T, D_MODEL), lambda s: (mix_tile(s), 0)), tab, tab, tab] + _weight_specs(wp),
        out_specs=[pl.BlockSpec((T, D_MODEL), lambda s: (ffn_tile(s), 0)), win, win],
        out_shape=[jax.ShapeDtypeStruct((B * S, D_MODEL), F32),
                   jax.ShapeDtypeStruct((B, WINDOW, D_KV), F32),
                   jax.ShapeDtypeStruct((B, WINDOW, D_KV), F32)],
        scratch_shapes=[pltpu.VMEM((2, N_KV_HEADS, WINDOW, 2 * LANES), BF16),
                        pltpu.VMEM((2, N_KV_HEADS, WINDOW, 2 * LANES), BF16),
                        pltpu.VMEM((2, T, D_MODEL), F32)],
        compiler_params=pltpu.CompilerParams(dimension_semantics=("arbitrary",), vmem_limit_bytes=VMEM_LIMIT),
        name="layer_prompt",
    )(x_prompt.reshape(B * S, D_MODEL), *tables, *wp)
    y_p = y_p.reshape(B, S, D_MODEL)

    NS = SAMPLE_SEQS
    Ts = NS * L
    ws = weights(L)
    tables_s = _rope_tables(PAST_LEN + np.arange(L), reps=NS)
    n_st = Bd // NS
    mix_rows = lambda n, width: pl.BlockSpec((n, width), lambda s: (jnp.minimum(s, n_st - 1), 0))
    cache_rows = NS * WINDOW * N_KV_HEADS
    y_s, k_s, v_s, vm_s = pl.pallas_call(
        functools.partial(_sample_body, n_st),
        grid=(n_st + 1,),
        in_specs=[mix_rows(Ts, D_MODEL)] + [_const_spec(t.shape) for t in tables_s]
        + [mix_rows(cache_rows, HEAD_DIM), mix_rows(cache_rows, HEAD_DIM)] + _weight_specs(ws),
        out_specs=[pl.BlockSpec((Ts, D_MODEL), lambda s: (jnp.maximum(s - 1, 0), 0)),
                   mix_rows(Ts * N_KV_HEADS, HEAD_DIM), mix_rows(Ts * N_KV_HEADS, HEAD_DIM),
                   mix_rows(Ts * CMLP_GROUPS, CMLP_GROUP_DIM)],
        out_shape=[jax.ShapeDtypeStruct((Bd * L, D_MODEL), F32),
                   jax.ShapeDtypeStruct((Bd * L * N_KV_HEADS, HEAD_DIM), F32),
                   jax.ShapeDtypeStruct((Bd * L * N_KV_HEADS, HEAD_DIM), F32),
                   jax.ShapeDtypeStruct((Bd * L * CMLP_GROUPS, CMLP_GROUP_DIM), F32)],
        scratch_shapes=[pltpu.VMEM((2, Ts, D_MODEL), F32)],
        compiler_params=pltpu.CompilerParams(dimension_semantics=("arbitrary",), vmem_limit_bytes=VMEM_LIMIT),
        name="layer_sample",
    )(x_sample.reshape(Bd * L, D_MODEL), *tables_s,
      cache_win_k.reshape(Bd * WINDOW * N_KV_HEADS, HEAD_DIM),
      cache_win_v.reshape(Bd * WINDOW * N_KV_HEADS, HEAD_DIM), *ws)

    return (y_p, y_s.reshape(Bd, L, D_MODEL),
            kwin.reshape(1, B, WINDOW, N_KV_HEADS, HEAD_DIM),
            vwin.reshape(1, B, WINDOW, N_KV_HEADS, HEAD_DIM),
            k_s.reshape(1, Bd, L, N_KV_HEADS, HEAD_DIM),
            v_s.reshape(1, Bd, L, N_KV_HEADS, HEAD_DIM),
            vm_s.reshape(1, Bd, L, CMLP_GROUPS, CMLP_GROUP_DIM))
```

```python
import collections
import functools

import jax
import jax.numpy as jnp
import numpy as np
from jax import lax
from jax.experimental import pallas as pl
from jax.experimental.pallas import tpu as pltpu

D_MODEL = 1024
CHUNK = 64
HEAD_DIM = 64
D_ATTN = 512
D_CMLP = 512
N_HEADS = 8
N_KV_HEADS = 2
GQA_GROUP = 4
D_KV = 128
WINDOW = 128
ROPE_DIM = 16
ROPE_THETA = 500000.0
CMLP_BLOCK = 128
CMLP_GROUPS = 8
CMLP_GROUP_DIM = 64
D_IN = 1792
D_FF = 2816
PAST_LEN = 1024
ALPHA = 2.0 ** 0.25
LN_EPS = 1e-5
NEG_INF = -1e30
LANES = 128
FF_CHUNK = 256
VMEM_LIMIT = 60 * 1024 * 1024
PROMPT_TILE = 512
SAMPLE_SEQS = 4

F32 = jnp.float32
BF16 = jnp.bfloat16


def _layernorm(x, g, b):
    mu = jnp.mean(x, -1, keepdims=True)
    d = x - mu
    var = jnp.mean(d * d, -1, keepdims=True)
    return d * lax.rsqrt(var + LN_EPS) * g + b


def _rmsnorm(x, g):
    ms = jnp.mean(x * x, -1, keepdims=True)
    return x * lax.rsqrt(ms + LN_EPS) * g


def _dot(a, b):
    return jnp.dot(a, b, preferred_element_type=F32)


def _rope(x, c, sa, sb):
    out = []
    for i in range(x.shape[1] // LANES):
        s = x[:, LANES * i:LANES * (i + 1)]
        up = pltpu.roll(s, LANES - ROPE_DIM // 2, 1)
        dn = pltpu.roll(s, ROPE_DIM // 2, 1)
        out.append(s * c + up * sa + dn * sb)
    return out[0] if len(out) == 1 else jnp.concatenate(out, axis=1)


def _row_halves(n):
    return (slice(0, n // 2), slice(n // 2, n))


def _rep4(a):
    ar = pltpu.roll(a, HEAD_DIM, 1)
    first = lax.broadcasted_iota(jnp.int32, a.shape, 1) < HEAD_DIM
    g0 = jnp.where(first, a, ar).astype(BF16)
    g1 = jnp.where(first, ar, a).astype(BF16)
    return [jnp.concatenate([g0, g0], axis=1), jnp.concatenate([g1, g1], axis=1)]


def _attend_windows(windows, first_valid, sinks, q, k, v):
    T = q.shape[0]
    k_rep, v_rep = _rep4(k), _rep4(v)
    head_lane = lax.broadcasted_iota(jnp.int32, (CHUNK, GQA_GROUP * HEAD_DIM), 1) // HEAD_DIM
    key_idx = lax.broadcasted_iota(jnp.int32, (GQA_GROUP * CHUNK, WINDOW + CHUNK), 1)
    ao_rows = []
    for ci in range(T // CHUNK):
        ao_g = []
        for g in range(N_KV_HEADS):
            qc = q[CHUNK * ci:CHUNK * (ci + 1), 256 * g:256 * (g + 1)]
            qm = jnp.concatenate([jnp.where(head_lane == h, qc, 0.0) for h in range(GQA_GROUP)], axis=0).astype(BF16)
            kw, vw = windows(k_rep, v_rep, g, ci)
            s = lax.dot_general(qm, kw, (((1,), (1,)), ((), ())), preferred_element_type=F32)
            fv = first_valid(ci)
            if fv is not None:
                s = jnp.where(key_idx >= fv, s, NEG_INF)
            ps = []
            for h in range(GQA_GROUP):
                sh = s[CHUNK * h:CHUNK * (h + 1)]
                sink = sinks[GQA_GROUP * g + h]
                m = jnp.maximum(jnp.max(sh, -1, keepdims=True), sink)
                p = jnp.exp(sh - m)
                denom = jnp.sum(p, -1, keepdims=True) + jnp.exp(sink - m)
                ps.append(p * (1.0 / denom))
            pm = jnp.concatenate(ps, axis=0).astype(BF16)
            r = _dot(pm, vw)
            o = jnp.where(head_lane == 0, r[0:CHUNK], 0.0)
            for h in range(1, GQA_GROUP):
                o = o + jnp.where(head_lane == h, r[CHUNK * h:CHUNK * (h + 1)], 0.0)
            ao_g.append(o)
        ao_rows.append(jnp.concatenate(ao_g, axis=1))
        yield
    return jnp.concatenate(ao_rows, axis=0), (k_rep, v_rep)


def _mixer(x, c, sa, sb, attend, LB, w):
    T = x.shape[0]
    nb = T // LB
    xb = x.astype(BF16)
    uv = jax.nn.gelu(_dot(xb, w.w_in[:, D_ATTN + 2 * D_KV:D_IN]))
    yield
    q = _dot(xb, w.w_in[:, 0:D_ATTN])
    kv = _dot(xb, w.w_in[:, D_ATTN:D_ATTN + 2 * D_KV])

    k = _rope(kv[:, :D_KV], c, sa, sb)
    v = kv[:, D_KV:]
    q = _rope(q, c, sa, sb) * (HEAD_DIM ** -0.5)

    yield
    u = uv[:, :D_CMLP]
    vm = _layernorm(uv[:, D_CMLP:], w.lnv_g[...], w.lnv_b[...])

    lo_mask = lax.broadcasted_iota(jnp.int32, (T, LANES), 1) < CMLP_GROUP_DIM
    gate_slabs = []
    for p in range(D_CMLP // LANES):
        slab = vm[:, LANES * p:LANES * (p + 1)]
        lo = jnp.where(lo_mask, slab, 0.0).astype(BF16)
        hi = jnp.where(lo_mask, 0.0, slab).astype(BF16)
        rhs = jnp.concatenate(
            [jnp.concatenate([lo[LB * b:LB * (b + 1)], hi[LB * b:LB * (b + 1)]], axis=0) for b in range(nb)], axis=1)
        o = _dot(w.wpair[p], rhs)
        gate_slabs.append(jnp.concatenate([o[:, LANES * b:LANES * (b + 1)] for b in range(nb)], axis=0))
    s_gate = jnp.concatenate(gate_slabs, axis=1) + jnp.concatenate([w.bias[...]] * nb, axis=0)
    co = u * s_gate
    yield

    ao, carry = yield from attend(q, k, v)

    an = _rmsnorm(ao, w.nga[...]).astype(BF16)
    cn = _rmsnorm(co, w.ngc[...]).astype(BF16)
    h1 = []
    for r in _row_halves(T):
        mix = _dot(an[r], w.w_out[0:D_ATTN, :]) + _dot(cn[r], w.w_out[D_ATTN:, :])
        yield
        h1.append(_layernorm(ALPHA * x[r] + mix, w.ln1g[...], w.ln1b[...]))
    return jnp.concatenate(h1, axis=0), k, v, vm, carry


def _ffn(h1, w):
    hb = h1.astype(BF16)
    acts = []
    for lo_c in range(0, D_FF, FF_CHUNK):
        hi_c = min(lo_c + FF_CHUNK, D_FF)
        gt = _dot(hb, w.wgu[:, lo_c:hi_c])
        up = _dot(hb, w.wgu[:, D_FF + lo_c:D_FF + hi_c])
        acts.append((gt * jax.nn.sigmoid(gt) * up).astype(BF16))
        yield
    act = jnp.concatenate(acts, axis=1)
    y = []
    for r in _row_halves(h1.shape[0]):
        f = _dot(act[r], w.wd[...])
        yield
        y.append(_layernorm(ALPHA * h1[r] + f, w.ln2g[...], w.ln2b[...]))
    return jnp.concatenate(y, axis=0)


def _run(*gens):
    results = [None] * len(gens)
    live = list(range(len(gens)))
    while live:
        for i in list(live):
            try:
                next(gens[i])
            except StopIteration as done:
                results[i] = done.value
                live.remove(i)
    return results


_WEIGHT_NAMES = ("w_in", "lnv_g", "lnv_b", "sinks", "wpair", "bias", "nga", "ngc", "w_out", "ln1g", "ln1b",
                 "wgu", "wd", "ln2g", "ln2b")
_Weights = collections.namedtuple("_Weights", _WEIGHT_NAMES)
N_W = len(_WEIGHT_NAMES)


def _scatter_heads(out_ref, a):
    n = a.shape[1] // HEAD_DIM
    for i in range(n):
        out_ref[pl.ds(i, a.shape[0], stride=n), :] = a[:, HEAD_DIM * i:HEAD_DIM * (i + 1)]


def _cache_rep4(ref):
    n = ref.shape[0] // N_KV_HEADS
    out = []
    for g in range(N_KV_HEADS):
        a = ref[pl.ds(g, n, stride=N_KV_HEADS), :]
        a2 = jnp.concatenate([a, a], axis=1).astype(BF16)
        out.append(jnp.concatenate([a2, a2], axis=1))
    return out


def _layer_body(T, Ts, tiles_per_seq, n_p, n_s, *refs):
    xp_ref, cp_ref, sap_ref, sbp_ref, xs_ref, cs_ref, sas_ref, sbs_ref, ck_ref, cv_ref = refs[:10]
    w = _Weights(*refs[10:10 + N_W])
    wpair_s_ref, bias_s_ref = refs[10 + N_W:12 + N_W]
    (yp_ref, kwin_ref, vwin_ref, ys_ref, ks_ref, vs_ref, vms_ref, hk_ref, hv_ref, h1_ref) = refs[12 + N_W:]
    w_s = w._replace(wpair=wpair_s_ref, bias=bias_s_ref)
    step = pl.program_id(0)
    cur = step % 2
    prev = 1 - cur
    j = step % tiles_per_seq
    rd = j % 2
    wr = 1 - rd

    @pl.when(j == 0)
    def _():
        hk_ref[0] = jnp.zeros(hk_ref.shape[1:], BF16)
        hv_ref[0] = jnp.zeros(hv_ref.shape[1:], BF16)

    def first_valid(ci):
        if ci >= WINDOW // CHUNK:
            return None
        return jnp.where(j == 0, (WINDOW // CHUNK - ci) * CHUNK, 0)

    def prompt_mixer():
        hist_k = [hk_ref[rd, g] for g in range(N_KV_HEADS)]
        hist_v = [hv_ref[rd, g] for g in range(N_KV_HEADS)]

        def windows(k_rep, v_rep, g, ci):
            k_all = jnp.concatenate([hist_k[g], k_rep[g]], axis=0)
            v_all = jnp.concatenate([hist_v[g], v_rep[g]], axis=0)
            lo = CHUNK * ci
            return k_all[lo:lo + WINDOW + CHUNK], v_all[lo:lo + WINDOW + CHUNK]

        attend = functools.partial(_attend_windows, windows, first_valid, w.sinks)
        return _mixer(xp_ref[...], cp_ref[...], sap_ref[...], sbp_ref[...], attend, CMLP_BLOCK, w)

    def prompt_store(h1, k, v, _, carry):
        h1_ref[cur] = h1
        _scatter_heads(kwin_ref, k[T - WINDOW:, :])
        _scatter_heads(vwin_ref, v[T - WINDOW:, :])
        k_rep, v_rep = carry
        for g in range(N_KV_HEADS):
            hk_ref[wr, g] = k_rep[g][T - WINDOW:]
            hv_ref[wr, g] = v_rep[g][T - WINDOW:]

    def sample_mixer():
        ck_rep, cv_rep = _cache_rep4(ck_ref), _cache_rep4(cv_ref)

        def windows(k_rep, v_rep, g, ci):
            cat = lambda cache, new: jnp.concatenate(
                [cache[g][WINDOW * ci:WINDOW * (ci + 1)], new[g][CHUNK * ci:CHUNK * (ci + 1)]], axis=0)
            return cat(ck_rep, k_rep), cat(cv_rep, v_rep)

        attend = functools.partial(_attend_windows, windows, lambda ci: None, w.sinks)
        return _mixer(xs_ref[...], cs_ref[...], sas_ref[...], sbs_ref[...], attend, CHUNK, w_s)

    def sample_store(h1, k, v, vm, _):
        h1_ref[cur, 0:Ts, :] = h1
        _scatter_heads(ks_ref, k)
        _scatter_heads(vs_ref, v)
        _scatter_heads(vms_ref, vm)

    def phase(mixer, ffn_rows):
        gens = []
        if ffn_rows:
            gens.append(_ffn(h1_ref[prev] if ffn_rows == T else h1_ref[prev, 0:ffn_rows, :], w))
        if mixer:
            gens.append(prompt_mixer() if mixer == "prompt" else sample_mixer())
        res = _run(*gens)
        if ffn_rows:
            (yp_ref if ffn_rows == T else ys_ref)[...] = res[0]
        if mixer:
            (prompt_store if mixer == "prompt" else sample_store)(*res[-1])

    between = lambda lo, hi: jnp.logical_and(step > lo, step < hi)
    pl.when(step == 0)(functools.partial(phase, "prompt", 0))
    pl.when(between(0, n_p))(functools.partial(phase, "prompt", T))
    pl.when(step == n_p)(functools.partial(phase, "sample", T))
    pl.when(between(n_p, n_p + n_s))(functools.partial(phase, "sample", Ts))
    pl.when(step == n_p + n_s)(functools.partial(phase, None, Ts))


def _const_spec(shape):
    nd = len(shape)
    return pl.BlockSpec(shape, lambda *_: (0,) * nd)


def _weight_specs(ws):
    smem = pl.BlockSpec(memory_space=pltpu.SMEM)
    return [smem if name == "sinks" else _const_spec(a.shape) for name, a in zip(_WEIGHT_NAMES, ws)]


def _rope_tables(pos, reps=1):
    half = ROPE_DIM // 2
    inv = np.power(ROPE_THETA, -np.arange(half, dtype=np.float64) * (2.0 / ROPE_DIM))
    ang = np.asarray(pos, np.float64)[:, None] * inv[None, :]
    cos, sin = np.cos(ang), np.sin(ang)
    n = ang.shape[0]
    pad = HEAD_DIM - ROPE_DIM
    c = np.concatenate([cos, cos, np.ones((n, pad))], 1)
    sa = np.concatenate([-sin, np.zeros((n, half + pad))], 1)
    sb = np.concatenate([np.zeros((n, half)), sin, np.zeros((n, pad))], 1)
    return [jnp.asarray(np.tile(t, (reps, LANES // HEAD_DIM)), F32) for t in (c, sa, sb)]


def _gate_params(w_s, b_s, lb):
    i = jnp.arange(lb)
    mask = (i[None, :] // CHUNK) <= (i[:, None] // CHUNK)
    w = jnp.where(mask[None], w_s[:, :lb, :lb], 0.0)
    wpair = jnp.concatenate([w[0::2], w[1::2]], axis=2).astype(BF16)
    bias = jnp.repeat(jnp.transpose(b_s[:, :lb]), CMLP_GROUP_DIM, axis=1)
    return wpair, bias


def kernel(x_prompt, x_sample, cache_win_k, cache_win_v, w_in, ln_v_g, ln_v_b, attn_sinks, w_spatial, b_spatial,
           norm_attn_g, norm_cmlp_g, w_out, ln1_g, ln1_b, w_gate_up, w_down, ln2_g, ln2_b):
    B, S, _ = x_prompt.shape
    Bd, L, _ = x_sample.shape
    assert cache_win_k.shape[2] == WINDOW and L == CHUNK and w_in.shape[0] == 1

    row = lambda a: a[0].reshape(1, -1)
    wpair_p, bias_p = _gate_params(w_spatial[0], b_spatial[0], CMLP_BLOCK)
    wpair_s, bias_s = _gate_params(w_spatial[0], b_spatial[0], L)
    ws = _Weights(w_in=w_in[0].astype(BF16), lnv_g=row(ln_v_g), lnv_b=row(ln_v_b), sinks=attn_sinks[0],
                  wpair=wpair_p, bias=bias_p, nga=row(norm_attn_g), ngc=row(norm_cmlp_g),
                  w_out=w_out[0].astype(BF16), ln1g=row(ln1_g), ln1b=row(ln1_b),
                  wgu=w_gate_up[0].astype(BF16), wd=w_down[0].astype(BF16), ln2g=row(ln2_g), ln2b=row(ln2_b))

    T = PROMPT_TILE
    tps = S // T
    n_p = B * tps
    NS = SAMPLE_SEQS
    Ts = NS * L
    n_s = Bd // NS
    tables_p = _rope_tables(np.arange(S))
    tables_s = _rope_tables(PAST_LEN + np.arange(L), reps=NS)

    p_mix = lambda s: jnp.minimum(s, n_p - 1)
    s_mix = lambda s: jnp.clip(s - n_p, 0, n_s - 1)
    p_ffn = lambda s: jnp.clip(s - 1, 0, n_p - 1)
    s_ffn = lambda s: jnp.clip(s - 1 - n_p, 0, n_s - 1)
    blk = lambda rows, width, tile: pl.BlockSpec((rows, width), lambda s: (tile(s), 0))
    tab_p = pl.BlockSpec((T, LANES), lambda s: (p_mix(s) % tps, 0))
    cache_rows = NS * WINDOW * N_KV_HEADS
    win = blk(WINDOW * N_KV_HEADS, HEAD_DIM, lambda s: p_mix(s) // tps)
    y_p, kwin, vwin, y_s, k_s, v_s, vm_s = pl.pallas_call(
        functools.partial(_layer_body, T, Ts, tps, n_p, n_s),
        grid=(n_p + n_s + 1,),
        in_specs=[blk(T, D_MODEL, p_mix), tab_p, tab_p, tab_p,
                  blk(Ts, D_MODEL, s_mix)] + [_const_spec(t.shape) for t in tables_s]
        + [blk(cache_rows, HEAD_DIM, s_mix), blk(cache_rows, HEAD_DIM, s_mix)]
        + _weight_specs(ws) + [_const_spec(wpair_s.shape), _const_spec(bias_s.shape)],
        out_specs=[blk(T, D_MODEL, p_ffn), win, win,
                   blk(Ts, D_MODEL, s_ffn), blk(Ts * N_KV_HEADS, HEAD_DIM, s_mix),
                   blk(Ts * N_KV_HEADS, HEAD_DIM, s_mix), blk(Ts * CMLP_GROUPS, CMLP_GROUP_DIM, s_mix)],
        out_shape=[jax.ShapeDtypeStruct((B * S, D_MODEL), F32),
                   jax.ShapeDtypeStruct((B * WINDOW * N_KV_HEADS, HEAD_DIM), F32),
                   jax.ShapeDtypeStruct((B * WINDOW * N_KV_HEADS, HEAD_DIM), F32),
                   jax.ShapeDtypeStruct((Bd * L, D_MODEL), F32),
                   jax.ShapeDtypeStruct((Bd * L * N_KV_HEADS, HEAD_DIM), F32),
                   jax.ShapeDtypeStruct((Bd * L * N_KV_HEADS, HEAD_DIM), F32),
                   jax.ShapeDtypeStruct((Bd * L * CMLP_GROUPS, CMLP_GROUP_DIM), F32)],
        scratch_shapes=[pltpu.VMEM((2, N_KV_HEADS, WINDOW, 2 * LANES), BF16),
                        pltpu.VMEM((2, N_KV_HEADS, WINDOW, 2 * LANES), BF16),
                        pltpu.VMEM((2, T, D_MODEL), F32)],
        compiler_params=pltpu.CompilerParams(dimension_semantics=("arbitrary",), vmem_limit_bytes=VMEM_LIMIT),
        name="layer",
    )(x_prompt.reshape(B * S, D_MODEL), *tables_p, x_sample.reshape(Bd * L, D_MODEL), *tables_s,
      cache_win_k.reshape(Bd * WINDOW * N_KV_HEADS, HEAD_DIM),
      cache_win_v.reshape(Bd * WINDOW * N_KV_HEADS, HEAD_DIM), *ws, wpair_s, bias_s)

    return (y_p.reshape(B, S, D_MODEL), y_s.reshape(Bd, L, D_MODEL),
            kwin.reshape(1, B, WINDOW, N_KV_HEADS, HEAD_DIM),
            vwin.reshape(1, B, WINDOW, N_KV_HEADS, HEAD_DIM),
            k_s.reshape(1, Bd, L, N_KV_HEADS, HEAD_DIM),
            v_s.reshape(1, Bd, L, N_KV_HEADS, HEAD_DIM),
            vm_s.reshape(1, Bd, L, CMLP_GROUPS, CMLP_GROUP_DIM))
```

```python
import collections
import functools

import jax
import jax.numpy as jnp
import numpy as np
from jax import lax
from jax.experimental import pallas as pl
from jax.experimental.pallas import tpu as pltpu

D_MODEL = 1024
CHUNK = 64
HEAD_DIM = 64
D_ATTN = 512
D_CMLP = 512
N_HEADS = 8
N_KV_HEADS = 2
GQA_GROUP = 4
D_KV = 128
WINDOW = 128
ROPE_DIM = 16
ROPE_THETA = 500000.0
CMLP_BLOCK = 128
CMLP_GROUPS = 8
CMLP_GROUP_DIM = 64
D_IN = 1792
D_FF = 2816
PAST_LEN = 1024
ALPHA = 2.0 ** 0.25
LN_EPS = 1e-5
NEG_INF = -1e30
LANES = 128
FF_CHUNK = 256
VMEM_LIMIT = 56 * 1024 * 1024
CAST_STEPS = 8
PROMPT_TILE = 512
SAMPLE_SEQS = 4

F32 = jnp.float32
BF16 = jnp.bfloat16


def _layernorm(x, g, b):
    mu = jnp.mean(x, -1, keepdims=True)
    d = x - mu
    var = jnp.mean(d * d, -1, keepdims=True)
    return d * lax.rsqrt(var + LN_EPS) * g + b


def _rmsnorm(x, g):
    ms = jnp.mean(x * x, -1, keepdims=True)
    return x * lax.rsqrt(ms + LN_EPS) * g


def _dot(a, b):
    return jnp.dot(a, b, preferred_element_type=F32)


def _rope(x, c, sa, sb):
    out = []
    for i in range(x.shape[1] // LANES):
        s = x[:, LANES * i:LANES * (i + 1)]
        up = pltpu.roll(s, LANES - ROPE_DIM // 2, 1)
        dn = pltpu.roll(s, ROPE_DIM // 2, 1)
        out.append(s * c + up * sa + dn * sb)
    return out[0] if len(out) == 1 else jnp.concatenate(out, axis=1)


def _row_halves(n):
    return (slice(0, n // 2), slice(n // 2, n))


def _rep4(a):
    ar = pltpu.roll(a, HEAD_DIM, 1)
    first = lax.broadcasted_iota(jnp.int32, a.shape, 1) < HEAD_DIM
    g0 = jnp.where(first, a, ar).astype(BF16)
    g1 = jnp.where(first, ar, a).astype(BF16)
    return [jnp.concatenate([g0, g0], axis=1), jnp.concatenate([g1, g1], axis=1)]


def _attend_windows(windows, first_valid, sinks, q, k, v):
    T = q.shape[0]
    k_rep, v_rep = _rep4(k), _rep4(v)
    head_lane = lax.broadcasted_iota(jnp.int32, (CHUNK, GQA_GROUP * HEAD_DIM), 1) // HEAD_DIM
    key_idx = lax.broadcasted_iota(jnp.int32, (GQA_GROUP * CHUNK, WINDOW + CHUNK), 1)
    ao_rows = []
    for ci in range(T // CHUNK):
        ao_g = []
        for g in range(N_KV_HEADS):
            qc = q[CHUNK * ci:CHUNK * (ci + 1), 256 * g:256 * (g + 1)]
            qm = jnp.concatenate([jnp.where(head_lane == h, qc, 0.0) for h in range(GQA_GROUP)], axis=0).astype(BF16)
            kw, vw = windows(k_rep, v_rep, g, ci)
            s = lax.dot_general(qm, kw, (((1,), (1,)), ((), ())), preferred_element_type=F32)
            fv = first_valid(ci)
            if fv is not None:
                s = jnp.where(key_idx >= fv, s, NEG_INF)
            ps = []
            for h in range(GQA_GROUP):
                sh = s[CHUNK * h:CHUNK * (h + 1)]
                sink = sinks[GQA_GROUP * g + h]
                m = jnp.maximum(jnp.max(sh, -1, keepdims=True), sink)
                p = jnp.exp(sh - m)
                denom = jnp.sum(p, -1, keepdims=True) + jnp.exp(sink - m)
                ps.append(p * (1.0 / denom))
            pm = jnp.concatenate(ps, axis=0).astype(BF16)
            r = _dot(pm, vw)
            o = jnp.where(head_lane == 0, r[0:CHUNK], 0.0)
            for h in range(1, GQA_GROUP):
                o = o + jnp.where(head_lane == h, r[CHUNK * h:CHUNK * (h + 1)], 0.0)
            ao_g.append(o)
        ao_rows.append(jnp.concatenate(ao_g, axis=1))
        yield
    return jnp.concatenate(ao_rows, axis=0), (k_rep, v_rep)


def _mixer(x, c, sa, sb, attend, LB, w):
    T = x.shape[0]
    nb = T // LB
    xb = x.astype(BF16)
    uv = jax.nn.gelu(_dot(xb, w.w_in[:, D_ATTN + 2 * D_KV:D_IN]))
    yield
    q = _dot(xb, w.w_in[:, 0:D_ATTN])
    kv = _dot(xb, w.w_in[:, D_ATTN:D_ATTN + 2 * D_KV])

    k = _rope(kv[:, :D_KV], c, sa, sb)
    v = kv[:, D_KV:]
    q = _rope(q, c, sa, sb) * (HEAD_DIM ** -0.5)

    yield
    u = uv[:, :D_CMLP]
    vm = _layernorm(uv[:, D_CMLP:], w.lnv_g[...], w.lnv_b[...])

    lo_mask = lax.broadcasted_iota(jnp.int32, (T, LANES), 1) < CMLP_GROUP_DIM
    gate_slabs = []
    for p in range(D_CMLP // LANES):
        slab = vm[:, LANES * p:LANES * (p + 1)]
        lo = jnp.where(lo_mask, slab, 0.0).astype(BF16)
        hi = jnp.where(lo_mask, 0.0, slab).astype(BF16)
        rhs = jnp.concatenate(
            [jnp.concatenate([lo[LB * b:LB * (b + 1)], hi[LB * b:LB * (b + 1)]], axis=0) for b in range(nb)], axis=1)
        o = _dot(w.wpair[p], rhs)
        gate_slabs.append(jnp.concatenate([o[:, LANES * b:LANES * (b + 1)] for b in range(nb)], axis=0))
    s_gate = jnp.concatenate(gate_slabs, axis=1) + jnp.concatenate([w.bias[...]] * nb, axis=0)
    co = u * s_gate
    yield

    ao, carry = yield from attend(q, k, v)

    an = _rmsnorm(ao, w.nga[...]).astype(BF16)
    cn = _rmsnorm(co, w.ngc[...]).astype(BF16)
    h1 = []
    for r in _row_halves(T):
        mix = _dot(an[r], w.w_out[0:D_ATTN, :]) + _dot(cn[r], w.w_out[D_ATTN:, :])
        yield
        h1.append(_layernorm(ALPHA * x[r] + mix, w.ln1g[...], w.ln1b[...]))
    return jnp.concatenate(h1, axis=0), k, v, vm, carry


def _ffn(h1, w):
    hb = h1.astype(BF16)
    acts = []
    for lo_c in range(0, D_FF, FF_CHUNK):
        hi_c = min(lo_c + FF_CHUNK, D_FF)
        gt = _dot(hb, w.wgu[:, lo_c:hi_c])
        up = _dot(hb, w.wgu[:, D_FF + lo_c:D_FF + hi_c])
        acts.append((gt * jax.nn.sigmoid(gt) * up).astype(BF16))
        yield
    act = jnp.concatenate(acts, axis=1)
    y = []
    for r in _row_halves(h1.shape[0]):
        f = _dot(act[r], w.wd[...])
        yield
        y.append(_layernorm(ALPHA * h1[r] + f, w.ln2g[...], w.ln2b[...]))
    return jnp.concatenate(y, axis=0)


def _run(*gens):
    results = [None] * len(gens)
    live = list(range(len(gens)))
    while live:
        for i in list(live):
            try:
                next(gens[i])
            except StopIteration as done:
                results[i] = done.value
                live.remove(i)
    return results


_WEIGHT_NAMES = ("w_in", "lnv_g", "lnv_b", "sinks", "wpair", "bias", "nga", "ngc", "w_out", "ln1g", "ln1b",
                 "wgu", "wd", "ln2g", "ln2b")
_Weights = collections.namedtuple("_Weights", _WEIGHT_NAMES)
N_W = len(_WEIGHT_NAMES)


def _pipelined_step(step, n_tiles, h1_ref, w, make_mixer, store_y, store_mixer):
    cur = step % 2
    prev = 1 - cur

    def phase(do_mixer, do_ffn):
        gens = ([_ffn(h1_ref[prev], w)] if do_ffn else []) + ([make_mixer()] if do_mixer else [])
        res = _run(*gens)
        if do_ffn:
            store_y(res[0])
        if do_mixer:
            h1_ref[cur] = res[-1][0]
            store_mixer(*res[-1][1:])

    pl.when(step == 0)(functools.partial(phase, True, False))
    pl.when(jnp.logical_and(step > 0, step < n_tiles))(functools.partial(phase, True, True))
    pl.when(step == n_tiles)(functools.partial(phase, False, True))


def _prompt_body(T, tiles_per_seq, n_tiles, *refs):
    x_ref, c_ref, sa_ref, sb_ref = refs[:4]
    w = _Weights(*refs[4:4 + N_W])
    y_ref, k_out_ref, v_out_ref, hk_ref, hv_ref, h1_ref = refs[4 + N_W:]
    step = pl.program_id(0)
    j = step % tiles_per_seq
    rd = j % 2
    wr = 1 - rd

    @pl.when(j == 0)
    def _():
        hk_ref[0] = jnp.zeros(hk_ref.shape[1:], BF16)
        hv_ref[0] = jnp.zeros(hv_ref.shape[1:], BF16)

    def first_valid(ci):
        if ci >= WINDOW // CHUNK:
            return None
        return jnp.where(j == 0, (WINDOW // CHUNK - ci) * CHUNK, 0)

    def make_mixer():
        hist_k = [hk_ref[rd, g] for g in range(N_KV_HEADS)]
        hist_v = [hv_ref[rd, g] for g in range(N_KV_HEADS)]

        def windows(k_rep, v_rep, g, ci):
            k_all = jnp.concatenate([hist_k[g], k_rep[g]], axis=0)
            v_all = jnp.concatenate([hist_v[g], v_rep[g]], axis=0)
            lo = CHUNK * ci
            return k_all[lo:lo + WINDOW + CHUNK], v_all[lo:lo + WINDOW + CHUNK]

        attend = functools.partial(_attend_windows, windows, first_valid, w.sinks)
        return _mixer(x_ref[...], c_ref[...], sa_ref[...], sb_ref[...], attend, CMLP_BLOCK, w)

    def store_y(y):
        y_ref[...] = y

    def store_mixer(k, v, _, carry):
        k_out_ref[0] = k[T - WINDOW:, :]
        v_out_ref[0] = v[T - WINDOW:, :]
        k_rep, v_rep = carry
        for g in range(N_KV_HEADS):
            hk_ref[wr, g] = k_rep[g][T - WINDOW:]
            hv_ref[wr, g] = v_rep[g][T - WINDOW:]

    _pipelined_step(step, n_tiles, h1_ref, w, make_mixer, store_y, store_mixer)


def _scatter_heads(out_ref, a):
    n = a.shape[1] // HEAD_DIM
    for i in range(n):
        out_ref[pl.ds(i, a.shape[0], stride=n), :] = a[:, HEAD_DIM * i:HEAD_DIM * (i + 1)]


def _cache_rep4(ref):
    n = ref.shape[0] // N_KV_HEADS
    out = []
    for g in range(N_KV_HEADS):
        a = ref[pl.ds(g, n, stride=N_KV_HEADS), :]
        a2 = jnp.concatenate([a, a], axis=1).astype(BF16)
        out.append(jnp.concatenate([a2, a2], axis=1))
    return out


def _sample_body(n_tiles, *refs):
    x_ref, c_ref, sa_ref, sb_ref, ck_ref, cv_ref = refs[:6]
    w = _Weights(*refs[6:6 + N_W])
    y_ref, k_out_ref, v_out_ref, vm_out_ref, h1_ref = refs[6 + N_W:]

    def make_mixer():
        ck_rep, cv_rep = _cache_rep4(ck_ref), _cache_rep4(cv_ref)

        def windows(k_rep, v_rep, g, ci):
            cat = lambda cache, new: jnp.concatenate(
                [cache[g][WINDOW * ci:WINDOW * (ci + 1)], new[g][CHUNK * ci:CHUNK * (ci + 1)]], axis=0)
            return cat(ck_rep, k_rep), cat(cv_rep, v_rep)

        attend = functools.partial(_attend_windows, windows, lambda ci: None, w.sinks)
        return _mixer(x_ref[...], c_ref[...], sa_ref[...], sb_ref[...], attend, CHUNK, w)

    def store_y(y):
        y_ref[...] = y

    def store_mixer(k, v, vm, *_):
        _scatter_heads(k_out_ref, k)
        _scatter_heads(v_out_ref, v)
        _scatter_heads(vm_out_ref, vm)

    _pipelined_step(pl.program_id(0), n_tiles, h1_ref, w, make_mixer, store_y, store_mixer)


def _cast_body(*refs):
    n = len(refs) // 2
    for src, dst in zip(refs[:n], refs[n:]):
        dst[...] = src[...].astype(BF16)


def _to_bf16(*ws):
    specs = [pl.BlockSpec((a.shape[0] // CAST_STEPS, a.shape[1]), lambda i: (i, 0)) for a in ws]
    return pl.pallas_call(
        _cast_body,
        grid=(CAST_STEPS,),
        in_specs=specs,
        out_specs=specs,
        out_shape=[jax.ShapeDtypeStruct(a.shape, BF16) for a in ws],
        compiler_params=pltpu.CompilerParams(dimension_semantics=("arbitrary",), vmem_limit_bytes=VMEM_LIMIT),
        name="cast_weights",
    )(*ws)


def _const_spec(shape):
    nd = len(shape)
    return pl.BlockSpec(shape, lambda *_: (0,) * nd)


def _weight_specs(ws):
    smem = pl.BlockSpec(memory_space=pltpu.SMEM)
    return [smem if name == "sinks" else _const_spec(a.shape) for name, a in zip(_WEIGHT_NAMES, ws)]


def _rope_tables(pos, reps=1):
    half = ROPE_DIM // 2
    inv = np.power(ROPE_THETA, -np.arange(half, dtype=np.float64) * (2.0 / ROPE_DIM))
    ang = np.asarray(pos, np.float64)[:, None] * inv[None, :]
    cos, sin = np.cos(ang), np.sin(ang)
    n = ang.shape[0]
    pad = HEAD_DIM - ROPE_DIM
    c = np.concatenate([cos, cos, np.ones((n, pad))], 1)
    sa = np.concatenate([-sin, np.zeros((n, half + pad))], 1)
    sb = np.concatenate([np.zeros((n, half)), sin, np.zeros((n, pad))], 1)
    return [jnp.asarray(np.tile(t, (reps, LANES // HEAD_DIM)), F32) for t in (c, sa, sb)]


def _gate_params(w_s, b_s, lb):
    i = jnp.arange(lb)
    mask = (i[None, :] // CHUNK) <= (i[:, None] // CHUNK)
    w = jnp.where(mask[None], w_s[:, :lb, :lb], 0.0)
    wpair = jnp.concatenate([w[0::2], w[1::2]], axis=2).astype(BF16)
    bias = jnp.repeat(jnp.transpose(b_s[:, :lb]), CMLP_GROUP_DIM, axis=1)
    return wpair, bias


def kernel(x_prompt, x_sample, cache_win_k, cache_win_v, w_in, ln_v_g, ln_v_b, attn_sinks, w_spatial, b_spatial,
           norm_attn_g, norm_cmlp_g, w_out, ln1_g, ln1_b, w_gate_up, w_down, ln2_g, ln2_b):
    B, S, _ = x_prompt.shape
    Bd, L, _ = x_sample.shape
    assert cache_win_k.shape[2] == WINDOW and L == CHUNK and w_in.shape[0] == 1

    row = lambda a: a[0].reshape(1, -1)
    w_in_b, w_out_b, wgu_b, wd_b = _to_bf16(w_in[0], w_out[0], w_gate_up[0], w_down[0])

    def weights(lb):
        wpair, bias = _gate_params(w_spatial[0], b_spatial[0], lb)
        return _Weights(w_in=w_in_b, lnv_g=row(ln_v_g), lnv_b=row(ln_v_b), sinks=attn_sinks[0],
                        wpair=wpair, bias=bias, nga=row(norm_attn_g), ngc=row(norm_cmlp_g),
                        w_out=w_out_b, ln1g=row(ln1_g), ln1b=row(ln1_b),
                        wgu=wgu_b, wd=wd_b, ln2g=row(ln2_g), ln2b=row(ln2_b))

    T = PROMPT_TILE
    tps = S // T
    n_tiles = B * tps
    wp = weights(CMLP_BLOCK)
    tables = _rope_tables(np.arange(S))
    mix_tile = lambda s: jnp.minimum(s, n_tiles - 1)
    ffn_tile = lambda s: jnp.maximum(s - 1, 0)
    tab = pl.BlockSpec((T, LANES), lambda s: (mix_tile(s) % tps, 0))
    win = pl.BlockSpec((1, WINDOW, D_KV), lambda s: (mix_tile(s) // tps, 0, 0))
    y_p, kwin, vwin = pl.pallas_call(
        functools.partial(_prompt_body, T, tps, n_tiles),
        grid=(n_tiles + 1,),
        in_specs=[pl.BlockSpec((T, D_MODEL), lambda s: (mix_tile(s), 0)), tab, tab, tab] + _weight_specs(wp),
        out_specs=[pl.BlockSpec((T, D_MODEL), lambda s: (ffn_tile(s), 0)), win, win],
        out_shape=[jax.ShapeDtypeStruct((B * S, D_MODEL), F32),
                   jax.ShapeDtypeStruct((B, WINDOW, D_KV), F32),
                   jax.ShapeDtypeStruct((B, WINDOW, D_KV), F32)],
        scratch_shapes=[pltpu.VMEM((2, N_KV_HEADS, WINDOW, 2 * LANES), BF16),
                        pltpu.VMEM((2, N_KV_HEADS, WINDOW, 2 * LANES), BF16),
                        pltpu.VMEM((2, T, D_MODEL), F32)],
        compiler_params=pltpu.CompilerParams(dimension_semantics=("arbitrary",), vmem_limit_bytes=VMEM_LIMIT),
        name="layer_prompt",
    )(x_prompt.reshape(B * S, D_MODEL), *tables, *wp)
    y_p = y_p.reshape(B, S, D_MODEL)

    NS = SAMPLE_SEQS
    Ts = NS * L
    ws = weights(L)
    tables_s = _rope_tables(PAST_LEN + np.arange(L), reps=NS)
    n_st = Bd // NS
    mix_rows = lambda n, width: pl.BlockSpec((n, width), lambda s: (jnp.minimum(s, n_st - 1), 0))
    cache_rows = NS * WINDOW * N_KV_HEADS
    y_s, k_s, v_s, vm_s = pl.pallas_call(
        functools.partial(_sample_body, n_st),
        grid=(n_st + 1,),
        in_specs=[mix_rows(Ts, D_MODEL)] + [_const_spec(t.shape) for t in tables_s]
        + [mix_rows(cache_rows, HEAD_DIM), mix_rows(cache_rows, HEAD_DIM)] + _weight_specs(ws),
        out_specs=[pl.BlockSpec((Ts, D_MODEL), lambda s: (jnp.maximum(s - 1, 0), 0)),
                   mix_rows(Ts * N_KV_HEADS, HEAD_DIM), mix_rows(Ts * N_KV_HEADS, HEAD_DIM),
                   mix_rows(Ts * CMLP_GROUPS, CMLP_GROUP_DIM)],
        out_shape=[jax.ShapeDtypeStruct((Bd * L, D_MODEL), F32),
                   jax.ShapeDtypeStruct((Bd * L * N_KV_HEADS, HEAD_DIM), F32),
                   jax.ShapeDtypeStruct((Bd * L * N_KV_HEADS, HEAD_DIM), F32),
                   jax.ShapeDtypeStruct((Bd * L * CMLP_GROUPS, CMLP_GROUP_DIM), F32)],
        scratch_shapes=[pltpu.VMEM((2, Ts, D_MODEL), F32)],
        compiler_params=pltpu.CompilerParams(dimension_semantics=("arbitrary",), vmem_limit_bytes=VMEM_LIMIT),
        name="layer_sample",
    )(x_sample.reshape(Bd * L, D_MODEL), *tables_s,
      cache_win_k.reshape(Bd * WINDOW * N_KV_HEADS, HEAD_DIM),
      cache_win_v.reshape(Bd * WINDOW * N_KV_HEADS, HEAD_DIM), *ws)

    return (y_p, y_s.reshape(Bd, L, D_MODEL),
            kwin.reshape(1, B, WINDOW, N_KV_HEADS, HEAD_DIM),
            vwin.reshape(1, B, WINDOW, N_KV_HEADS, HEAD_DIM),
            k_s.reshape(1, Bd, L, N_KV_HEADS, HEAD_DIM),
            v_s.reshape(1, Bd, L, N_KV_HEADS, HEAD_DIM),
            vm_s.reshape(1, Bd, L, CMLP_GROUPS, CMLP_GROUP_DIM))
```

```python
import collections
import functools

import jax
import jax.numpy as jnp
import numpy as np
from jax import lax
from jax.experimental import pallas as pl
from jax.experimental.pallas import tpu as pltpu

D_MODEL = 1024
CHUNK = 64
HEAD_DIM = 64
D_ATTN = 512
D_CMLP = 512
N_HEADS = 8
N_KV_HEADS = 2
GQA_GROUP = 4
D_KV = 128
WINDOW = 128
ROPE_DIM = 16
ROPE_THETA = 500000.0
CMLP_BLOCK = 128
CMLP_GROUPS = 8
CMLP_GROUP_DIM = 64
D_IN = 1792
D_FF = 2816
PAST_LEN = 1024
ALPHA = 2.0 ** 0.25
LN_EPS = 1e-5
NEG_INF = -1e30
LANES = 128
FF_CHUNK = 256
VMEM_LIMIT = 56 * 1024 * 1024
CAST_STEPS = 4
PROMPT_TILE = 512
SAMPLE_SEQS = 4

F32 = jnp.float32
BF16 = jnp.bfloat16


def _layernorm(x, g, b):
    mu = jnp.mean(x, -1, keepdims=True)
    d = x - mu
    var = jnp.mean(d * d, -1, keepdims=True)
    return d * lax.rsqrt(var + LN_EPS) * g + b


def _rmsnorm(x, g):
    ms = jnp.mean(x * x, -1, keepdims=True)
    return x * lax.rsqrt(ms + LN_EPS) * g


def _dot(a, b):
    return jnp.dot(a, b, preferred_element_type=F32)


def _rope(x, c, sa, sb):
    out = []
    for i in range(x.shape[1] // LANES):
        s = x[:, LANES * i:LANES * (i + 1)]
        up = pltpu.roll(s, LANES - ROPE_DIM // 2, 1)
        dn = pltpu.roll(s, ROPE_DIM // 2, 1)
        out.append(s * c + up * sa + dn * sb)
    return out[0] if len(out) == 1 else jnp.concatenate(out, axis=1)


def _row_halves(n):
    return (slice(0, n // 2), slice(n // 2, n))


def _rep4(a):
    ar = pltpu.roll(a, HEAD_DIM, 1)
    first = lax.broadcasted_iota(jnp.int32, a.shape, 1) < HEAD_DIM
    g0 = jnp.where(first, a, ar).astype(BF16)
    g1 = jnp.where(first, ar, a).astype(BF16)
    return [jnp.concatenate([g0, g0], axis=1), jnp.concatenate([g1, g1], axis=1)]


def _attend_windows(windows, first_valid, sinks, q, k, v):
    T = q.shape[0]
    k_rep, v_rep = _rep4(k), _rep4(v)
    head_lane = lax.broadcasted_iota(jnp.int32, (CHUNK, GQA_GROUP * HEAD_DIM), 1) // HEAD_DIM
    key_idx = lax.broadcasted_iota(jnp.int32, (GQA_GROUP * CHUNK, WINDOW + CHUNK), 1)
    ao_rows = []
    for ci in range(T // CHUNK):
        ao_g = []
        for g in range(N_KV_HEADS):
            qc = q[CHUNK * ci:CHUNK * (ci + 1), 256 * g:256 * (g + 1)]
            qm = jnp.concatenate([jnp.where(head_lane == h, qc, 0.0) for h in range(GQA_GROUP)], axis=0).astype(BF16)
            kw, vw = windows(k_rep, v_rep, g, ci)
            s = lax.dot_general(qm, kw, (((1,), (1,)), ((), ())), preferred_element_type=F32)
            fv = first_valid(ci)
            if fv is not None:
                s = jnp.where(key_idx >= fv, s, NEG_INF)
            ps = []
            for h in range(GQA_GROUP):
                sh = s[CHUNK * h:CHUNK * (h + 1)]
                sink = sinks[GQA_GROUP * g + h]
                m = jnp.maximum(jnp.max(sh, -1, keepdims=True), sink)
                p = jnp.exp(sh - m)
                denom = jnp.sum(p, -1, keepdims=True) + jnp.exp(sink - m)
                ps.append(p * (1.0 / denom))
            pm = jnp.concatenate(ps, axis=0).astype(BF16)
            r = _dot(pm, vw)
            o = jnp.where(head_lane == 0, r[0:CHUNK], 0.0)
            for h in range(1, GQA_GROUP):
                o = o + jnp.where(head_lane == h, r[CHUNK * h:CHUNK * (h + 1)], 0.0)
            ao_g.append(o)
        ao_rows.append(jnp.concatenate(ao_g, axis=1))
        yield
    return jnp.concatenate(ao_rows, axis=0), (k_rep, v_rep)


def _mixer(x, c, sa, sb, attend, LB, w):
    T = x.shape[0]
    nb = T // LB
    xb = x.astype(BF16)
    uv = jax.nn.gelu(_dot(xb, w.w_in[:, D_ATTN + 2 * D_KV:D_IN]))
    yield
    q = _dot(xb, w.w_in[:, 0:D_ATTN])
    kv = _dot(xb, w.w_in[:, D_ATTN:D_ATTN + 2 * D_KV])

    k = _rope(kv[:, :D_KV], c, sa, sb)
    v = kv[:, D_KV:]
    q = _rope(q, c, sa, sb) * (HEAD_DIM ** -0.5)

    yield
    u = uv[:, :D_CMLP]
    vm = _layernorm(uv[:, D_CMLP:], w.lnv_g[...], w.lnv_b[...])

    lo_mask = lax.broadcasted_iota(jnp.int32, (T, LANES), 1) < CMLP_GROUP_DIM
    gate_slabs = []
    for p in range(D_CMLP // LANES):
        slab = vm[:, LANES * p:LANES * (p + 1)]
        lo = jnp.where(lo_mask, slab, 0.0).astype(BF16)
        hi = jnp.where(lo_mask, 0.0, slab).astype(BF16)
        rhs = jnp.concatenate(
            [jnp.concatenate([lo[LB * b:LB * (b + 1)], hi[LB * b:LB * (b + 1)]], axis=0) for b in range(nb)], axis=1)
        o = _dot(w.wpair[p], rhs)
        gate_slabs.append(jnp.concatenate([o[:, LANES * b:LANES * (b + 1)] for b in range(nb)], axis=0))
    s_gate = jnp.concatenate(gate_slabs, axis=1) + jnp.concatenate([w.bias[...]] * nb, axis=0)
    co = u * s_gate
    yield

    ao, carry = yield from attend(q, k, v)

    an = _rmsnorm(ao, w.nga[...]).astype(BF16)
    cn = _rmsnorm(co, w.ngc[...]).astype(BF16)
    h1 = []
    for r in _row_halves(T):
        mix = _dot(an[r], w.w_out[0:D_ATTN, :]) + _dot(cn[r], w.w_out[D_ATTN:, :])
        yield
        h1.append(_layernorm(ALPHA * x[r] + mix, w.ln1g[...], w.ln1b[...]))
    return jnp.concatenate(h1, axis=0), k, v, vm, carry


def _ffn(h1, w):
    hb = h1.astype(BF16)
    acts = []
    for lo_c in range(0, D_FF, FF_CHUNK):
        hi_c = min(lo_c + FF_CHUNK, D_FF)
        gt = _dot(hb, w.wgu[:, lo_c:hi_c])
        up = _dot(hb, w.wgu[:, D_FF + lo_c:D_FF + hi_c])
        acts.append((gt * jax.nn.sigmoid(gt) * up).astype(BF16))
        yield
    act = jnp.concatenate(acts, axis=1)
    y = []
    for r in _row_halves(h1.shape[0]):
        f = _dot(act[r], w.wd[...])
        yield
        y.append(_layernorm(ALPHA * h1[r] + f, w.ln2g[...], w.ln2b[...]))
    return jnp.concatenate(y, axis=0)


def _run(*gens):
    results = [None] * len(gens)
    live = list(range(len(gens)))
    while live:
        for i in list(live):
            try:
                next(gens[i])
            except StopIteration as done:
                results[i] = done.value
                live.remove(i)
    return results


_WEIGHT_NAMES = ("w_in", "lnv_g", "lnv_b", "sinks", "wpair", "bias", "nga", "ngc", "w_out", "ln1g", "ln1b",
                 "wgu", "wd", "ln2g", "ln2b")
_Weights = collections.namedtuple("_Weights", _WEIGHT_NAMES)
N_W = len(_WEIGHT_NAMES)


def _pipelined_step(step, n_tiles, h1_ref, w, make_mixer, store_y, store_mixer):
    cur = step % 2
    prev = 1 - cur

    def phase(do_mixer, do_ffn):
        gens = ([_ffn(h1_ref[prev], w)] if do_ffn else []) + ([make_mixer()] if do_mixer else [])
        res = _run(*gens)
        if do_ffn:
            store_y(res[0])
        if do_mixer:
            h1_ref[cur] = res[-1][0]
            store_mixer(*res[-1][1:])

    pl.when(step == 0)(functools.partial(phase, True, False))
    pl.when(jnp.logical_and(step > 0, step < n_tiles))(functools.partial(phase, True, True))
    pl.when(step == n_tiles)(functools.partial(phase, False, True))


def _prompt_body(T, tiles_per_seq, n_tiles, *refs):
    x_ref, c_ref, sa_ref, sb_ref = refs[:4]
    w = _Weights(*refs[4:4 + N_W])
    y_ref, k_out_ref, v_out_ref, hk_ref, hv_ref, h1_ref = refs[4 + N_W:]
    step = pl.program_id(0)
    j = step % tiles_per_seq
    rd = j % 2
    wr = 1 - rd

    @pl.when(j == 0)
    def _():
        hk_ref[0] = jnp.zeros(hk_ref.shape[1:], BF16)
        hv_ref[0] = jnp.zeros(hv_ref.shape[1:], BF16)

    def first_valid(ci):
        if ci >= WINDOW // CHUNK:
            return None
        return jnp.where(j == 0, (WINDOW // CHUNK - ci) * CHUNK, 0)

    def make_mixer():
        hist_k = [hk_ref[rd, g] for g in range(N_KV_HEADS)]
        hist_v = [hv_ref[rd, g] for g in range(N_KV_HEADS)]

        def windows(k_rep, v_rep, g, ci):
            k_all = jnp.concatenate([hist_k[g], k_rep[g]], axis=0)
            v_all = jnp.concatenate([hist_v[g], v_rep[g]], axis=0)
            lo = CHUNK * ci
            return k_all[lo:lo + WINDOW + CHUNK], v_all[lo:lo + WINDOW + CHUNK]

        attend = functools.partial(_attend_windows, windows, first_valid, w.sinks)
        return _mixer(x_ref[...], c_ref[...], sa_ref[...], sb_ref[...], attend, CMLP_BLOCK, w)

    def store_y(y):
        y_ref[...] = y

    def store_mixer(k, v, _, carry):
        k_out_ref[0] = k[T - WINDOW:, :]
        v_out_ref[0] = v[T - WINDOW:, :]
        k_rep, v_rep = carry
        for g in range(N_KV_HEADS):
            hk_ref[wr, g] = k_rep[g][T - WINDOW:]
            hv_ref[wr, g] = v_rep[g][T - WINDOW:]

    _pipelined_step(step, n_tiles, h1_ref, w, make_mixer, store_y, store_mixer)


def _scatter_heads(out_ref, a):
    n = a.shape[1] // HEAD_DIM
    for i in range(n):
        out_ref[pl.ds(i, a.shape[0], stride=n), :] = a[:, HEAD_DIM * i:HEAD_DIM * (i + 1)]


def _cache_rep4(ref):
    n = ref.shape[0] // N_KV_HEADS
    out = []
    for g in range(N_KV_HEADS):
        a = ref[pl.ds(g, n, stride=N_KV_HEADS), :]
        a2 = jnp.concatenate([a, a], axis=1).astype(BF16)
        out.append(jnp.concatenate([a2, a2], axis=1))
    return out


def _sample_body(n_tiles, *refs):
    x_ref, c_ref, sa_ref, sb_ref, ck_ref, cv_ref = refs[:6]
    w = _Weights(*refs[6:6 + N_W])
    y_ref, k_out_ref, v_out_ref, vm_out_ref, h1_ref = refs[6 + N_W:]

    def make_mixer():
        ck_rep, cv_rep = _cache_rep4(ck_ref), _cache_rep4(cv_ref)

        def windows(k_rep, v_rep, g, ci):
            cat = lambda cache, new: jnp.concatenate(
                [cache[g][WINDOW * ci:WINDOW * (ci + 1)], new[g][CHUNK * ci:CHUNK * (ci + 1)]], axis=0)
            return cat(ck_rep, k_rep), cat(cv_rep, v_rep)

        attend = functools.partial(_attend_windows, windows, lambda ci: None, w.sinks)
        return _mixer(x_ref[...], c_ref[...], sa_ref[...], sb_ref[...], attend, CHUNK, w)

    def store_y(y):
        y_ref[...] = y

    def store_mixer(k, v, vm, *_):
        _scatter_heads(k_out_ref, k)
        _scatter_heads(v_out_ref, v)
        _scatter_heads(vm_out_ref, vm)

    _pipelined_step(pl.program_id(0), n_tiles, h1_ref, w, make_mixer, store_y, store_mixer)


def _cast_body(*refs):
    n = len(refs) // 2
    for src, dst in zip(refs[:n], refs[n:]):
        dst[...] = src[...].astype(BF16)


def _to_bf16(*ws):
    specs = [pl.BlockSpec((a.shape[0] // CAST_STEPS, a.shape[1]), lambda i: (i, 0)) for a in ws]
    return pl.pallas_call(
        _cast_body,
        grid=(CAST_STEPS,),
        in_specs=specs,
        out_specs=specs,
        out_shape=[jax.ShapeDtypeStruct(a.shape, BF16) for a in ws],
        compiler_params=pltpu.CompilerParams(dimension_semantics=("arbitrary",), vmem_limit_bytes=VMEM_LIMIT),
        name="cast_weights",
    )(*ws)


def _const_spec(shape):
    nd = len(shape)
    return pl.BlockSpec(shape, lambda *_: (0,) * nd)


def _weight_specs(ws):
    smem = pl.BlockSpec(memory_space=pltpu.SMEM)
    return [smem if name == "sinks" else _const_spec(a.shape) for name, a in zip(_WEIGHT_NAMES, ws)]


def _rope_tables(pos, reps=1):
    half = ROPE_DIM // 2
    inv = np.power(ROPE_THETA, -np.arange(half, dtype=np.float64) * (2.0 / ROPE_DIM))
    ang = np.asarray(pos, np.float64)[:, None] * inv[None, :]
    cos, sin = np.cos(ang), np.sin(ang)
    n = ang.shape[0]
    pad = HEAD_DIM - ROPE_DIM
    c = np.concatenate([cos, cos, np.ones((n, pad))], 1)
    sa = np.concatenate([-sin, np.zeros((n, half + pad))], 1)
    sb = np.concatenate([np.zeros((n, half)), sin, np.zeros((n, pad))], 1)
    return [jnp.asarray(np.tile(t, (reps, LANES // HEAD_DIM)), F32) for t in (c, sa, sb)]


def _gate_params(w_s, b_s, lb):
    i = jnp.arange(lb)
    mask = (i[None, :] // CHUNK) <= (i[:, None] // CHUNK)
    w = jnp.where(mask[None], w_s[:, :lb, :lb], 0.0)
    wpair = jnp.concatenate([w[0::2], w[1::2]], axis=2).astype(BF16)
    bias = jnp.repeat(jnp.transpose(b_s[:, :lb]), CMLP_GROUP_DIM, axis=1)
    return wpair, bias


def kernel(x_prompt, x_sample, cache_win_k, cache_win_v, w_in, ln_v_g, ln_v_b, attn_sinks, w_spatial, b_spatial,
           norm_attn_g, norm_cmlp_g, w_out, ln1_g, ln1_b, w_gate_up, w_down, ln2_g, ln2_b):
    B, S, _ = x_prompt.shape
    Bd, L, _ = x_sample.shape
    assert cache_win_k.shape[2] == WINDOW and L == CHUNK and w_in.shape[0] == 1

    row = lambda a: a[0].reshape(1, -1)
    w_in_b, w_out_b, wgu_b, wd_b = _to_bf16(w_in[0], w_out[0], w_gate_up[0], w_down[0])

    def weights(lb):
        wpair, bias = _gate_params(w_spatial[0], b_spatial[0], lb)
        return _Weights(w_in=w_in_b, lnv_g=row(ln_v_g), lnv_b=row(ln_v_b), sinks=attn_sinks[0],
                        wpair=wpair, bias=bias, nga=row(norm_attn_g), ngc=row(norm_cmlp_g),
                        w_out=w_out_b, ln1g=row(ln1_g), ln1b=row(ln1_b),
                        wgu=wgu_b, wd=wd_b, ln2g=row(ln2_g), ln2b=row(ln2_b))

    T = PROMPT_TILE
    tps = S // T
    n_tiles = B * tps
    wp = weights(CMLP_BLOCK)
    tables = _rope_tables(np.arange(S))
    mix_tile = lambda s: jnp.minimum(s, n_tiles - 1)
    ffn_tile = lambda s: jnp.maximum(s - 1, 0)
    tab = pl.BlockSpec((T, LANES), lambda s: (mix_tile(s) % tps, 0))
    win = pl.BlockSpec((1, WINDOW, D_KV), lambda s: (mix_tile(s) // tps, 0, 0))
    y_p, kwin, vwin = pl.pallas_call(
        functools.partial(_prompt_body, T, tps, n_tiles),
        grid=(n_tiles + 1,),
        in_specs=[pl.BlockSpec((T, D_MODEL), lambda s: (mix_tile(s), 0)), tab, tab, tab] + _weight_specs(wp),
        out_specs=[pl.BlockSpec((T, D_MODEL), lambda s: (ffn_tile(s), 0)), win, win],
        out_shape=[jax.ShapeDtypeStruct((B * S, D_MODEL), F32),
                   jax.ShapeDtypeStruct((B, WINDOW, D_KV), F32),
                   jax.ShapeDtypeStruct((B, WINDOW, D_KV), F32)],
        scratch_shapes=[pltpu.VMEM((2, N_KV_HEADS, WINDOW, 2 * LANES), BF16),
                        pltpu.VMEM((2, N_KV_HEADS, WINDOW, 2 * LANES), BF16),
                        pltpu.VMEM((2, T, D_MODEL), F32)],
        compiler_params=pltpu.CompilerParams(dimension_semantics=("arbitrary",), vmem_limit_bytes=VMEM_LIMIT),
        name="layer_prompt",
    )(x_prompt.reshape(B * S, D_MODEL), *tables, *wp)
    y_p = y_p.reshape(B, S, D_MODEL)

    NS = SAMPLE_SEQS
    Ts = NS * L
    ws = weights(L)
    tables_s = _rope_tables(PAST_LEN + np.arange(L), reps=NS)
    n_st = Bd // NS
    mix_rows = lambda n, width: pl.BlockSpec((n, width), lambda s: (jnp.minimum(s, n_st - 1), 0))
    cache_rows = NS * WINDOW * N_KV_HEADS
    y_s, k_s, v_s, vm_s = pl.pallas_call(
        functools.partial(_sample_body, n_st),
        grid=(n_st + 1,),
        in_specs=[mix_rows(Ts, D_MODEL)] + [_const_spec(t.shape) for t in tables_s]
        + [mix_rows(cache_rows, HEAD_DIM), mix_rows(cache_rows, HEAD_DIM)] + _weight_specs(ws),
        out_specs=[pl.BlockSpec((Ts, D_MODEL), lambda s: (jnp.maximum(s - 1, 0), 0)),
                   mix_rows(Ts * N_KV_HEADS, HEAD_DIM), mix_rows(Ts * N_KV_HEADS, HEAD_DIM),
                   mix_rows(Ts * CMLP_GROUPS, CMLP_GROUP_DIM)],
        out_shape=[jax.ShapeDtypeStruct((Bd * L, D_MODEL), F32),
                   jax.ShapeDtypeStruct((Bd * L * N_KV_HEADS, HEAD_DIM), F32),
                   jax.ShapeDtypeStruct((Bd * L * N_KV_HEADS, HEAD_DIM), F32),
                   jax.ShapeDtypeStruct((Bd * L * CMLP_GROUPS, CMLP_GROUP_DIM), F32)],
        scratch_shapes=[pltpu.VMEM((2, Ts, D_MODEL), F32)],
        compiler_params=pltpu.CompilerParams(dimension_semantics=("arbitrary",), vmem_limit_bytes=VMEM_LIMIT),
        name="layer_sample",
    )(x_sample.reshape(Bd * L, D_MODEL), *tables_s,
      cache_win_k.reshape(Bd * WINDOW * N_KV_HEADS, HEAD_DIM),
      cache_win_v.reshape(Bd * WINDOW * N_KV_HEADS, HEAD_DIM), *ws)

    return (y_p, y_s.reshape(Bd, L, D_MODEL),
            kwin.reshape(1, B, WINDOW, N_KV_HEADS, HEAD_DIM),
            vwin.reshape(1, B, WINDOW, N_KV_HEADS, HEAD_DIM),
            k_s.reshape(1, Bd, L, N_KV_HEADS, HEAD_DIM),
            v_s.reshape(1, Bd, L, N_KV_HEADS, HEAD_DIM),
            vm_s.reshape(1, Bd, L, CMLP_GROUPS, CMLP_GROUP_DIM))
```

```python
import collections
import functools

import jax
import jax.numpy as jnp
import numpy as np
from jax import lax
from jax.experimental import pallas as pl
from jax.experimental.pallas import tpu as pltpu

D_MODEL = 1024
CHUNK = 64
HEAD_DIM = 64
D_ATTN = 512
D_CMLP = 512
N_HEADS = 8
N_KV_HEADS = 2
GQA_GROUP = 4
D_KV = 128
WINDOW = 128
ROPE_DIM = 16
ROPE_THETA = 500000.0
CMLP_BLOCK = 128
CMLP_GROUPS = 8
CMLP_GROUP_DIM = 64
D_IN = 1792
D_FF = 2816
PAST_LEN = 1024
ALPHA = 2.0 ** 0.25
LN_EPS = 1e-5
NEG_INF = -1e30
LANES = 128
FF_CHUNK = 256
VMEM_LIMIT = 56 * 1024 * 1024
CAST_STEPS = 4
PROMPT_TILE = 512
SAMPLE_SEQS = 4
WGU_ROWS, WD_ROWS = 32, 128
W_SLOTS = 4
W_CHUNKS_PER_PHASE = 4

F32 = jnp.float32
BF16 = jnp.bfloat16


def _layernorm(x, g, b):
    mu = jnp.mean(x, -1, keepdims=True)
    d = x - mu
    var = jnp.mean(d * d, -1, keepdims=True)
    return d * lax.rsqrt(var + LN_EPS) * g + b


def _rmsnorm(x, g):
    ms = jnp.mean(x * x, -1, keepdims=True)
    return x * lax.rsqrt(ms + LN_EPS) * g


def _dot(a, b):
    return jnp.dot(a, b, preferred_element_type=F32)


def _rope(x, c, sa, sb):
    out = []
    for i in range(x.shape[1] // LANES):
        s = x[:, LANES * i:LANES * (i + 1)]
        up = pltpu.roll(s, LANES - ROPE_DIM // 2, 1)
        dn = pltpu.roll(s, ROPE_DIM // 2, 1)
        out.append(s * c + up * sa + dn * sb)
    return out[0] if len(out) == 1 else jnp.concatenate(out, axis=1)


def _row_halves(n):
    return (slice(0, n // 2), slice(n // 2, n))


def _rep4(a):
    ar = pltpu.roll(a, HEAD_DIM, 1)
    first = lax.broadcasted_iota(jnp.int32, a.shape, 1) < HEAD_DIM
    g0 = jnp.where(first, a, ar).astype(BF16)
    g1 = jnp.where(first, ar, a).astype(BF16)
    return [jnp.concatenate([g0, g0], axis=1), jnp.concatenate([g1, g1], axis=1)]


def _attend_windows(windows, first_valid, sinks, q, k, v):
    T = q.shape[0]
    k_rep, v_rep = _rep4(k), _rep4(v)
    head_lane = lax.broadcasted_iota(jnp.int32, (CHUNK, GQA_GROUP * HEAD_DIM), 1) // HEAD_DIM
    key_idx = lax.broadcasted_iota(jnp.int32, (GQA_GROUP * CHUNK, WINDOW + CHUNK), 1)
    ao_rows = []
    for ci in range(T // CHUNK):
        ao_g = []
        for g in range(N_KV_HEADS):
            qc = q[CHUNK * ci:CHUNK * (ci + 1), 256 * g:256 * (g + 1)]
            qm = jnp.concatenate([jnp.where(head_lane == h, qc, 0.0) for h in range(GQA_GROUP)], axis=0).astype(BF16)
            kw, vw = windows(k_rep, v_rep, g, ci)
            s = lax.dot_general(qm, kw, (((1,), (1,)), ((), ())), preferred_element_type=F32)
            fv = first_valid(ci)
            if fv is not None:
                s = jnp.where(key_idx >= fv, s, NEG_INF)
            ps = []
            for h in range(GQA_GROUP):
                sh = s[CHUNK * h:CHUNK * (h + 1)]
                sink = sinks[GQA_GROUP * g + h]
                m = jnp.maximum(jnp.max(sh, -1, keepdims=True), sink)
                p = jnp.exp(sh - m)
                denom = jnp.sum(p, -1, keepdims=True) + jnp.exp(sink - m)
                ps.append(p * (1.0 / denom))
            pm = jnp.concatenate(ps, axis=0).astype(BF16)
            r = _dot(pm, vw)
            o = jnp.where(head_lane == 0, r[0:CHUNK], 0.0)
            for h in range(1, GQA_GROUP):
                o = o + jnp.where(head_lane == h, r[CHUNK * h:CHUNK * (h + 1)], 0.0)
            ao_g.append(o)
        ao_rows.append(jnp.concatenate(ao_g, axis=1))
        yield
    return jnp.concatenate(ao_rows, axis=0), (k_rep, v_rep)


def _mixer(x, c, sa, sb, attend, LB, w):
    T = x.shape[0]
    nb = T // LB
    xb = x.astype(BF16)
    uv = jax.nn.gelu(_dot(xb, w.w_in[:, D_ATTN + 2 * D_KV:D_IN]))
    yield
    q = _dot(xb, w.w_in[:, 0:D_ATTN])
    kv = _dot(xb, w.w_in[:, D_ATTN:D_ATTN + 2 * D_KV])

    k = _rope(kv[:, :D_KV], c, sa, sb)
    v = kv[:, D_KV:]
    q = _rope(q, c, sa, sb) * (HEAD_DIM ** -0.5)

    yield
    u = uv[:, :D_CMLP]
    vm = _layernorm(uv[:, D_CMLP:], w.lnv_g[...], w.lnv_b[...])

    lo_mask = lax.broadcasted_iota(jnp.int32, (T, LANES), 1) < CMLP_GROUP_DIM
    gate_slabs = []
    for p in range(D_CMLP // LANES):
        slab = vm[:, LANES * p:LANES * (p + 1)]
        lo = jnp.where(lo_mask, slab, 0.0).astype(BF16)
        hi = jnp.where(lo_mask, 0.0, slab).astype(BF16)
        rhs = jnp.concatenate(
            [jnp.concatenate([lo[LB * b:LB * (b + 1)], hi[LB * b:LB * (b + 1)]], axis=0) for b in range(nb)], axis=1)
        o = _dot(w.wpair[p], rhs)
        gate_slabs.append(jnp.concatenate([o[:, LANES * b:LANES * (b + 1)] for b in range(nb)], axis=0))
    s_gate = jnp.concatenate(gate_slabs, axis=1) + jnp.concatenate([w.bias[...]] * nb, axis=0)
    co = u * s_gate
    yield

    ao, carry = yield from attend(q, k, v)

    an = _rmsnorm(ao, w.nga[...]).astype(BF16)
    cn = _rmsnorm(co, w.ngc[...]).astype(BF16)
    h1 = []
    for r in _row_halves(T):
        mix = _dot(an[r], w.w_out[0:D_ATTN, :]) + _dot(cn[r], w.w_out[D_ATTN:, :])
        yield
        h1.append(_layernorm(ALPHA * x[r] + mix, w.ln1g[...], w.ln1b[...]))
    return jnp.concatenate(h1, axis=0), k, v, vm, carry


def _ffn(h1, w):
    hb = h1.astype(BF16)
    acts = []
    for lo_c in range(0, D_FF, FF_CHUNK):
        hi_c = min(lo_c + FF_CHUNK, D_FF)
        gt = _dot(hb, w.wgu[:, lo_c:hi_c])
        up = _dot(hb, w.wgu[:, D_FF + lo_c:D_FF + hi_c])
        acts.append((gt * jax.nn.sigmoid(gt) * up).astype(BF16))
        yield
    act = jnp.concatenate(acts, axis=1)
    y = []
    for r in _row_halves(h1.shape[0]):
        f = _dot(act[r], w.wd[...])
        yield
        y.append(_layernorm(ALPHA * h1[r] + f, w.ln2g[...], w.ln2b[...]))
    return jnp.concatenate(y, axis=0)


def _run(*gens):
    results = [None] * len(gens)
    live = list(range(len(gens)))
    while live:
        for i in list(live):
            try:
                next(gens[i])
            except StopIteration as done:
                results[i] = done.value
                live.remove(i)
    return results


_WEIGHT_NAMES = ("w_in", "lnv_g", "lnv_b", "sinks", "wpair", "bias", "nga", "ngc", "w_out", "ln1g", "ln1b",
                 "wgu", "wd", "ln2g", "ln2b")
_Weights = collections.namedtuple("_Weights", _WEIGHT_NAMES)
N_W = len(_WEIGHT_NAMES)


def _pipelined_step(step, n_tiles, h1_ref, w, make_mixer, store_y, store_mixer, fill_extra=None):
    cur = step % 2
    prev = 1 - cur

    def phase(do_mixer, do_ffn, extra=None):
        gens = (([_ffn(h1_ref[prev], w)] if do_ffn else []) + ([extra()] if extra else [])
                + ([make_mixer()] if do_mixer else []))
        res = _run(*gens)
        if do_ffn:
            store_y(res[0])
        if do_mixer:
            h1_ref[cur] = res[-1][0]
            store_mixer(*res[-1][1:])

    pl.when(step == 0)(functools.partial(phase, True, False, fill_extra))
    pl.when(jnp.logical_and(step > 0, step < n_tiles))(functools.partial(phase, True, True))
    pl.when(step == n_tiles)(functools.partial(phase, False, True))


def _load_convert(jobs, then):
    plan = [(job, c) for job in jobs for c in range(job[0].shape[0] // job[4])]
    depth = W_SLOTS - 1

    def copy(i):
        (src, _, stage, sem, rows), c = plan[i]
        return pltpu.make_async_copy(src.at[pl.ds(c * rows, rows), :], stage.at[c % W_SLOTS], sem.at[c % W_SLOTS])

    for i in range(min(depth, len(plan))):
        copy(i).start(priority=i % 2)
    for i, ((_, dst, stage, _, rows), c) in enumerate(plan):
        copy(i).wait()
        dst[c * rows:(c + 1) * rows, :] = stage[c % W_SLOTS].astype(BF16)
        if i + depth < len(plan):
            copy(i + depth).start(priority=(i + depth) % 2)
        if (i + 1) % W_CHUNKS_PER_PHASE == 0:
            yield
    then()


def _prompt_body(T, tiles_per_seq, n_tiles, *refs):
    x_ref, c_ref, sa_ref, sb_ref = refs[:4]
    w = _Weights(*refs[4:4 + N_W])
    (y_ref, k_out_ref, v_out_ref, wgu_out, wd_out, hk_ref, hv_ref, h1_ref,
     wgu_v, wd_v, stage_gu, stage_d, sem_gu, sem_d, sem_out) = refs[4 + N_W:]
    step = pl.program_id(0)

    out_copies = (pltpu.make_async_copy(wgu_v, wgu_out, sem_out.at[0]),
                  pltpu.make_async_copy(wd_v, wd_out, sem_out.at[1]))

    def start_out_copies():
        for cp in out_copies:
            cp.start()

    def convert_weights(w_hbm=w):
        jobs = [(w_hbm.wgu, wgu_v, stage_gu, sem_gu, WGU_ROWS), (w_hbm.wd, wd_v, stage_d, sem_d, WD_ROWS)]
        return _load_convert(jobs, start_out_copies)

    @pl.when(step == n_tiles)
    def _():
        for cp in out_copies:
            cp.wait()

    w = w._replace(wgu=wgu_v, wd=wd_v)
    j = step % tiles_per_seq
    rd = j % 2
    wr = 1 - rd

    @pl.when(j == 0)
    def _():
        hk_ref[0] = jnp.zeros(hk_ref.shape[1:], BF16)
        hv_ref[0] = jnp.zeros(hv_ref.shape[1:], BF16)

    def first_valid(ci):
        if ci >= WINDOW // CHUNK:
            return None
        return jnp.where(j == 0, (WINDOW // CHUNK - ci) * CHUNK, 0)

    def make_mixer():
        hist_k = [hk_ref[rd, g] for g in range(N_KV_HEADS)]
        hist_v = [hv_ref[rd, g] for g in range(N_KV_HEADS)]

        def windows(k_rep, v_rep, g, ci):
            k_all = jnp.concatenate([hist_k[g], k_rep[g]], axis=0)
            v_all = jnp.concatenate([hist_v[g], v_rep[g]], axis=0)
            lo = CHUNK * ci
            return k_all[lo:lo + WINDOW + CHUNK], v_all[lo:lo + WINDOW + CHUNK]

        attend = functools.partial(_attend_windows, windows, first_valid, w.sinks)
        return _mixer(x_ref[...], c_ref[...], sa_ref[...], sb_ref[...], attend, CMLP_BLOCK, w)

    def store_y(y):
        y_ref[...] = y

    def store_mixer(k, v, _, carry):
        k_out_ref[0] = k[T - WINDOW:, :]
        v_out_ref[0] = v[T - WINDOW:, :]
        k_rep, v_rep = carry
        for g in range(N_KV_HEADS):
            hk_ref[wr, g] = k_rep[g][T - WINDOW:]
            hv_ref[wr, g] = v_rep[g][T - WINDOW:]

    _pipelined_step(step, n_tiles, h1_ref, w, make_mixer, store_y, store_mixer, fill_extra=convert_weights)


def _scatter_heads(out_ref, a):
    n = a.shape[1] // HEAD_DIM
    for i in range(n):
        out_ref[pl.ds(i, a.shape[0], stride=n), :] = a[:, HEAD_DIM * i:HEAD_DIM * (i + 1)]


def _cache_rep4(ref):
    n = ref.shape[0] // N_KV_HEADS
    out = []
    for g in range(N_KV_HEADS):
        a = ref[pl.ds(g, n, stride=N_KV_HEADS), :]
        a2 = jnp.concatenate([a, a], axis=1).astype(BF16)
        out.append(jnp.concatenate([a2, a2], axis=1))
    return out


def _sample_body(n_tiles, *refs):
    x_ref, c_ref, sa_ref, sb_ref, ck_ref, cv_ref = refs[:6]
    w = _Weights(*refs[6:6 + N_W])
    y_ref, k_out_ref, v_out_ref, vm_out_ref, h1_ref = refs[6 + N_W:]

    def make_mixer():
        ck_rep, cv_rep = _cache_rep4(ck_ref), _cache_rep4(cv_ref)

        def windows(k_rep, v_rep, g, ci):
            cat = lambda cache, new: jnp.concatenate(
                [cache[g][WINDOW * ci:WINDOW * (ci + 1)], new[g][CHUNK * ci:CHUNK * (ci + 1)]], axis=0)
            return cat(ck_rep, k_rep), cat(cv_rep, v_rep)

        attend = functools.partial(_attend_windows, windows, lambda ci: None, w.sinks)
        return _mixer(x_ref[...], c_ref[...], sa_ref[...], sb_ref[...], attend, CHUNK, w)

    def store_y(y):
        y_ref[...] = y

    def store_mixer(k, v, vm, *_):
        _scatter_heads(k_out_ref, k)
        _scatter_heads(v_out_ref, v)
        _scatter_heads(vm_out_ref, vm)

    _pipelined_step(pl.program_id(0), n_tiles, h1_ref, w, make_mixer, store_y, store_mixer)


def _cast_body(*refs):
    n = len(refs) // 2
    for src, dst in zip(refs[:n], refs[n:]):
        dst[...] = src[...].astype(BF16)


def _to_bf16(*ws):
    specs = [pl.BlockSpec((a.shape[0] // CAST_STEPS, a.shape[1]), lambda i: (i, 0)) for a in ws]
    return pl.pallas_call(
        _cast_body,
        grid=(CAST_STEPS,),
        in_specs=specs,
        out_specs=specs,
        out_shape=[jax.ShapeDtypeStruct(a.shape, BF16) for a in ws],
        compiler_params=pltpu.CompilerParams(dimension_semantics=("arbitrary",), vmem_limit_bytes=VMEM_LIMIT),
        name="cast_weights",
    )(*ws)


def _const_spec(shape):
    nd = len(shape)
    return pl.BlockSpec(shape, lambda *_: (0,) * nd)


def _weight_specs(ws, in_hbm=()):
    spec = lambda name, a: (pl.BlockSpec(memory_space=pltpu.SMEM) if name == "sinks" else
                            pl.BlockSpec(memory_space=pl.ANY) if name in in_hbm else _const_spec(a.shape))
    return [spec(name, a) for name, a in zip(_WEIGHT_NAMES, ws)]


def _rope_tables(pos, reps=1):
    half = ROPE_DIM // 2
    inv = np.power(ROPE_THETA, -np.arange(half, dtype=np.float64) * (2.0 / ROPE_DIM))
    ang = np.asarray(pos, np.float64)[:, None] * inv[None, :]
    cos, sin = np.cos(ang), np.sin(ang)
    n = ang.shape[0]
    pad = HEAD_DIM - ROPE_DIM
    c = np.concatenate([cos, cos, np.ones((n, pad))], 1)
    sa = np.concatenate([-sin, np.zeros((n, half + pad))], 1)
    sb = np.concatenate([np.zeros((n, half)), sin, np.zeros((n, pad))], 1)
    return [jnp.asarray(np.tile(t, (reps, LANES // HEAD_DIM)), F32) for t in (c, sa, sb)]


def _gate_params(w_s, b_s, lb):
    i = jnp.arange(lb)
    mask = (i[None, :] // CHUNK) <= (i[:, None] // CHUNK)
    w = jnp.where(mask[None], w_s[:, :lb, :lb], 0.0)
    wpair = jnp.concatenate([w[0::2], w[1::2]], axis=2).astype(BF16)
    bias = jnp.repeat(jnp.transpose(b_s[:, :lb]), CMLP_GROUP_DIM, axis=1)
    return wpair, bias


def kernel(x_prompt, x_sample, cache_win_k, cache_win_v, w_in, ln_v_g, ln_v_b, attn_sinks, w_spatial, b_spatial,
           norm_attn_g, norm_cmlp_g, w_out, ln1_g, ln1_b, w_gate_up, w_down, ln2_g, ln2_b):
    B, S, _ = x_prompt.shape
    Bd, L, _ = x_sample.shape
    assert cache_win_k.shape[2] == WINDOW and L == CHUNK and w_in.shape[0] == 1

    row = lambda a: a[0].reshape(1, -1)
    w_in_b, w_out_b = _to_bf16(w_in[0], w_out[0])

    def weights(lb, wgu, wd):
        wpair, bias = _gate_params(w_spatial[0], b_spatial[0], lb)
        return _Weights(w_in=w_in_b, lnv_g=row(ln_v_g), lnv_b=row(ln_v_b), sinks=attn_sinks[0],
                        wpair=wpair, bias=bias, nga=row(norm_attn_g), ngc=row(norm_cmlp_g),
                        w_out=w_out_b, ln1g=row(ln1_g), ln1b=row(ln1_b),
                        wgu=wgu, wd=wd, ln2g=row(ln2_g), ln2b=row(ln2_b))

    T = PROMPT_TILE
    tps = S // T
    n_tiles = B * tps
    wp = weights(CMLP_BLOCK, w_gate_up[0], w_down[0])
    tables = _rope_tables(np.arange(S))
    mix_tile = lambda s: jnp.minimum(s, n_tiles - 1)
    ffn_tile = lambda s: jnp.maximum(s - 1, 0)
    tab = pl.BlockSpec((T, LANES), lambda s: (mix_tile(s) % tps, 0))
    win = pl.BlockSpec((1, WINDOW, D_KV), lambda s: (mix_tile(s) // tps, 0, 0))
    hbm = pl.BlockSpec(memory_space=pl.ANY)
    y_p, kwin, vwin, wgu_b, wd_b = pl.pallas_call(
        functools.partial(_prompt_body, T, tps, n_tiles),
        grid=(n_tiles + 1,),
        in_specs=[pl.BlockSpec((T, D_MODEL), lambda s: (mix_tile(s), 0)), tab, tab, tab]
        + _weight_specs(wp, in_hbm=("wgu", "wd")),
        out_specs=[pl.BlockSpec((T, D_MODEL), lambda s: (ffn_tile(s), 0)), win, win, hbm, hbm],
        out_shape=[jax.ShapeDtypeStruct((B * S, D_MODEL), F32),
                   jax.ShapeDtypeStruct((B, WINDOW, D_KV), F32),
                   jax.ShapeDtypeStruct((B, WINDOW, D_KV), F32),
                   jax.ShapeDtypeStruct((D_MODEL, 2 * D_FF), BF16),
                   jax.ShapeDtypeStruct((D_FF, D_MODEL), BF16)],
        scratch_shapes=[pltpu.VMEM((2, N_KV_HEADS, WINDOW, 2 * LANES), BF16),
                        pltpu.VMEM((2, N_KV_HEADS, WINDOW, 2 * LANES), BF16),
                        pltpu.VMEM((2, T, D_MODEL), F32),
                        pltpu.VMEM((D_MODEL, 2 * D_FF), BF16),
                        pltpu.VMEM((D_FF, D_MODEL), BF16),
                        pltpu.VMEM((W_SLOTS, WGU_ROWS, 2 * D_FF), F32),
                        pltpu.VMEM((W_SLOTS, WD_ROWS, D_MODEL), F32),
                        pltpu.SemaphoreType.DMA((W_SLOTS,)),
                        pltpu.SemaphoreType.DMA((W_SLOTS,)),
                        pltpu.SemaphoreType.DMA((2,))],
        compiler_params=pltpu.CompilerParams(dimension_semantics=("arbitrary",), vmem_limit_bytes=VMEM_LIMIT),
        name="layer_prompt",
    )(x_prompt.reshape(B * S, D_MODEL), *tables, *wp)
    y_p = y_p.reshape(B, S, D_MODEL)

    NS = SAMPLE_SEQS
    Ts = NS * L
    ws = weights(L, wgu_b, wd_b)
    tables_s = _rope_tables(PAST_LEN + np.arange(L), reps=NS)
    n_st = Bd // NS
    mix_rows = lambda n, width: pl.BlockSpec((n, width), lambda s: (jnp.minimum(s, n_st - 1), 0))
    cache_rows = NS * WINDOW * N_KV_HEADS
    y_s, k_s, v_s, vm_s = pl.pallas_call(
        functools.partial(_sample_body, n_st),
        grid=(n_st + 1,),
        in_specs=[mix_rows(Ts, D_MODEL)] + [_const_spec(t.shape) for t in tables_s]
        + [mix_rows(cache_rows, HEAD_DIM), mix_rows(cache_rows, HEAD_DIM)] + _weight_specs(ws),
        out_specs=[pl.BlockSpec((Ts, D_MODEL), lambda s: (jnp.maximum(s - 1, 0), 0)),
                   mix_rows(Ts * N_KV_HEADS, HEAD_DIM), mix_rows(Ts * N_KV_HEADS, HEAD_DIM),
                   mix_rows(Ts * CMLP_GROUPS, CMLP_GROUP_DIM)],
        out_shape=[jax.ShapeDtypeStruct((Bd * L, D_MODEL), F32),
                   jax.ShapeDtypeStruct((Bd * L * N_KV_HEADS, HEAD_DIM), F32),
                   jax.ShapeDtypeStruct((Bd * L * N_KV_HEADS, HEAD_DIM), F32),
                   jax.ShapeDtypeStruct((Bd * L * CMLP_GROUPS, CMLP_GROUP_DIM), F32)],
        scratch_shapes=[pltpu.VMEM((2, Ts, D_MODEL), F32)],
        compiler_params=pltpu.CompilerParams(dimension_semantics=("arbitrary",), vmem_limit_bytes=VMEM_LIMIT),
        name="layer_sample",
    )(x_sample.reshape(Bd * L, D_MODEL), *tables_s,
      cache_win_k.reshape(Bd * WINDOW * N_KV_HEADS, HEAD_DIM),
      cache_win_v.reshape(Bd * WINDOW * N_KV_HEADS, HEAD_DIM), *ws)

    return (y_p, y_s.reshape(Bd, L, D_MODEL),
            kwin.reshape(1, B, WINDOW, N_KV_HEADS, HEAD_DIM),
            vwin.reshape(1, B, WINDOW, N_KV_HEADS, HEAD_DIM),
            k_s.reshape(1, Bd, L, N_KV_HEADS, HEAD_DIM),
            v_s.reshape(1, Bd, L, N_KV_HEADS, HEAD_DIM),
            vm_s.reshape(1, Bd, L, CMLP_GROUPS, CMLP_GROUP_DIM))
```

```python
import collections
import functools

import jax
import jax.numpy as jnp
import numpy as np
from jax import lax
from jax.experimental import pallas as pl
from jax.experimental.pallas import tpu as pltpu

D_MODEL = 1024
CHUNK = 64
HEAD_DIM = 64
D_ATTN = 512
D_CMLP = 512
N_HEADS = 8
N_KV_HEADS = 2
GQA_GROUP = 4
D_KV = 128
WINDOW = 128
ROPE_DIM = 16
ROPE_THETA = 500000.0
CMLP_BLOCK = 128
CMLP_GROUPS = 8
CMLP_GROUP_DIM = 64
D_IN = 1792
D_FF = 2816
PAST_LEN = 1024
ALPHA = 2.0 ** 0.25
LN_EPS = 1e-5
NEG_INF = -1e30
LANES = 128
FF_CHUNK = 256
VMEM_LIMIT = 56 * 1024 * 1024
CAST_STEPS = 4
PROMPT_TILE = 512
SAMPLE_SEQS = 4
WGU_ROWS, WD_ROWS = 16, 64
W_SLOTS = 8
W_CHUNKS_PER_PHASE = 8

F32 = jnp.float32
BF16 = jnp.bfloat16


def _layernorm(x, g, b):
    mu = jnp.mean(x, -1, keepdims=True)
    d = x - mu
    var = jnp.mean(d * d, -1, keepdims=True)
    return d * lax.rsqrt(var + LN_EPS) * g + b


def _rmsnorm(x, g):
    ms = jnp.mean(x * x, -1, keepdims=True)
    return x * lax.rsqrt(ms + LN_EPS) * g


def _dot(a, b):
    return jnp.dot(a, b, preferred_element_type=F32)


def _rope(x, c, sa, sb):
    out = []
    for i in range(x.shape[1] // LANES):
        s = x[:, LANES * i:LANES * (i + 1)]
        up = pltpu.roll(s, LANES - ROPE_DIM // 2, 1)
        dn = pltpu.roll(s, ROPE_DIM // 2, 1)
        out.append(s * c + up * sa + dn * sb)
    return out[0] if len(out) == 1 else jnp.concatenate(out, axis=1)


def _row_halves(n):
    return (slice(0, n // 2), slice(n // 2, n))


def _rep4(a):
    ar = pltpu.roll(a, HEAD_DIM, 1)
    first = lax.broadcasted_iota(jnp.int32, a.shape, 1) < HEAD_DIM
    g0 = jnp.where(first, a, ar).astype(BF16)
    g1 = jnp.where(first, ar, a).astype(BF16)
    return [jnp.concatenate([g0, g0], axis=1), jnp.concatenate([g1, g1], axis=1)]


def _attend_windows(windows, first_valid, sinks, q, k, v):
    T = q.shape[0]
    k_rep, v_rep = _rep4(k), _rep4(v)
    head_lane = lax.broadcasted_iota(jnp.int32, (CHUNK, GQA_GROUP * HEAD_DIM), 1) // HEAD_DIM
    key_idx = lax.broadcasted_iota(jnp.int32, (GQA_GROUP * CHUNK, WINDOW + CHUNK), 1)
    ao_rows = []
    for ci in range(T // CHUNK):
        ao_g = []
        for g in range(N_KV_HEADS):
            qc = q[CHUNK * ci:CHUNK * (ci + 1), 256 * g:256 * (g + 1)]
            qm = jnp.concatenate([jnp.where(head_lane == h, qc, 0.0) for h in range(GQA_GROUP)], axis=0).astype(BF16)
            kw, vw = windows(k_rep, v_rep, g, ci)
            s = lax.dot_general(qm, kw, (((1,), (1,)), ((), ())), preferred_element_type=F32)
            fv = first_valid(ci)
            if fv is not None:
                s = jnp.where(key_idx >= fv, s, NEG_INF)
            ps = []
            for h in range(GQA_GROUP):
                sh = s[CHUNK * h:CHUNK * (h + 1)]
                sink = sinks[GQA_GROUP * g + h]
                m = jnp.maximum(jnp.max(sh, -1, keepdims=True), sink)
                p = jnp.exp(sh - m)
                denom = jnp.sum(p, -1, keepdims=True) + jnp.exp(sink - m)
                ps.append(p * (1.0 / denom))
            pm = jnp.concatenate(ps, axis=0).astype(BF16)
            r = _dot(pm, vw)
            o = jnp.where(head_lane == 0, r[0:CHUNK], 0.0)
            for h in range(1, GQA_GROUP):
                o = o + jnp.where(head_lane == h, r[CHUNK * h:CHUNK * (h + 1)], 0.0)
            ao_g.append(o)
        ao_rows.append(jnp.concatenate(ao_g, axis=1))
        yield
    return jnp.concatenate(ao_rows, axis=0), (k_rep, v_rep)


def _mixer(x, c, sa, sb, attend, LB, w):
    T = x.shape[0]
    nb = T // LB
    xb = x.astype(BF16)
    uv = jax.nn.gelu(_dot(xb, w.w_in[:, D_ATTN + 2 * D_KV:D_IN]))
    yield
    q = _dot(xb, w.w_in[:, 0:D_ATTN])
    kv = _dot(xb, w.w_in[:, D_ATTN:D_ATTN + 2 * D_KV])

    k = _rope(kv[:, :D_KV], c, sa, sb)
    v = kv[:, D_KV:]
    q = _rope(q, c, sa, sb) * (HEAD_DIM ** -0.5)

    yield
    u = uv[:, :D_CMLP]
    vm = _layernorm(uv[:, D_CMLP:], w.lnv_g[...], w.lnv_b[...])

    lo_mask = lax.broadcasted_iota(jnp.int32, (T, LANES), 1) < CMLP_GROUP_DIM
    gate_slabs = []
    for p in range(D_CMLP // LANES):
        slab = vm[:, LANES * p:LANES * (p + 1)]
        lo = jnp.where(lo_mask, slab, 0.0).astype(BF16)
        hi = jnp.where(lo_mask, 0.0, slab).astype(BF16)
        rhs = jnp.concatenate(
            [jnp.concatenate([lo[LB * b:LB * (b + 1)], hi[LB * b:LB * (b + 1)]], axis=0) for b in range(nb)], axis=1)
        o = _dot(w.wpair[p], rhs)
        gate_slabs.append(jnp.concatenate([o[:, LANES * b:LANES * (b + 1)] for b in range(nb)], axis=0))
    s_gate = jnp.concatenate(gate_slabs, axis=1) + jnp.concatenate([w.bias[...]] * nb, axis=0)
    co = u * s_gate
    yield

    ao, carry = yield from attend(q, k, v)

    an = _rmsnorm(ao, w.nga[...]).astype(BF16)
    cn = _rmsnorm(co, w.ngc[...]).astype(BF16)
    h1 = []
    for r in _row_halves(T):
        mix = _dot(an[r], w.w_out[0:D_ATTN, :]) + _dot(cn[r], w.w_out[D_ATTN:, :])
        yield
        h1.append(_layernorm(ALPHA * x[r] + mix, w.ln1g[...], w.ln1b[...]))
    return jnp.concatenate(h1, axis=0), k, v, vm, carry


def _ffn(h1, w):
    hb = h1.astype(BF16)
    acts = []
    for lo_c in range(0, D_FF, FF_CHUNK):
        hi_c = min(lo_c + FF_CHUNK, D_FF)
        gt = _dot(hb, w.wgu[:, lo_c:hi_c])
        up = _dot(hb, w.wgu[:, D_FF + lo_c:D_FF + hi_c])
        acts.append((gt * jax.nn.sigmoid(gt) * up).astype(BF16))
        yield
    act = jnp.concatenate(acts, axis=1)
    y = []
    for r in _row_halves(h1.shape[0]):
        f = _dot(act[r], w.wd[...])
        yield
        y.append(_layernorm(ALPHA * h1[r] + f, w.ln2g[...], w.ln2b[...]))
    return jnp.concatenate(y, axis=0)


def _run(*gens):
    results = [None] * len(gens)
    live = list(range(len(gens)))
    while live:
        for i in list(live):
            try:
                next(gens[i])
            except StopIteration as done:
                results[i] = done.value
                live.remove(i)
    return results


_WEIGHT_NAMES = ("w_in", "lnv_g", "lnv_b", "sinks", "wpair", "bias", "nga", "ngc", "w_out", "ln1g", "ln1b",
                 "wgu", "wd", "ln2g", "ln2b")
_Weights = collections.namedtuple("_Weights", _WEIGHT_NAMES)
N_W = len(_WEIGHT_NAMES)


def _pipelined_step(step, n_tiles, h1_ref, w, make_mixer, store_y, store_mixer, fill_extra=None):
    cur = step % 2
    prev = 1 - cur

    def phase(do_mixer, do_ffn, extra=None):
        gens = (([_ffn(h1_ref[prev], w)] if do_ffn else []) + ([extra()] if extra else [])
                + ([make_mixer()] if do_mixer else []))
        res = _run(*gens)
        if do_ffn:
            store_y(res[0])
        if do_mixer:
            h1_ref[cur] = res[-1][0]
            store_mixer(*res[-1][1:])

    pl.when(step == 0)(functools.partial(phase, True, False, fill_extra))
    pl.when(jnp.logical_and(step > 0, step < n_tiles))(functools.partial(phase, True, True))
    pl.when(step == n_tiles)(functools.partial(phase, False, True))


def _load_convert(jobs, then):
    plan = [(job, c) for job in jobs for c in range(job[0].shape[0] // job[4])]
    depth = W_SLOTS - 1

    def copy(i):
        (src, _, stage, sem, rows), c = plan[i]
        return pltpu.make_async_copy(src.at[pl.ds(c * rows, rows), :], stage.at[c % W_SLOTS], sem.at[c % W_SLOTS])

    for i in range(min(depth, len(plan))):
        copy(i).start(priority=i % 2)
    for i, ((_, dst, stage, _, rows), c) in enumerate(plan):
        copy(i).wait()
        dst[c * rows:(c + 1) * rows, :] = stage[c % W_SLOTS].astype(BF16)
        if i + depth < len(plan):
            copy(i + depth).start(priority=(i + depth) % 2)
        if (i + 1) % W_CHUNKS_PER_PHASE == 0:
            yield
    then()


def _prompt_body(T, tiles_per_seq, n_tiles, *refs):
    x_ref, c_ref, sa_ref, sb_ref = refs[:4]
    w = _Weights(*refs[4:4 + N_W])
    (y_ref, k_out_ref, v_out_ref, wgu_out, wd_out, hk_ref, hv_ref, h1_ref,
     wgu_v, wd_v, stage_gu, stage_d, sem_gu, sem_d, sem_out) = refs[4 + N_W:]
    step = pl.program_id(0)

    out_copies = (pltpu.make_async_copy(wgu_v, wgu_out, sem_out.at[0]),
                  pltpu.make_async_copy(wd_v, wd_out, sem_out.at[1]))

    def start_out_copies():
        for cp in out_copies:
            cp.start()

    def convert_weights(w_hbm=w):
        jobs = [(w_hbm.wgu, wgu_v, stage_gu, sem_gu, WGU_ROWS), (w_hbm.wd, wd_v, stage_d, sem_d, WD_ROWS)]
        return _load_convert(jobs, start_out_copies)

    @pl.when(step == n_tiles)
    def _():
        for cp in out_copies:
            cp.wait()

    w = w._replace(wgu=wgu_v, wd=wd_v)
    j = step % tiles_per_seq
    rd = j % 2
    wr = 1 - rd

    @pl.when(j == 0)
    def _():
        hk_ref[0] = jnp.zeros(hk_ref.shape[1:], BF16)
        hv_ref[0] = jnp.zeros(hv_ref.shape[1:], BF16)

    def first_valid(ci):
        if ci >= WINDOW // CHUNK:
            return None
        return jnp.where(j == 0, (WINDOW // CHUNK - ci) * CHUNK, 0)

    def make_mixer():
        hist_k = [hk_ref[rd, g] for g in range(N_KV_HEADS)]
        hist_v = [hv_ref[rd, g] for g in range(N_KV_HEADS)]

        def windows(k_rep, v_rep, g, ci):
            k_all = jnp.concatenate([hist_k[g], k_rep[g]], axis=0)
            v_all = jnp.concatenate([hist_v[g], v_rep[g]], axis=0)
            lo = CHUNK * ci
            return k_all[lo:lo + WINDOW + CHUNK], v_all[lo:lo + WINDOW + CHUNK]

        attend = functools.partial(_attend_windows, windows, first_valid, w.sinks)
        return _mixer(x_ref[...], c_ref[...], sa_ref[...], sb_ref[...], attend, CMLP_BLOCK, w)

    def store_y(y):
        y_ref[...] = y

    def store_mixer(k, v, _, carry):
        k_out_ref[0] = k[T - WINDOW:, :]
        v_out_ref[0] = v[T - WINDOW:, :]
        k_rep, v_rep = carry
        for g in range(N_KV_HEADS):
            hk_ref[wr, g] = k_rep[g][T - WINDOW:]
            hv_ref[wr, g] = v_rep[g][T - WINDOW:]

    _pipelined_step(step, n_tiles, h1_ref, w, make_mixer, store_y, store_mixer, fill_extra=convert_weights)


def _scatter_heads(out_ref, a):
    n = a.shape[1] // HEAD_DIM
    for i in range(n):
        out_ref[pl.ds(i, a.shape[0], stride=n), :] = a[:, HEAD_DIM * i:HEAD_DIM * (i + 1)]


def _cache_rep4(ref):
    n = ref.shape[0] // N_KV_HEADS
    out = []
    for g in range(N_KV_HEADS):
        a = ref[pl.ds(g, n, stride=N_KV_HEADS), :]
        a2 = jnp.concatenate([a, a], axis=1).astype(BF16)
        out.append(jnp.concatenate([a2, a2], axis=1))
    return out


def _sample_body(n_tiles, *refs):
    x_ref, c_ref, sa_ref, sb_ref, ck_ref, cv_ref = refs[:6]
    w = _Weights(*refs[6:6 + N_W])
    y_ref, k_out_ref, v_out_ref, vm_out_ref, h1_ref = refs[6 + N_W:]

    def make_mixer():
        ck_rep, cv_rep = _cache_rep4(ck_ref), _cache_rep4(cv_ref)

        def windows(k_rep, v_rep, g, ci):
            cat = lambda cache, new: jnp.concatenate(
                [cache[g][WINDOW * ci:WINDOW * (ci + 1)], new[g][CHUNK * ci:CHUNK * (ci + 1)]], axis=0)
            return cat(ck_rep, k_rep), cat(cv_rep, v_rep)

        attend = functools.partial(_attend_windows, windows, lambda ci: None, w.sinks)
        return _mixer(x_ref[...], c_ref[...], sa_ref[...], sb_ref[...], attend, CHUNK, w)

    def store_y(y):
        y_ref[...] = y

    def store_mixer(k, v, vm, *_):
        _scatter_heads(k_out_ref, k)
        _scatter_heads(v_out_ref, v)
        _scatter_heads(vm_out_ref, vm)

    _pipelined_step(pl.program_id(0), n_tiles, h1_ref, w, make_mixer, store_y, store_mixer)


def _cast_body(*refs):
    n = len(refs) // 2
    for src, dst in zip(refs[:n], refs[n:]):
        dst[...] = src[...].astype(BF16)


def _to_bf16(*ws):
    specs = [pl.BlockSpec((a.shape[0] // CAST_STEPS, a.shape[1]), lambda i: (i, 0)) for a in ws]
    return pl.pallas_call(
        _cast_body,
        grid=(CAST_STEPS,),
        in_specs=specs,
        out_specs=specs,
        out_shape=[jax.ShapeDtypeStruct(a.shape, BF16) for a in ws],
        compiler_params=pltpu.CompilerParams(dimension_semantics=("arbitrary",), vmem_limit_bytes=VMEM_LIMIT),
        name="cast_weights",
    )(*ws)


def _const_spec(shape):
    nd = len(shape)
    return pl.BlockSpec(shape, lambda *_: (0,) * nd)


def _weight_specs(ws, in_hbm=()):
    spec = lambda name, a: (pl.BlockSpec(memory_space=pltpu.SMEM) if name == "sinks" else
                            pl.BlockSpec(memory_space=pl.ANY) if name in in_hbm else _const_spec(a.shape))
    return [spec(name, a) for name, a in zip(_WEIGHT_NAMES, ws)]


def _rope_tables(pos, reps=1):
    half = ROPE_DIM // 2
    inv = np.power(ROPE_THETA, -np.arange(half, dtype=np.float64) * (2.0 / ROPE_DIM))
    ang = np.asarray(pos, np.float64)[:, None] * inv[None, :]
    cos, sin = np.cos(ang), np.sin(ang)
    n = ang.shape[0]
    pad = HEAD_DIM - ROPE_DIM
    c = np.concatenate([cos, cos, np.ones((n, pad))], 1)
    sa = np.concatenate([-sin, np.zeros((n, half + pad))], 1)
    sb = np.concatenate([np.zeros((n, half)), sin, np.zeros((n, pad))], 1)
    return [jnp.asarray(np.tile(t, (reps, LANES // HEAD_DIM)), F32) for t in (c, sa, sb)]


def _gate_params(w_s, b_s, lb):
    i = jnp.arange(lb)
    mask = (i[None, :] // CHUNK) <= (i[:, None] // CHUNK)
    w = jnp.where(mask[None], w_s[:, :lb, :lb], 0.0)
    wpair = jnp.concatenate([w[0::2], w[1::2]], axis=2).astype(BF16)
    bias = jnp.repeat(jnp.transpose(b_s[:, :lb]), CMLP_GROUP_DIM, axis=1)
    return wpair, bias


def kernel(x_prompt, x_sample, cache_win_k, cache_win_v, w_in, ln_v_g, ln_v_b, attn_sinks, w_spatial, b_spatial,
           norm_attn_g, norm_cmlp_g, w_out, ln1_g, ln1_b, w_gate_up, w_down, ln2_g, ln2_b):
    B, S, _ = x_prompt.shape
    Bd, L, _ = x_sample.shape
    assert cache_win_k.shape[2] == WINDOW and L == CHUNK and w_in.shape[0] == 1

    row = lambda a: a[0].reshape(1, -1)
    w_in_b, w_out_b = _to_bf16(w_in[0], w_out[0])

    def weights(lb, wgu, wd):
        wpair, bias = _gate_params(w_spatial[0], b_spatial[0], lb)
        return _Weights(w_in=w_in_b, lnv_g=row(ln_v_g), lnv_b=row(ln_v_b), sinks=attn_sinks[0],
                        wpair=wpair, bias=bias, nga=row(norm_attn_g), ngc=row(norm_cmlp_g),
                        w_out=w_out_b, ln1g=row(ln1_g), ln1b=row(ln1_b),
                        wgu=wgu, wd=wd, ln2g=row(ln2_g), ln2b=row(ln2_b))

    T = PROMPT_TILE
    tps = S // T
    n_tiles = B * tps
    wp = weights(CMLP_BLOCK, w_gate_up[0], w_down[0])
    tables = _rope_tables(np.arange(S))
    mix_tile = lambda s: jnp.minimum(s, n_tiles - 1)
    ffn_tile = lambda s: jnp.maximum(s - 1, 0)
    tab = pl.BlockSpec((T, LANES), lambda s: (mix_tile(s) % tps, 0))
    win = pl.BlockSpec((1, WINDOW, D_KV), lambda s: (mix_tile(s) // tps, 0, 0))
    hbm = pl.BlockSpec(memory_space=pl.ANY)
    y_p, kwin, vwin, wgu_b, wd_b = pl.pallas_call(
        functools.partial(_prompt_body, T, tps, n_tiles),
        grid=(n_tiles + 1,),
        in_specs=[pl.BlockSpec((T, D_MODEL), lambda s: (mix_tile(s), 0)), tab, tab, tab]
        + _weight_specs(wp, in_hbm=("wgu", "wd")),
        out_specs=[pl.BlockSpec((T, D_MODEL), lambda s: (ffn_tile(s), 0)), win, win, hbm, hbm],
        out_shape=[jax.ShapeDtypeStruct((B * S, D_MODEL), F32),
                   jax.ShapeDtypeStruct((B, WINDOW, D_KV), F32),
                   jax.ShapeDtypeStruct((B, WINDOW, D_KV), F32),
                   jax.ShapeDtypeStruct((D_MODEL, 2 * D_FF), BF16),
                   jax.ShapeDtypeStruct((D_FF, D_MODEL), BF16)],
        scratch_shapes=[pltpu.VMEM((2, N_KV_HEADS, WINDOW, 2 * LANES), BF16),
                        pltpu.VMEM((2, N_KV_HEADS, WINDOW, 2 * LANES), BF16),
                        pltpu.VMEM((2, T, D_MODEL), F32),
                        pltpu.VMEM((D_MODEL, 2 * D_FF), BF16),
                        pltpu.VMEM((D_FF, D_MODEL), BF16),
                        pltpu.VMEM((W_SLOTS, WGU_ROWS, 2 * D_FF), F32),
                        pltpu.VMEM((W_SLOTS, WD_ROWS, D_MODEL), F32),
                        pltpu.SemaphoreType.DMA((W_SLOTS,)),
                        pltpu.SemaphoreType.DMA((W_SLOTS,)),
                        pltpu.SemaphoreType.DMA((2,))],
        compiler_params=pltpu.CompilerParams(dimension_semantics=("arbitrary",), vmem_limit_bytes=VMEM_LIMIT),
        name="layer_prompt",
    )(x_prompt.reshape(B * S, D_MODEL), *tables, *wp)
    y_p = y_p.reshape(B, S, D_MODEL)

    NS = SAMPLE_SEQS
    Ts = NS * L
    ws = weights(L, wgu_b, wd_b)
    tables_s = _rope_tables(PAST_LEN + np.arange(L), reps=NS)
    n_st = Bd // NS
    mix_rows = lambda n, width: pl.BlockSpec((n, width), lambda s: (jnp.minimum(s, n_st - 1), 0))
    cache_rows = NS * WINDOW * N_KV_HEADS
    y_s, k_s, v_s, vm_s = pl.pallas_call(
        functools.partial(_sample_body, n_st),
        grid=(n_st + 1,),
        in_specs=[mix_rows(Ts, D_MODEL)] + [_const_spec(t.shape) for t in tables_s]
        + [mix_rows(cache_rows, HEAD_DIM), mix_rows(cache_rows, HEAD_DIM)] + _weight_specs(ws),
        out_specs=[pl.BlockSpec((Ts, D_MODEL), lambda s: (jnp.maximum(s - 1, 0), 0)),
                   mix_rows(Ts * N_KV_HEADS, HEAD_DIM), mix_rows(Ts * N_KV_HEADS, HEAD_DIM),
                   mix_rows(Ts * CMLP_GROUPS, CMLP_GROUP_DIM)],
        out_shape=[jax.ShapeDtypeStruct((Bd * L, D_MODEL), F32),
                   jax.ShapeDtypeStruct((Bd * L * N_KV_HEADS, HEAD_DIM), F32),
                   jax.ShapeDtypeStruct((Bd * L * N_KV_HEADS, HEAD_DIM), F32),
                   jax.ShapeDtypeStruct((Bd * L * CMLP_GROUPS, CMLP_GROUP_DIM), F32)],
        scratch_shapes=[pltpu.VMEM((2, Ts, D_MODEL), F32)],
        compiler_params=pltpu.CompilerParams(dimension_semantics=("arbitrary",), vmem_limit_bytes=VMEM_LIMIT),
        name="layer_sample",
    )(x_sample.reshape(Bd * L, D_MODEL), *tables_s,
      cache_win_k.reshape(Bd * WINDOW * N_KV_HEADS, HEAD_DIM),
      cache_win_v.reshape(Bd * WINDOW * N_KV_HEADS, HEAD_DIM), *ws)

    return (y_p, y_s.reshape(Bd, L, D_MODEL),
            kwin.reshape(1, B, WINDOW, N_KV_HEADS, HEAD_DIM),
            vwin.reshape(1, B, WINDOW, N_KV_HEADS, HEAD_DIM),
            k_s.reshape(1, Bd, L, N_KV_HEADS, HEAD_DIM),
            v_s.reshape(1, Bd, L, N_KV_HEADS, HEAD_DIM),
            vm_s.reshape(1, Bd, L, CMLP_GROUPS, CMLP_GROUP_DIM))
```

```python
import collections
import functools

import jax
import jax.numpy as jnp
import numpy as np
from jax import lax
from jax.experimental import pallas as pl
from jax.experimental.pallas import tpu as pltpu

D_MODEL = 1024
CHUNK = 64
HEAD_DIM = 64
D_ATTN = 512
D_CMLP = 512
N_HEADS = 8
N_KV_HEADS = 2
GQA_GROUP = 4
D_KV = 128
WINDOW = 128
ROPE_DIM = 16
ROPE_THETA = 500000.0
CMLP_BLOCK = 128
CMLP_GROUPS = 8
CMLP_GROUP_DIM = 64
D_IN = 1792
D_FF = 2816
PAST_LEN = 1024
ALPHA = 2.0 ** 0.25
LN_EPS = 1e-5
NEG_INF = -1e30
LANES = 128
FF_CHUNK = 256
VMEM_LIMIT = 56 * 1024 * 1024
CAST_STEPS = 4
PROMPT_TILE = 512
SAMPLE_SEQS = 4
WGU_ROWS, WD_ROWS = 16, 64
W_SLOTS = 8
W_CHUNKS_PER_PHASE = 8

F32 = jnp.float32
BF16 = jnp.bfloat16


def _layernorm(x, g, b):
    mu = jnp.mean(x, -1, keepdims=True)
    d = x - mu
    var = jnp.mean(d * d, -1, keepdims=True)
    return d * lax.rsqrt(var + LN_EPS) * g + b


def _rmsnorm(x, g):
    ms = jnp.mean(x * x, -1, keepdims=True)
    return x * lax.rsqrt(ms + LN_EPS) * g


def _dot(a, b):
    return jnp.dot(a, b, preferred_element_type=F32)


def _rope(x, c, sa, sb):
    out = []
    for i in range(x.shape[1] // LANES):
        s = x[:, LANES * i:LANES * (i + 1)]
        up = pltpu.roll(s, LANES - ROPE_DIM // 2, 1)
        dn = pltpu.roll(s, ROPE_DIM // 2, 1)
        out.append(s * c + up * sa + dn * sb)
    return out[0] if len(out) == 1 else jnp.concatenate(out, axis=1)


def _row_halves(n):
    return (slice(0, n // 2), slice(n // 2, n))


def _rep4(a):
    ar = pltpu.roll(a, HEAD_DIM, 1)
    first = lax.broadcasted_iota(jnp.int32, a.shape, 1) < HEAD_DIM
    g0 = jnp.where(first, a, ar).astype(BF16)
    g1 = jnp.where(first, ar, a).astype(BF16)
    return [jnp.concatenate([g0, g0], axis=1), jnp.concatenate([g1, g1], axis=1)]


def _attend_windows(windows, first_valid, sinks, q, k, v):
    T = q.shape[0]
    k_rep, v_rep = _rep4(k), _rep4(v)
    head_lane = lax.broadcasted_iota(jnp.int32, (CHUNK, GQA_GROUP * HEAD_DIM), 1) // HEAD_DIM
    key_idx = lax.broadcasted_iota(jnp.int32, (GQA_GROUP * CHUNK, WINDOW + CHUNK), 1)
    ao_rows = []
    for ci in range(T // CHUNK):
        ao_g = []
        for g in range(N_KV_HEADS):
            qc = q[CHUNK * ci:CHUNK * (ci + 1), 256 * g:256 * (g + 1)]
            qm = jnp.concatenate([jnp.where(head_lane == h, qc, 0.0) for h in range(GQA_GROUP)], axis=0).astype(BF16)
            kw, vw = windows(k_rep, v_rep, g, ci)
            s = lax.dot_general(qm, kw, (((1,), (1,)), ((), ())), preferred_element_type=F32)
            fv = first_valid(ci)
            if fv is not None:
                s = jnp.where(key_idx >= fv, s, NEG_INF)
            ps = []
            for h in range(GQA_GROUP):
                sh = s[CHUNK * h:CHUNK * (h + 1)]
                sink = sinks[GQA_GROUP * g + h]
                m = jnp.maximum(jnp.max(sh, -1, keepdims=True), sink)
                p = jnp.exp(sh - m)
                denom = jnp.sum(p, -1, keepdims=True) + jnp.exp(sink - m)
                ps.append(p * (1.0 / denom))
            pm = jnp.concatenate(ps, axis=0).astype(BF16)
            r = _dot(pm, vw)
            o = jnp.where(head_lane == 0, r[0:CHUNK], 0.0)
            for h in range(1, GQA_GROUP):
                o = o + jnp.where(head_lane == h, r[CHUNK * h:CHUNK * (h + 1)], 0.0)
            ao_g.append(o)
        ao_rows.append(jnp.concatenate(ao_g, axis=1))
        yield
    return jnp.concatenate(ao_rows, axis=0), (k_rep, v_rep)


def _mixer(x, c, sa, sb, attend, LB, w):
    T = x.shape[0]
    nb = T // LB
    xb = x.astype(BF16)
    uv = jax.nn.gelu(_dot(xb, w.w_in[:, D_ATTN + 2 * D_KV:D_IN]))
    yield
    q = _dot(xb, w.w_in[:, 0:D_ATTN])
    kv = _dot(xb, w.w_in[:, D_ATTN:D_ATTN + 2 * D_KV])

    k = _rope(kv[:, :D_KV], c, sa, sb)
    v = kv[:, D_KV:]
    q = _rope(q, c, sa, sb) * (HEAD_DIM ** -0.5)

    yield
    u = uv[:, :D_CMLP]
    vm = _layernorm(uv[:, D_CMLP:], w.lnv_g[...], w.lnv_b[...])

    lo_mask = lax.broadcasted_iota(jnp.int32, (T, LANES), 1) < CMLP_GROUP_DIM
    pad = [jnp.zeros((CMLP_BLOCK - LB, LANES), BF16)] if LB < CMLP_BLOCK else []
    gate_slabs = []
    for p in range(D_CMLP // LANES):
        slab = vm[:, LANES * p:LANES * (p + 1)]
        lo = jnp.where(lo_mask, slab, 0.0).astype(BF16)
        hi = jnp.where(lo_mask, 0.0, slab).astype(BF16)
        rhs = jnp.concatenate(
            [jnp.concatenate([lo[LB * b:LB * (b + 1)]] + pad + [hi[LB * b:LB * (b + 1)]] + pad, axis=0)
             for b in range(nb)], axis=1)
        o = _dot(w.wpair[p, 0:LB, :], rhs)
        gate_slabs.append(jnp.concatenate([o[:, LANES * b:LANES * (b + 1)] for b in range(nb)], axis=0))
    s_gate = jnp.concatenate(gate_slabs, axis=1) + jnp.concatenate([w.bias[0:LB, :]] * nb, axis=0)
    co = u * s_gate
    yield

    ao, carry = yield from attend(q, k, v)

    an = _rmsnorm(ao, w.nga[...]).astype(BF16)
    cn = _rmsnorm(co, w.ngc[...]).astype(BF16)
    h1 = []
    for r in _row_halves(T):
        mix = _dot(an[r], w.w_out[0:D_ATTN, :]) + _dot(cn[r], w.w_out[D_ATTN:, :])
        yield
        h1.append(_layernorm(ALPHA * x[r] + mix, w.ln1g[...], w.ln1b[...]))
    return jnp.concatenate(h1, axis=0), k, v, vm, carry


def _ffn(h1, w):
    hb = h1.astype(BF16)
    acts = []
    for lo_c in range(0, D_FF, FF_CHUNK):
        hi_c = min(lo_c + FF_CHUNK, D_FF)
        gt = _dot(hb, w.wgu[:, lo_c:hi_c])
        up = _dot(hb, w.wgu[:, D_FF + lo_c:D_FF + hi_c])
        acts.append((gt * jax.nn.sigmoid(gt) * up).astype(BF16))
        yield
    act = jnp.concatenate(acts, axis=1)
    y = []
    for r in _row_halves(h1.shape[0]):
        f = _dot(act[r], w.wd[...])
        yield
        y.append(_layernorm(ALPHA * h1[r] + f, w.ln2g[...], w.ln2b[...]))
    return jnp.concatenate(y, axis=0)


def _run(*gens):
    results = [None] * len(gens)
    live = list(range(len(gens)))
    while live:
        for i in list(live):
            try:
                next(gens[i])
            except StopIteration as done:
                results[i] = done.value
                live.remove(i)
    return results


_WEIGHT_NAMES = ("w_in", "lnv_g", "lnv_b", "sinks", "wpair", "bias", "nga", "ngc", "w_out", "ln1g", "ln1b",
                 "wgu", "wd", "ln2g", "ln2b")
_Weights = collections.namedtuple("_Weights", _WEIGHT_NAMES)
N_W = len(_WEIGHT_NAMES)


def _pipelined_step(step, n_tiles, h1_ref, w, make_mixer, store_y, store_mixer, fill_extra=None):
    cur = step % 2
    prev = 1 - cur

    def phase(do_mixer, do_ffn, extra=None):
        gens = (([_ffn(h1_ref[prev], w)] if do_ffn else []) + ([extra()] if extra else [])
                + ([make_mixer()] if do_mixer else []))
        res = _run(*gens)
        if do_ffn:
            store_y(res[0])
        if do_mixer:
            h1_ref[cur] = res[-1][0]
            store_mixer(*res[-1][1:])

    pl.when(step == 0)(functools.partial(phase, True, False, fill_extra))
    pl.when(jnp.logical_and(step > 0, step < n_tiles))(functools.partial(phase, True, True))
    pl.when(step == n_tiles)(functools.partial(phase, False, True))


def _load_convert(jobs, then):
    plan = [(job, c) for job in jobs for c in range(job[0].shape[0] // job[4])]
    depth = W_SLOTS - 1

    def copy(i):
        (src, _, stage, sem, rows), c = plan[i]
        return pltpu.make_async_copy(src.at[pl.ds(c * rows, rows), :], stage.at[c % W_SLOTS], sem.at[c % W_SLOTS])

    for i in range(min(depth, len(plan))):
        copy(i).start(priority=i % 2)
    for i, ((_, dst, stage, _, rows), c) in enumerate(plan):
        copy(i).wait()
        dst[c * rows:(c + 1) * rows, :] = stage[c % W_SLOTS].astype(BF16)
        if i + depth < len(plan):
            copy(i + depth).start(priority=(i + depth) % 2)
        if (i + 1) % W_CHUNKS_PER_PHASE == 0:
            yield
    then()


def _prompt_body(T, tiles_per_seq, n_tiles, *refs):
    x_ref, c_ref, sa_ref, sb_ref = refs[:4]
    w = _Weights(*refs[4:4 + N_W])
    (y_ref, k_out_ref, v_out_ref, wgu_out, wd_out, hk_ref, hv_ref, h1_ref,
     wgu_v, wd_v, stage_gu, stage_d, sem_gu, sem_d, sem_out) = refs[4 + N_W:]
    step = pl.program_id(0)

    out_copies = (pltpu.make_async_copy(wgu_v, wgu_out, sem_out.at[0]),
                  pltpu.make_async_copy(wd_v, wd_out, sem_out.at[1]))

    def start_out_copies():
        for cp in out_copies:
            cp.start()

    def convert_weights(w_hbm=w):
        jobs = [(w_hbm.wgu, wgu_v, stage_gu, sem_gu, WGU_ROWS), (w_hbm.wd, wd_v, stage_d, sem_d, WD_ROWS)]
        return _load_convert(jobs, start_out_copies)

    @pl.when(step == n_tiles)
    def _():
        for cp in out_copies:
            cp.wait()

    w = w._replace(wgu=wgu_v, wd=wd_v)
    j = step % tiles_per_seq
    rd = j % 2
    wr = 1 - rd

    @pl.when(j == 0)
    def _():
        hk_ref[0] = jnp.zeros(hk_ref.shape[1:], BF16)
        hv_ref[0] = jnp.zeros(hv_ref.shape[1:], BF16)

    def first_valid(ci):
        if ci >= WINDOW // CHUNK:
            return None
        return jnp.where(j == 0, (WINDOW // CHUNK - ci) * CHUNK, 0)

    def make_mixer():
        hist_k = [hk_ref[rd, g] for g in range(N_KV_HEADS)]
        hist_v = [hv_ref[rd, g] for g in range(N_KV_HEADS)]

        def windows(k_rep, v_rep, g, ci):
            k_all = jnp.concatenate([hist_k[g], k_rep[g]], axis=0)
            v_all = jnp.concatenate([hist_v[g], v_rep[g]], axis=0)
            lo = CHUNK * ci
            return k_all[lo:lo + WINDOW + CHUNK], v_all[lo:lo + WINDOW + CHUNK]

        attend = functools.partial(_attend_windows, windows, first_valid, w.sinks)
        return _mixer(x_ref[...], c_ref[...], sa_ref[...], sb_ref[...], attend, CMLP_BLOCK, w)

    def store_y(y):
        y_ref[...] = y

    def store_mixer(k, v, _, carry):
        k_out_ref[0] = k[T - WINDOW:, :]
        v_out_ref[0] = v[T - WINDOW:, :]
        k_rep, v_rep = carry
        for g in range(N_KV_HEADS):
            hk_ref[wr, g] = k_rep[g][T - WINDOW:]
            hv_ref[wr, g] = v_rep[g][T - WINDOW:]

    _pipelined_step(step, n_tiles, h1_ref, w, make_mixer, store_y, store_mixer, fill_extra=convert_weights)


def _scatter_heads(out_ref, a):
    n = a.shape[1] // HEAD_DIM
    for i in range(n):
        out_ref[pl.ds(i, a.shape[0], stride=n), :] = a[:, HEAD_DIM * i:HEAD_DIM * (i + 1)]


def _cache_rep4(ref):
    n = ref.shape[0] // N_KV_HEADS
    out = []
    for g in range(N_KV_HEADS):
        a = ref[pl.ds(g, n, stride=N_KV_HEADS), :]
        a2 = jnp.concatenate([a, a], axis=1).astype(BF16)
        out.append(jnp.concatenate([a2, a2], axis=1))
    return out


def _sample_body(n_tiles, *refs):
    x_ref, c_ref, sa_ref, sb_ref, ck_ref, cv_ref = refs[:6]
    w = _Weights(*refs[6:6 + N_W])
    y_ref, k_out_ref, v_out_ref, vm_out_ref, h1_ref = refs[6 + N_W:]

    def make_mixer():
        ck_rep, cv_rep = _cache_rep4(ck_ref), _cache_rep4(cv_ref)

        def windows(k_rep, v_rep, g, ci):
            cat = lambda cache, new: jnp.concatenate(
                [cache[g][WINDOW * ci:WINDOW * (ci + 1)], new[g][CHUNK * ci:CHUNK * (ci + 1)]], axis=0)
            return cat(ck_rep, k_rep), cat(cv_rep, v_rep)

        attend = functools.partial(_attend_windows, windows, lambda ci: None, w.sinks)
        return _mixer(x_ref[...], c_ref[...], sa_ref[...], sb_ref[...], attend, CHUNK, w)

    def store_y(y):
        y_ref[...] = y

    def store_mixer(k, v, vm, *_):
        _scatter_heads(k_out_ref, k)
        _scatter_heads(v_out_ref, v)
        _scatter_heads(vm_out_ref, vm)

    _pipelined_step(pl.program_id(0), n_tiles, h1_ref, w, make_mixer, store_y, store_mixer)


def _cast_body(*refs):
    n = len(refs) // 2
    for src, dst in zip(refs[:n], refs[n:]):
        dst[...] = src[...].astype(BF16)


def _to_bf16(*ws):
    specs = [pl.BlockSpec((a.shape[0] // CAST_STEPS, a.shape[1]), lambda i: (i, 0)) for a in ws]
    return pl.pallas_call(
        _cast_body,
        grid=(CAST_STEPS,),
        in_specs=specs,
        out_specs=specs,
        out_shape=[jax.ShapeDtypeStruct(a.shape, BF16) for a in ws],
        compiler_params=pltpu.CompilerParams(dimension_semantics=("arbitrary",), vmem_limit_bytes=VMEM_LIMIT),
        name="cast_weights",
    )(*ws)


def _const_spec(shape):
    nd = len(shape)
    return pl.BlockSpec(shape, lambda *_: (0,) * nd)


def _weight_specs(ws, in_hbm=()):
    spec = lambda name, a: (pl.BlockSpec(memory_space=pltpu.SMEM) if name == "sinks" else
                            pl.BlockSpec(memory_space=pl.ANY) if name in in_hbm else _const_spec(a.shape))
    return [spec(name, a) for name, a in zip(_WEIGHT_NAMES, ws)]


def _rope_tables(pos, reps=1):
    half = ROPE_DIM // 2
    inv = np.power(ROPE_THETA, -np.arange(half, dtype=np.float64) * (2.0 / ROPE_DIM))
    ang = np.asarray(pos, np.float64)[:, None] * inv[None, :]
    cos, sin = np.cos(ang), np.sin(ang)
    n = ang.shape[0]
    pad = HEAD_DIM - ROPE_DIM
    c = np.concatenate([cos, cos, np.ones((n, pad))], 1)
    sa = np.concatenate([-sin, np.zeros((n, half + pad))], 1)
    sb = np.concatenate([np.zeros((n, half)), sin, np.zeros((n, pad))], 1)
    return [jnp.asarray(np.tile(t, (reps, LANES // HEAD_DIM)), F32) for t in (c, sa, sb)]


def _gate_params(w_s, b_s, lb):
    i = jnp.arange(lb)
    mask = (i[None, :] // CHUNK) <= (i[:, None] // CHUNK)
    w = jnp.where(mask[None], w_s[:, :lb, :lb], 0.0)
    wpair = jnp.concatenate([w[0::2], w[1::2]], axis=2).astype(BF16)
    bias = jnp.repeat(jnp.transpose(b_s[:, :lb]), CMLP_GROUP_DIM, axis=1)
    return wpair, bias


def kernel(x_prompt, x_sample, cache_win_k, cache_win_v, w_in, ln_v_g, ln_v_b, attn_sinks, w_spatial, b_spatial,
           norm_attn_g, norm_cmlp_g, w_out, ln1_g, ln1_b, w_gate_up, w_down, ln2_g, ln2_b):
    B, S, _ = x_prompt.shape
    Bd, L, _ = x_sample.shape
    assert cache_win_k.shape[2] == WINDOW and L == CHUNK and w_in.shape[0] == 1

    row = lambda a: a[0].reshape(1, -1)
    w_in_b, w_out_b = _to_bf16(w_in[0], w_out[0])

    wpair, bias = _gate_params(w_spatial[0], b_spatial[0], CMLP_BLOCK)

    def weights(wgu, wd):
        return _Weights(w_in=w_in_b, lnv_g=row(ln_v_g), lnv_b=row(ln_v_b), sinks=attn_sinks[0],
                        wpair=wpair, bias=bias, nga=row(norm_attn_g), ngc=row(norm_cmlp_g),
                        w_out=w_out_b, ln1g=row(ln1_g), ln1b=row(ln1_b),
                        wgu=wgu, wd=wd, ln2g=row(ln2_g), ln2b=row(ln2_b))

    T = PROMPT_TILE
    tps = S // T
    n_tiles = B * tps
    wp = weights(w_gate_up[0], w_down[0])
    tables = _rope_tables(np.arange(S))
    mix_tile = lambda s: jnp.minimum(s, n_tiles - 1)
    ffn_tile = lambda s: jnp.maximum(s - 1, 0)
    tab = pl.BlockSpec((T, LANES), lambda s: (mix_tile(s) % tps, 0))
    win = pl.BlockSpec((1, WINDOW, D_KV), lambda s: (mix_tile(s) // tps, 0, 0))
    hbm = pl.BlockSpec(memory_space=pl.ANY)
    y_p, kwin, vwin, wgu_b, wd_b = pl.pallas_call(
        functools.partial(_prompt_body, T, tps, n_tiles),
        grid=(n_tiles + 1,),
        in_specs=[pl.BlockSpec((T, D_MODEL), lambda s: (mix_tile(s), 0)), tab, tab, tab]
        + _weight_specs(wp, in_hbm=("wgu", "wd")),
        out_specs=[pl.BlockSpec((T, D_MODEL), lambda s: (ffn_tile(s), 0)), win, win, hbm, hbm],
        out_shape=[jax.ShapeDtypeStruct((B * S, D_MODEL), F32),
                   jax.ShapeDtypeStruct((B, WINDOW, D_KV), F32),
                   jax.ShapeDtypeStruct((B, WINDOW, D_KV), F32),
                   jax.ShapeDtypeStruct((D_MODEL, 2 * D_FF), BF16),
                   jax.ShapeDtypeStruct((D_FF, D_MODEL), BF16)],
        scratch_shapes=[pltpu.VMEM((2, N_KV_HEADS, WINDOW, 2 * LANES), BF16),
                        pltpu.VMEM((2, N_KV_HEADS, WINDOW, 2 * LANES), BF16),
                        pltpu.VMEM((2, T, D_MODEL), F32),
                        pltpu.VMEM((D_MODEL, 2 * D_FF), BF16),
                        pltpu.VMEM((D_FF, D_MODEL), BF16),
                        pltpu.VMEM((W_SLOTS, WGU_ROWS, 2 * D_FF), F32),
                        pltpu.VMEM((W_SLOTS, WD_ROWS, D_MODEL), F32),
                        pltpu.SemaphoreType.DMA((W_SLOTS,)),
                        pltpu.SemaphoreType.DMA((W_SLOTS,)),
                        pltpu.SemaphoreType.DMA((2,))],
        compiler_params=pltpu.CompilerParams(dimension_semantics=("arbitrary",), vmem_limit_bytes=VMEM_LIMIT),
        name="layer_prompt",
    )(x_prompt.reshape(B * S, D_MODEL), *tables, *wp)
    y_p = y_p.reshape(B, S, D_MODEL)

    NS = SAMPLE_SEQS
    Ts = NS * L
    ws = weights(wgu_b, wd_b)
    tables_s = _rope_tables(PAST_LEN + np.arange(L), reps=NS)
    n_st = Bd // NS
    mix_rows = lambda n, width: pl.BlockSpec((n, width), lambda s: (jnp.minimum(s, n_st - 1), 0))
    cache_rows = NS * WINDOW * N_KV_HEADS
    y_s, k_s, v_s, vm_s = pl.pallas_call(
        functools.partial(_sample_body, n_st),
        grid=(n_st + 1,),
        in_specs=[mix_rows(Ts, D_MODEL)] + [_const_spec(t.shape) for t in tables_s]
        + [mix_rows(cache_rows, HEAD_DIM), mix_rows(cache_rows, HEAD_DIM)] + _weight_specs(ws),
        out_specs=[pl.BlockSpec((Ts, D_MODEL), lambda s: (jnp.maximum(s - 1, 0), 0)),
                   mix_rows(Ts * N_KV_HEADS, HEAD_DIM), mix_rows(Ts * N_KV_HEADS, HEAD_DIM),
                   mix_rows(Ts * CMLP_GROUPS, CMLP_GROUP_DIM)],
        out_shape=[jax.ShapeDtypeStruct((Bd * L, D_MODEL), F32),
                   jax.ShapeDtypeStruct((Bd * L * N_KV_HEADS, HEAD_DIM), F32),
                   jax.ShapeDtypeStruct((Bd * L * N_KV_HEADS, HEAD_DIM), F32),
                   jax.ShapeDtypeStruct((Bd * L * CMLP_GROUPS, CMLP_GROUP_DIM), F32)],
        scratch_shapes=[pltpu.VMEM((2, Ts, D_MODEL), F32)],
        compiler_params=pltpu.CompilerParams(dimension_semantics=("arbitrary",), vmem_limit_bytes=VMEM_LIMIT),
        name="layer_sample",
    )(x_sample.reshape(Bd * L, D_MODEL), *tables_s,
      cache_win_k.reshape(Bd * WINDOW * N_KV_HEADS, HEAD_DIM),
      cache_win_v.reshape(Bd * WINDOW * N_KV_HEADS, HEAD_DIM), *ws)

    return (y_p, y_s.reshape(Bd, L, D_MODEL),
            kwin.reshape(1, B, WINDOW, N_KV_HEADS, HEAD_DIM),
            vwin.reshape(1, B, WINDOW, N_KV_HEADS, HEAD_DIM),
            k_s.reshape(1, Bd, L, N_KV_HEADS, HEAD_DIM),
            v_s.reshape(1, Bd, L, N_KV_HEADS, HEAD_DIM),
            vm_s.reshape(1, Bd, L, CMLP_GROUPS, CMLP_GROUP_DIM))
```

```python
import collections
import functools

import jax
import jax.numpy as jnp
import numpy as np
from jax import lax
from jax.experimental import pallas as pl
from jax.experimental.pallas import tpu as pltpu

D_MODEL = 1024
CHUNK = 64
HEAD_DIM = 64
D_ATTN = 512
D_CMLP = 512
N_HEADS = 8
N_KV_HEADS = 2
GQA_GROUP = 4
D_KV = 128
WINDOW = 128
ROPE_DIM = 16
ROPE_THETA = 500000.0
CMLP_BLOCK = 128
CMLP_GROUPS = 8
CMLP_GROUP_DIM = 64
D_IN = 1792
D_FF = 2816
PAST_LEN = 1024
ALPHA = 2.0 ** 0.25
LN_EPS = 1e-5
NEG_INF = -1e30
LANES = 128
FF_CHUNK = 256
VMEM_LIMIT = 56 * 1024 * 1024
CAST_STEPS = 4
PROMPT_TILE = 512
SAMPLE_SEQS = 4
WGU_ROWS, WD_ROWS = 16, 64
W_SLOTS = 8
W_CHUNKS_PER_PHASE = 8

F32 = jnp.float32
BF16 = jnp.bfloat16


def _layernorm(x, g, b):
    mu = jnp.mean(x, -1, keepdims=True)
    d = x - mu
    var = jnp.mean(d * d, -1, keepdims=True)
    return d * lax.rsqrt(var + LN_EPS) * g + b


def _rmsnorm(x, g):
    ms = jnp.mean(x * x, -1, keepdims=True)
    return x * lax.rsqrt(ms + LN_EPS) * g


def _dot(a, b):
    return jnp.dot(a, b, preferred_element_type=F32)


def _rope(x, c, sa, sb):
    out = []
    for i in range(x.shape[1] // LANES):
        s = x[:, LANES * i:LANES * (i + 1)]
        up = pltpu.roll(s, LANES - ROPE_DIM // 2, 1)
        dn = pltpu.roll(s, ROPE_DIM // 2, 1)
        out.append(s * c + up * sa + dn * sb)
    return out[0] if len(out) == 1 else jnp.concatenate(out, axis=1)


def _row_halves(n):
    return (slice(0, n // 2), slice(n // 2, n))


def _rep4(a):
    ar = pltpu.roll(a, HEAD_DIM, 1)
    first = lax.broadcasted_iota(jnp.int32, a.shape, 1) < HEAD_DIM
    g0 = jnp.where(first, a, ar).astype(BF16)
    g1 = jnp.where(first, ar, a).astype(BF16)
    return [jnp.concatenate([g0, g0], axis=1), jnp.concatenate([g1, g1], axis=1)]


def _attend_windows(windows, first_valid, sinks, q, k, v):
    T = q.shape[0]
    k_rep, v_rep = _rep4(k), _rep4(v)
    head_lane = lax.broadcasted_iota(jnp.int32, (CHUNK, GQA_GROUP * HEAD_DIM), 1) // HEAD_DIM
    key_idx = lax.broadcasted_iota(jnp.int32, (GQA_GROUP * CHUNK, WINDOW + CHUNK), 1)
    ao_rows = []
    for ci in range(T // CHUNK):
        ao_g = []
        for g in range(N_KV_HEADS):
            qc = q[CHUNK * ci:CHUNK * (ci + 1), 256 * g:256 * (g + 1)]
            qm = jnp.concatenate([jnp.where(head_lane == h, qc, 0.0) for h in range(GQA_GROUP)], axis=0).astype(BF16)
            kw, vw = windows(k_rep, v_rep, g, ci)
            s = lax.dot_general(qm, kw, (((1,), (1,)), ((), ())), preferred_element_type=F32)
            fv = first_valid(ci)
            if fv is not None:
                s = jnp.where(key_idx >= fv, s, NEG_INF)
            ps = []
            for h in range(GQA_GROUP):
                sh = s[CHUNK * h:CHUNK * (h + 1)]
                sink = sinks[GQA_GROUP * g + h]
                m = jnp.maximum(jnp.max(sh, -1, keepdims=True), sink)
                p = jnp.exp(sh - m)
                denom = jnp.sum(p, -1, keepdims=True) + jnp.exp(sink - m)
                ps.append(p * (1.0 / denom))
            pm = jnp.concatenate(ps, axis=0).astype(BF16)
            r = _dot(pm, vw)
            o = jnp.where(head_lane == 0, r[0:CHUNK], 0.0)
            for h in range(1, GQA_GROUP):
                o = o + jnp.where(head_lane == h, r[CHUNK * h:CHUNK * (h + 1)], 0.0)
            ao_g.append(o)
        ao_rows.append(jnp.concatenate(ao_g, axis=1))
        yield
    return jnp.concatenate(ao_rows, axis=0), (k_rep, v_rep)


def _mixer(x, c, sa, sb, attend, LB, w):
    T = x.shape[0]
    nb = T // LB
    xb = x.astype(BF16)
    uv = jax.nn.gelu(_dot(xb, w.w_in[:, D_ATTN + 2 * D_KV:D_IN]))
    yield
    q = _dot(xb, w.w_in[:, 0:D_ATTN])
    kv = _dot(xb, w.w_in[:, D_ATTN:D_ATTN + 2 * D_KV])

    k = _rope(kv[:, :D_KV], c, sa, sb)
    v = kv[:, D_KV:]
    q = _rope(q, c, sa, sb) * (HEAD_DIM ** -0.5)

    yield
    u = uv[:, :D_CMLP]
    vm = _layernorm(uv[:, D_CMLP:], w.lnv_g[...], w.lnv_b[...])

    lo_mask = lax.broadcasted_iota(jnp.int32, (T, LANES), 1) < CMLP_GROUP_DIM
    pad = [jnp.zeros((CMLP_BLOCK - LB, LANES), BF16)] if LB < CMLP_BLOCK else []
    gate_slabs = []
    for p in range(D_CMLP // LANES):
        slab = vm[:, LANES * p:LANES * (p + 1)]
        lo = jnp.where(lo_mask, slab, 0.0).astype(BF16)
        hi = jnp.where(lo_mask, 0.0, slab).astype(BF16)
        rhs = jnp.concatenate(
            [jnp.concatenate([lo[LB * b:LB * (b + 1)]] + pad + [hi[LB * b:LB * (b + 1)]] + pad, axis=0)
             for b in range(nb)], axis=1)
        o = _dot(w.wpair[p, 0:LB, :], rhs)
        gate_slabs.append(jnp.concatenate([o[:, LANES * b:LANES * (b + 1)] for b in range(nb)], axis=0))
    s_gate = jnp.concatenate(gate_slabs, axis=1) + jnp.concatenate([w.bias[0:LB, :]] * nb, axis=0)
    co = u * s_gate
    yield

    ao, carry = yield from attend(q, k, v)

    an = _rmsnorm(ao, w.nga[...]).astype(BF16)
    cn = _rmsnorm(co, w.ngc[...]).astype(BF16)
    h1 = []
    for r in _row_halves(T):
        mix = _dot(an[r], w.w_out[0:D_ATTN, :]) + _dot(cn[r], w.w_out[D_ATTN:, :])
        yield
        h1.append(_layernorm(ALPHA * x[r] + mix, w.ln1g[...], w.ln1b[...]))
    return jnp.concatenate(h1, axis=0), k, v, vm, carry


def _ffn(h1, w):
    hb = h1.astype(BF16)
    acts = []
    for lo_c in range(0, D_FF, FF_CHUNK):
        hi_c = min(lo_c + FF_CHUNK, D_FF)
        gt = _dot(hb, w.wgu[:, lo_c:hi_c])
        up = _dot(hb, w.wgu[:, D_FF + lo_c:D_FF + hi_c])
        acts.append((gt * jax.nn.sigmoid(gt) * up).astype(BF16))
        yield
    act = jnp.concatenate(acts, axis=1)
    y = []
    for r in _row_halves(h1.shape[0]):
        f = _dot(act[r], w.wd[...])
        yield
        y.append(_layernorm(ALPHA * h1[r] + f, w.ln2g[...], w.ln2b[...]))
    return jnp.concatenate(y, axis=0)


def _run(*gens):
    results = [None] * len(gens)
    live = list(range(len(gens)))
    while live:
        for i in list(live):
            try:
                next(gens[i])
            except StopIteration as done:
                results[i] = done.value
                live.remove(i)
    return results


_WEIGHT_NAMES = ("w_in", "lnv_g", "lnv_b", "sinks", "wpair", "bias", "nga", "ngc", "w_out", "ln1g", "ln1b",
                 "wgu", "wd", "ln2g", "ln2b")
_Weights = collections.namedtuple("_Weights", _WEIGHT_NAMES)
N_W = len(_WEIGHT_NAMES)


def _pipelined_step(step, n_tiles, h1_ref, w, make_mixer, store_y, store_mixer, fill_extra=None):
    cur = step % 2
    prev = 1 - cur

    def phase(do_mixer, do_ffn, extra=None):
        gens = (([_ffn(h1_ref[prev], w)] if do_ffn else []) + ([extra()] if extra else [])
                + ([make_mixer()] if do_mixer else []))
        res = _run(*gens)
        if do_ffn:
            store_y(res[0])
        if do_mixer:
            h1_ref[cur] = res[-1][0]
            store_mixer(*res[-1][1:])

    pl.when(step == 0)(functools.partial(phase, True, False, fill_extra))
    pl.when(jnp.logical_and(step > 0, step < n_tiles))(functools.partial(phase, True, True))
    pl.when(step == n_tiles)(functools.partial(phase, False, True))


def _load_convert(jobs, then):
    plan = [(job, c) for job in jobs for c in range(job[0].shape[0] // job[4])]
    depth = W_SLOTS - 1

    def copy(i):
        (src, _, stage, sem, rows), c = plan[i]
        return pltpu.make_async_copy(src.at[pl.ds(c * rows, rows), :], stage.at[c % W_SLOTS], sem.at[c % W_SLOTS])

    for i in range(min(depth, len(plan))):
        copy(i).start(priority=i % 2)
    for i, ((_, dst, stage, _, rows), c) in enumerate(plan):
        copy(i).wait()
        dst[c * rows:(c + 1) * rows, :] = stage[c % W_SLOTS].astype(BF16)
        if i + depth < len(plan):
            copy(i + depth).start(priority=(i + depth) % 2)
        if (i + 1) % W_CHUNKS_PER_PHASE == 0:
            yield
    then()


def _prompt_body(T, tiles_per_seq, n_tiles, *refs):
    x_ref, c_ref, sa_ref, sb_ref = refs[:4]
    w = _Weights(*refs[4:4 + N_W])
    (y_ref, k_out_ref, v_out_ref, wgu_out, wd_out, hk_ref, hv_ref, h1_ref,
     wgu_v, wd_v, stage_gu, stage_d, sem_gu, sem_d, sem_out) = refs[4 + N_W:]
    step = pl.program_id(0)

    out_copies = (pltpu.make_async_copy(wgu_v, wgu_out, sem_out.at[0]),
                  pltpu.make_async_copy(wd_v, wd_out, sem_out.at[1]))

    def start_out_copies():
        for cp in out_copies:
            cp.start()

    def convert_weights(w_hbm=w):
        jobs = [(w_hbm.wgu, wgu_v, stage_gu, sem_gu, WGU_ROWS), (w_hbm.wd, wd_v, stage_d, sem_d, WD_ROWS)]
        return _load_convert(jobs, start_out_copies)

    @pl.when(step == n_tiles)
    def _():
        for cp in out_copies:
            cp.wait()

    w = w._replace(wgu=wgu_v, wd=wd_v)
    j = step % tiles_per_seq
    rd = j % 2
    wr = 1 - rd

    @pl.when(j == 0)
    def _():
        hk_ref[0] = jnp.zeros(hk_ref.shape[1:], BF16)
        hv_ref[0] = jnp.zeros(hv_ref.shape[1:], BF16)

    def first_valid(ci):
        if ci >= WINDOW // CHUNK:
            return None
        return jnp.where(j == 0, (WINDOW // CHUNK - ci) * CHUNK, 0)

    def make_mixer():
        hist_k = [hk_ref[rd, g] for g in range(N_KV_HEADS)]
        hist_v = [hv_ref[rd, g] for g in range(N_KV_HEADS)]

        def windows(k_rep, v_rep, g, ci):
            k_all = jnp.concatenate([hist_k[g], k_rep[g]], axis=0)
            v_all = jnp.concatenate([hist_v[g], v_rep[g]], axis=0)
            lo = CHUNK * ci
            return k_all[lo:lo + WINDOW + CHUNK], v_all[lo:lo + WINDOW + CHUNK]

        attend = functools.partial(_attend_windows, windows, first_valid, w.sinks)
        return _mixer(x_ref[...], c_ref[...], sa_ref[...], sb_ref[...], attend, CMLP_BLOCK, w)

    def store_y(y):
        y_ref[...] = y

    def store_mixer(k, v, _, carry):
        k_out_ref[0] = jnp.transpose(k[T - WINDOW:, :])
        v_out_ref[0] = jnp.transpose(v[T - WINDOW:, :])
        k_rep, v_rep = carry
        for g in range(N_KV_HEADS):
            hk_ref[wr, g] = k_rep[g][T - WINDOW:]
            hv_ref[wr, g] = v_rep[g][T - WINDOW:]

    _pipelined_step(step, n_tiles, h1_ref, w, make_mixer, store_y, store_mixer, fill_extra=convert_weights)


def _scatter_heads(out_ref, a):
    n = a.shape[1] // HEAD_DIM
    for i in range(n):
        out_ref[pl.ds(i, a.shape[0], stride=n), :] = a[:, HEAD_DIM * i:HEAD_DIM * (i + 1)]


def _cache_rep4(ref):
    rows = [jnp.transpose(ref[D_KV * s:D_KV * (s + 1), :]) for s in range(ref.shape[0] // D_KV)]
    return _rep4(jnp.concatenate(rows, axis=0))


def _sample_body(n_tiles, *refs):
    x_ref, c_ref, sa_ref, sb_ref, ck_ref, cv_ref = refs[:6]
    w = _Weights(*refs[6:6 + N_W])
    y_ref, k_out_ref, v_out_ref, vm_out_ref, h1_ref = refs[6 + N_W:]

    def make_mixer():
        ck_rep, cv_rep = _cache_rep4(ck_ref), _cache_rep4(cv_ref)

        def windows(k_rep, v_rep, g, ci):
            cat = lambda cache, new: jnp.concatenate(
                [cache[g][WINDOW * ci:WINDOW * (ci + 1)], new[g][CHUNK * ci:CHUNK * (ci + 1)]], axis=0)
            return cat(ck_rep, k_rep), cat(cv_rep, v_rep)

        attend = functools.partial(_attend_windows, windows, lambda ci: None, w.sinks)
        return _mixer(x_ref[...], c_ref[...], sa_ref[...], sb_ref[...], attend, CHUNK, w)

    def store_y(y):
        y_ref[...] = y

    def store_mixer(k, v, vm, *_):
        _scatter_heads(k_out_ref, k)
        _scatter_heads(v_out_ref, v)
        _scatter_heads(vm_out_ref, vm)

    _pipelined_step(pl.program_id(0), n_tiles, h1_ref, w, make_mixer, store_y, store_mixer)


def _cast_body(*refs):
    n = len(refs) // 2
    for src, dst in zip(refs[:n], refs[n:]):
        dst[...] = src[...].astype(BF16)


def _to_bf16(*ws):
    specs = [pl.BlockSpec((a.shape[0] // CAST_STEPS, a.shape[1]), lambda i: (i, 0)) for a in ws]
    return pl.pallas_call(
        _cast_body,
        grid=(CAST_STEPS,),
        in_specs=specs,
        out_specs=specs,
        out_shape=[jax.ShapeDtypeStruct(a.shape, BF16) for a in ws],
        compiler_params=pltpu.CompilerParams(dimension_semantics=("arbitrary",), vmem_limit_bytes=VMEM_LIMIT),
        name="cast_weights",
    )(*ws)


def _const_spec(shape):
    nd = len(shape)
    return pl.BlockSpec(shape, lambda *_: (0,) * nd)


def _weight_specs(ws, in_hbm=()):
    spec = lambda name, a: (pl.BlockSpec(memory_space=pltpu.SMEM) if name == "sinks" else
                            pl.BlockSpec(memory_space=pl.ANY) if name in in_hbm else _const_spec(a.shape))
    return [spec(name, a) for name, a in zip(_WEIGHT_NAMES, ws)]


def _rope_tables(pos, reps=1):
    half = ROPE_DIM // 2
    inv = np.power(ROPE_THETA, -np.arange(half, dtype=np.float64) * (2.0 / ROPE_DIM))
    ang = np.asarray(pos, np.float64)[:, None] * inv[None, :]
    cos, sin = np.cos(ang), np.sin(ang)
    n = ang.shape[0]
    pad = HEAD_DIM - ROPE_DIM
    c = np.concatenate([cos, cos, np.ones((n, pad))], 1)
    sa = np.concatenate([-sin, np.zeros((n, half + pad))], 1)
    sb = np.concatenate([np.zeros((n, half)), sin, np.zeros((n, pad))], 1)
    return [jnp.asarray(np.tile(t, (reps, LANES // HEAD_DIM)), F32) for t in (c, sa, sb)]


def _gate_params(w_s, b_s, lb):
    i = jnp.arange(lb)
    mask = (i[None, :] // CHUNK) <= (i[:, None] // CHUNK)
    w = jnp.where(mask[None], w_s[:, :lb, :lb], 0.0)
    wpair = jnp.concatenate([w[0::2], w[1::2]], axis=2).astype(BF16)
    bias = jnp.repeat(jnp.transpose(b_s[:, :lb]), CMLP_GROUP_DIM, axis=1)
    return wpair, bias


def kernel(x_prompt, x_sample, cache_win_k, cache_win_v, w_in, ln_v_g, ln_v_b, attn_sinks, w_spatial, b_spatial,
           norm_attn_g, norm_cmlp_g, w_out, ln1_g, ln1_b, w_gate_up, w_down, ln2_g, ln2_b):
    B, S, _ = x_prompt.shape
    Bd, L, _ = x_sample.shape
    assert cache_win_k.shape[2] == WINDOW and L == CHUNK and w_in.shape[0] == 1

    row = lambda a: a[0].reshape(1, -1)
    w_in_b, w_out_b = _to_bf16(w_in[0], w_out[0])

    wpair, bias = _gate_params(w_spatial[0], b_spatial[0], CMLP_BLOCK)

    def weights(wgu, wd):
        return _Weights(w_in=w_in_b, lnv_g=row(ln_v_g), lnv_b=row(ln_v_b), sinks=attn_sinks[0],
                        wpair=wpair, bias=bias, nga=row(norm_attn_g), ngc=row(norm_cmlp_g),
                        w_out=w_out_b, ln1g=row(ln1_g), ln1b=row(ln1_b),
                        wgu=wgu, wd=wd, ln2g=row(ln2_g), ln2b=row(ln2_b))

    T = PROMPT_TILE
    tps = S // T
    n_tiles = B * tps
    wp = weights(w_gate_up[0], w_down[0])
    tables = _rope_tables(np.arange(S))
    mix_tile = lambda s: jnp.minimum(s, n_tiles - 1)
    ffn_tile = lambda s: jnp.maximum(s - 1, 0)
    tab = pl.BlockSpec((T, LANES), lambda s: (mix_tile(s) % tps, 0))
    win = pl.BlockSpec((1, WINDOW, D_KV), lambda s: (mix_tile(s) // tps, 0, 0))
    hbm = pl.BlockSpec(memory_space=pl.ANY)
    y_p, kwin, vwin, wgu_b, wd_b = pl.pallas_call(
        functools.partial(_prompt_body, T, tps, n_tiles),
        grid=(n_tiles + 1,),
        in_specs=[pl.BlockSpec((T, D_MODEL), lambda s: (mix_tile(s), 0)), tab, tab, tab]
        + _weight_specs(wp, in_hbm=("wgu", "wd")),
        out_specs=[pl.BlockSpec((T, D_MODEL), lambda s: (ffn_tile(s), 0)), win, win, hbm, hbm],
        out_shape=[jax.ShapeDtypeStruct((B * S, D_MODEL), F32),
                   jax.ShapeDtypeStruct((B, WINDOW, D_KV), F32),
                   jax.ShapeDtypeStruct((B, WINDOW, D_KV), F32),
                   jax.ShapeDtypeStruct((D_MODEL, 2 * D_FF), BF16),
                   jax.ShapeDtypeStruct((D_FF, D_MODEL), BF16)],
        scratch_shapes=[pltpu.VMEM((2, N_KV_HEADS, WINDOW, 2 * LANES), BF16),
                        pltpu.VMEM((2, N_KV_HEADS, WINDOW, 2 * LANES), BF16),
                        pltpu.VMEM((2, T, D_MODEL), F32),
                        pltpu.VMEM((D_MODEL, 2 * D_FF), BF16),
                        pltpu.VMEM((D_FF, D_MODEL), BF16),
                        pltpu.VMEM((W_SLOTS, WGU_ROWS, 2 * D_FF), F32),
                        pltpu.VMEM((W_SLOTS, WD_ROWS, D_MODEL), F32),
                        pltpu.SemaphoreType.DMA((W_SLOTS,)),
                        pltpu.SemaphoreType.DMA((W_SLOTS,)),
                        pltpu.SemaphoreType.DMA((2,))],
        compiler_params=pltpu.CompilerParams(dimension_semantics=("arbitrary",), vmem_limit_bytes=VMEM_LIMIT),
        name="layer_prompt",
    )(x_prompt.reshape(B * S, D_MODEL), *tables, *wp)
    y_p = y_p.reshape(B, S, D_MODEL)

    NS = SAMPLE_SEQS
    Ts = NS * L
    ws = weights(wgu_b, wd_b)
    tables_s = _rope_tables(PAST_LEN + np.arange(L), reps=NS)
    n_st = Bd // NS
    mix_rows = lambda n, width: pl.BlockSpec((n, width), lambda s: (jnp.minimum(s, n_st - 1), 0))
    cache_rows = NS * D_KV
    cache_t = lambda a: jnp.transpose(a[0], (0, 2, 3, 1)).reshape(Bd * D_KV, WINDOW)
    y_s, k_s, v_s, vm_s = pl.pallas_call(
        functools.partial(_sample_body, n_st),
        grid=(n_st + 1,),
        in_specs=[mix_rows(Ts, D_MODEL)] + [_const_spec(t.shape) for t in tables_s]
        + [mix_rows(cache_rows, WINDOW), mix_rows(cache_rows, WINDOW)] + _weight_specs(ws),
        out_specs=[pl.BlockSpec((Ts, D_MODEL), lambda s: (jnp.maximum(s - 1, 0), 0)),
                   mix_rows(Ts * N_KV_HEADS, HEAD_DIM), mix_rows(Ts * N_KV_HEADS, HEAD_DIM),
                   mix_rows(Ts * CMLP_GROUPS, CMLP_GROUP_DIM)],
        out_shape=[jax.ShapeDtypeStruct((Bd * L, D_MODEL), F32),
                   jax.ShapeDtypeStruct((Bd * L * N_KV_HEADS, HEAD_DIM), F32),
                   jax.ShapeDtypeStruct((Bd * L * N_KV_HEADS, HEAD_DIM), F32),
                   jax.ShapeDtypeStruct((Bd * L * CMLP_GROUPS, CMLP_GROUP_DIM), F32)],
        scratch_shapes=[pltpu.VMEM((2, Ts, D_MODEL), F32)],
        compiler_params=pltpu.CompilerParams(dimension_semantics=("arbitrary",), vmem_limit_bytes=VMEM_LIMIT),
        name="layer_sample",
    )(x_sample.reshape(Bd * L, D_MODEL), *tables_s,
      cache_t(cache_win_k), cache_t(cache_win_v), *ws)

    win_result = lambda a: jnp.transpose(a.reshape(B, N_KV_HEADS, HEAD_DIM, WINDOW), (0, 3, 1, 2))[None]
    return (y_p, y_s.reshape(Bd, L, D_MODEL),
            win_result(kwin), win_result(vwin),
            k_s.reshape(1, Bd, L, N_KV_HEADS, HEAD_DIM),
            v_s.reshape(1, Bd, L, N_KV_HEADS, HEAD_DIM),
            vm_s.reshape(1, Bd, L, CMLP_GROUPS, CMLP_GROUP_DIM))
```

```python
import collections
import functools

import jax
import jax.numpy as jnp
import numpy as np
from jax import lax
from jax.experimental import pallas as pl
from jax.experimental.pallas import tpu as pltpu

D_MODEL = 1024
CHUNK = 64
HEAD_DIM = 64
D_ATTN = 512
D_CMLP = 512
N_HEADS = 8
N_KV_HEADS = 2
GQA_GROUP = 4
D_KV = 128
WINDOW = 128
ROPE_DIM = 16
ROPE_THETA = 500000.0
CMLP_BLOCK = 128
CMLP_GROUPS = 8
CMLP_GROUP_DIM = 64
D_IN = 1792
D_FF = 2816
PAST_LEN = 1024
ALPHA = 2.0 ** 0.25
LN_EPS = 1e-5
NEG_INF = -1e30
LANES = 128
FF_CHUNK = 256
VMEM_LIMIT = 56 * 1024 * 1024
CAST_STEPS = 4
PROMPT_TILE = 512
SAMPLE_SEQS = 4
WGU_ROWS, WD_ROWS = 16, 64
W_SLOTS = 8
W_CHUNKS_PER_PHASE = 8

F32 = jnp.float32
BF16 = jnp.bfloat16


def _layernorm(x, g, b):
    mu = jnp.mean(x, -1, keepdims=True)
    d = x - mu
    var = jnp.mean(d * d, -1, keepdims=True)
    return d * lax.rsqrt(var + LN_EPS) * g + b


def _rmsnorm(x, g):
    ms = jnp.mean(x * x, -1, keepdims=True)
    return x * lax.rsqrt(ms + LN_EPS) * g


def _dot(a, b):
    return jnp.dot(a, b, preferred_element_type=F32)


def _rope(x, c, sa, sb):
    out = []
    for i in range(x.shape[1] // LANES):
        s = x[:, LANES * i:LANES * (i + 1)]
        up = pltpu.roll(s, LANES - ROPE_DIM // 2, 1)
        dn = pltpu.roll(s, ROPE_DIM // 2, 1)
        out.append(s * c + up * sa + dn * sb)
    return out[0] if len(out) == 1 else jnp.concatenate(out, axis=1)


def _row_halves(n):
    return (slice(0, n // 2), slice(n // 2, n))


def _rep4(a):
    ar = pltpu.roll(a, HEAD_DIM, 1)
    first = lax.broadcasted_iota(jnp.int32, a.shape, 1) < HEAD_DIM
    g0 = jnp.where(first, a, ar).astype(BF16)
    g1 = jnp.where(first, ar, a).astype(BF16)
    return [jnp.concatenate([g0, g0], axis=1), jnp.concatenate([g1, g1], axis=1)]


def _attend_windows(windows, first_valid, sinks, q, k, v):
    T = q.shape[0]
    k_rep, v_rep = _rep4(k), _rep4(v)
    head_lane = lax.broadcasted_iota(jnp.int32, (CHUNK, GQA_GROUP * HEAD_DIM), 1) // HEAD_DIM
    key_idx = lax.broadcasted_iota(jnp.int32, (GQA_GROUP * CHUNK, WINDOW + CHUNK), 1)
    ao_rows = []
    for ci in range(T // CHUNK):
        ao_g = []
        for g in range(N_KV_HEADS):
            qc = q[CHUNK * ci:CHUNK * (ci + 1), 256 * g:256 * (g + 1)]
            qm = jnp.concatenate([jnp.where(head_lane == h, qc, 0.0) for h in range(GQA_GROUP)], axis=0).astype(BF16)
            kw, vw = windows(k_rep, v_rep, g, ci)
            s = lax.dot_general(qm, kw, (((1,), (1,)), ((), ())), preferred_element_type=F32)
            fv = first_valid(ci)
            if fv is not None:
                s = jnp.where(key_idx >= fv, s, NEG_INF)
            ps = []
            for h in range(GQA_GROUP):
                sh = s[CHUNK * h:CHUNK * (h + 1)]
                sink = sinks[GQA_GROUP * g + h]
                m = jnp.maximum(jnp.max(sh, -1, keepdims=True), sink)
                p = jnp.exp(sh - m)
                denom = jnp.sum(p, -1, keepdims=True) + jnp.exp(sink - m)
                ps.append(p * (1.0 / denom))
            pm = jnp.concatenate(ps, axis=0).astype(BF16)
            r = _dot(pm, vw)
            o = jnp.where(head_lane == 0, r[0:CHUNK], 0.0)
            for h in range(1, GQA_GROUP):
                o = o + jnp.where(head_lane == h, r[CHUNK * h:CHUNK * (h + 1)], 0.0)
            ao_g.append(o)
        ao_rows.append(jnp.concatenate(ao_g, axis=1))
        yield
    return jnp.concatenate(ao_rows, axis=0), (k_rep, v_rep)


def _mixer(x, c, sa, sb, attend, LB, w):
    T = x.shape[0]
    nb = T // LB
    xb = x.astype(BF16)
    uv = jax.nn.gelu(_dot(xb, w.w_in[:, D_ATTN + 2 * D_KV:D_IN]))
    yield
    q = _dot(xb, w.w_in[:, 0:D_ATTN])
    kv = _dot(xb, w.w_in[:, D_ATTN:D_ATTN + 2 * D_KV])

    k = _rope(kv[:, :D_KV], c, sa, sb)
    v = kv[:, D_KV:]
    q = _rope(q, c, sa, sb) * (HEAD_DIM ** -0.5)

    yield
    u = uv[:, :D_CMLP]
    vm = _layernorm(uv[:, D_CMLP:], w.lnv_g[...], w.lnv_b[...])

    lo_mask = lax.broadcasted_iota(jnp.int32, (T, LANES), 1) < CMLP_GROUP_DIM
    pad = [jnp.zeros((CMLP_BLOCK - LB, LANES), BF16)] if LB < CMLP_BLOCK else []
    gate_slabs = []
    for p in range(D_CMLP // LANES):
        slab = vm[:, LANES * p:LANES * (p + 1)]
        lo = jnp.where(lo_mask, slab, 0.0).astype(BF16)
        hi = jnp.where(lo_mask, 0.0, slab).astype(BF16)
        rhs = jnp.concatenate(
            [jnp.concatenate([lo[LB * b:LB * (b + 1)]] + pad + [hi[LB * b:LB * (b + 1)]] + pad, axis=0)
             for b in range(nb)], axis=1)
        o = _dot(w.wpair[p, 0:LB, :], rhs)
        gate_slabs.append(jnp.concatenate([o[:, LANES * b:LANES * (b + 1)] for b in range(nb)], axis=0))
    s_gate = jnp.concatenate(gate_slabs, axis=1) + jnp.concatenate([w.bias[0:LB, :]] * nb, axis=0)
    co = u * s_gate
    yield

    ao, carry = yield from attend(q, k, v)

    an = _rmsnorm(ao, w.nga[...]).astype(BF16)
    cn = _rmsnorm(co, w.ngc[...]).astype(BF16)
    h1 = []
    for r in _row_halves(T):
        mix = _dot(an[r], w.w_out[0:D_ATTN, :]) + _dot(cn[r], w.w_out[D_ATTN:, :])
        yield
        h1.append(_layernorm(ALPHA * x[r] + mix, w.ln1g[...], w.ln1b[...]))
    return jnp.concatenate(h1, axis=0), k, v, vm, carry


def _ffn(h1, w):
    hb = h1.astype(BF16)
    acts = []
    for lo_c in range(0, D_FF, FF_CHUNK):
        hi_c = min(lo_c + FF_CHUNK, D_FF)
        gt = _dot(hb, w.wgu[:, lo_c:hi_c])
        up = _dot(hb, w.wgu[:, D_FF + lo_c:D_FF + hi_c])
        acts.append((gt * jax.nn.sigmoid(gt) * up).astype(BF16))
        yield
    act = jnp.concatenate(acts, axis=1)
    y = []
    for r in _row_halves(h1.shape[0]):
        f = _dot(act[r], w.wd[...])
        yield
        y.append(_layernorm(ALPHA * h1[r] + f, w.ln2g[...], w.ln2b[...]))
    return jnp.concatenate(y, axis=0)


def _run(*gens):
    results = [None] * len(gens)
    live = list(range(len(gens)))
    while live:
        for i in list(live):
            try:
                next(gens[i])
            except StopIteration as done:
                results[i] = done.value
                live.remove(i)
    return results


_WEIGHT_NAMES = ("w_in", "lnv_g", "lnv_b", "sinks", "wpair", "bias", "nga", "ngc", "w_out", "ln1g", "ln1b",
                 "wgu", "wd", "ln2g", "ln2b")
_Weights = collections.namedtuple("_Weights", _WEIGHT_NAMES)
N_W = len(_WEIGHT_NAMES)


def _pipelined_step(step, n_tiles, h1_ref, w, make_mixer, store_y, store_mixer, fill_extra=None):
    cur = step % 2
    prev = 1 - cur

    def phase(do_mixer, do_ffn, extra=None):
        gens = (([_ffn(h1_ref[prev], w)] if do_ffn else []) + ([extra()] if extra else [])
                + ([make_mixer()] if do_mixer else []))
        res = _run(*gens)
        if do_ffn:
            store_y(res[0])
        if do_mixer:
            h1_ref[cur] = res[-1][0]
            store_mixer(*res[-1][1:])

    pl.when(step == 0)(functools.partial(phase, True, False, fill_extra))
    pl.when(jnp.logical_and(step > 0, step < n_tiles))(functools.partial(phase, True, True))
    pl.when(step == n_tiles)(functools.partial(phase, False, True))


def _load_convert(jobs, then):
    plan = [(job, c) for job in jobs for c in range(job[0].shape[0] // job[4])]
    depth = W_SLOTS - 1

    def copy(i):
        (src, _, stage, sem, rows), c = plan[i]
        return pltpu.make_async_copy(src.at[pl.ds(c * rows, rows), :], stage.at[c % W_SLOTS], sem.at[c % W_SLOTS])

    for i in range(min(depth, len(plan))):
        copy(i).start(priority=i % 2)
    for i, ((_, dst, stage, _, rows), c) in enumerate(plan):
        copy(i).wait()
        dst[c * rows:(c + 1) * rows, :] = stage[c % W_SLOTS].astype(BF16)
        if i + depth < len(plan):
            copy(i + depth).start(priority=(i + depth) % 2)
        if (i + 1) % W_CHUNKS_PER_PHASE == 0:
            yield
    then()


def _prompt_body(T, tiles_per_seq, n_tiles, *refs):
    x_ref, c_ref, sa_ref, sb_ref = refs[:4]
    w = _Weights(*refs[4:4 + N_W])
    (y_ref, k_out_ref, v_out_ref, wgu_out, wd_out, hk_ref, hv_ref, h1_ref,
     wgu_v, wd_v, stage_gu, stage_d, sem_gu, sem_d, sem_out) = refs[4 + N_W:]
    step = pl.program_id(0)

    out_copies = (pltpu.make_async_copy(wgu_v, wgu_out, sem_out.at[0]),
                  pltpu.make_async_copy(wd_v, wd_out, sem_out.at[1]))

    def start_out_copies():
        for cp in out_copies:
            cp.start()

    def convert_weights(w_hbm=w):
        jobs = [(w_hbm.wgu, wgu_v, stage_gu, sem_gu, WGU_ROWS), (w_hbm.wd, wd_v, stage_d, sem_d, WD_ROWS)]
        return _load_convert(jobs, start_out_copies)

    @pl.when(step == n_tiles)
    def _():
        for cp in out_copies:
            cp.wait()

    w = w._replace(wgu=wgu_v, wd=wd_v)
    j = step % tiles_per_seq
    rd = j % 2
    wr = 1 - rd

    @pl.when(j == 0)
    def _():
        hk_ref[0] = jnp.zeros(hk_ref.shape[1:], BF16)
        hv_ref[0] = jnp.zeros(hv_ref.shape[1:], BF16)

    def first_valid(ci):
        if ci >= WINDOW // CHUNK:
            return None
        return jnp.where(j == 0, (WINDOW // CHUNK - ci) * CHUNK, 0)

    def make_mixer():
        hist_k = [hk_ref[rd, g] for g in range(N_KV_HEADS)]
        hist_v = [hv_ref[rd, g] for g in range(N_KV_HEADS)]

        def windows(k_rep, v_rep, g, ci):
            k_all = jnp.concatenate([hist_k[g], k_rep[g]], axis=0)
            v_all = jnp.concatenate([hist_v[g], v_rep[g]], axis=0)
            lo = CHUNK * ci
            return k_all[lo:lo + WINDOW + CHUNK], v_all[lo:lo + WINDOW + CHUNK]

        attend = functools.partial(_attend_windows, windows, first_valid, w.sinks)
        return _mixer(x_ref[...], c_ref[...], sa_ref[...], sb_ref[...], attend, CMLP_BLOCK, w)

    def store_y(y):
        y_ref[...] = y

    def store_mixer(k, v, _, carry):
        k_out_ref[0] = jnp.transpose(k[T - WINDOW:, :])
        v_out_ref[0] = jnp.transpose(v[T - WINDOW:, :])
        k_rep, v_rep = carry
        for g in range(N_KV_HEADS):
            hk_ref[wr, g] = k_rep[g][T - WINDOW:]
            hv_ref[wr, g] = v_rep[g][T - WINDOW:]

    _pipelined_step(step, n_tiles, h1_ref, w, make_mixer, store_y, store_mixer, fill_extra=convert_weights)


def _scatter_heads(out_ref, a):
    n = a.shape[1] // HEAD_DIM
    for i in range(n):
        out_ref[pl.ds(i, a.shape[0], stride=n), :] = a[:, HEAD_DIM * i:HEAD_DIM * (i + 1)]


def _cache_rep4(ref):
    rows = [jnp.transpose(ref[D_KV * s:D_KV * (s + 1), :]) for s in range(ref.shape[0] // D_KV)]
    return _rep4(jnp.concatenate(rows, axis=0))


def _sample_body(n_tiles, *refs):
    x_ref, c_ref, sa_ref, sb_ref, ck_ref, cv_ref = refs[:6]
    w = _Weights(*refs[6:6 + N_W])
    y_ref, k_out_ref, v_out_ref, vm_out_ref, h1_ref, wgu_v, wd_v, sem = refs[6 + N_W:]
    step = pl.program_id(0)

    fetches = (pltpu.make_async_copy(w.wgu, wgu_v, sem.at[0]), pltpu.make_async_copy(w.wd, wd_v, sem.at[1]))

    @pl.when(step == 0)
    def _():
        for cp in fetches:
            cp.start()

    @pl.when(step == 1)
    def _():
        for cp in fetches:
            cp.wait()

    w = w._replace(wgu=wgu_v, wd=wd_v)

    def make_mixer():
        ck_rep, cv_rep = _cache_rep4(ck_ref), _cache_rep4(cv_ref)

        def windows(k_rep, v_rep, g, ci):
            cat = lambda cache, new: jnp.concatenate(
                [cache[g][WINDOW * ci:WINDOW * (ci + 1)], new[g][CHUNK * ci:CHUNK * (ci + 1)]], axis=0)
            return cat(ck_rep, k_rep), cat(cv_rep, v_rep)

        attend = functools.partial(_attend_windows, windows, lambda ci: None, w.sinks)
        return _mixer(x_ref[...], c_ref[...], sa_ref[...], sb_ref[...], attend, CHUNK, w)

    def store_y(y):
        y_ref[...] = y

    def store_mixer(k, v, vm, *_):
        _scatter_heads(k_out_ref, k)
        _scatter_heads(v_out_ref, v)
        _scatter_heads(vm_out_ref, vm)

    _pipelined_step(step, n_tiles, h1_ref, w, make_mixer, store_y, store_mixer)


def _cast_body(*refs):
    n = len(refs) // 2
    for src, dst in zip(refs[:n], refs[n:]):
        dst[...] = src[...].astype(BF16)


def _to_bf16(*ws):
    specs = [pl.BlockSpec((a.shape[0] // CAST_STEPS, a.shape[1]), lambda i: (i, 0)) for a in ws]
    return pl.pallas_call(
        _cast_body,
        grid=(CAST_STEPS,),
        in_specs=specs,
        out_specs=specs,
        out_shape=[jax.ShapeDtypeStruct(a.shape, BF16) for a in ws],
        compiler_params=pltpu.CompilerParams(dimension_semantics=("arbitrary",), vmem_limit_bytes=VMEM_LIMIT),
        name="cast_weights",
    )(*ws)


def _const_spec(shape):
    nd = len(shape)
    return pl.BlockSpec(shape, lambda *_: (0,) * nd)


def _weight_specs(ws, in_hbm=()):
    spec = lambda name, a: (pl.BlockSpec(memory_space=pltpu.SMEM) if name == "sinks" else
                            pl.BlockSpec(memory_space=pl.ANY) if name in in_hbm else _const_spec(a.shape))
    return [spec(name, a) for name, a in zip(_WEIGHT_NAMES, ws)]


def _rope_tables(pos, reps=1):
    half = ROPE_DIM // 2
    inv = np.power(ROPE_THETA, -np.arange(half, dtype=np.float64) * (2.0 / ROPE_DIM))
    ang = np.asarray(pos, np.float64)[:, None] * inv[None, :]
    cos, sin = np.cos(ang), np.sin(ang)
    n = ang.shape[0]
    pad = HEAD_DIM - ROPE_DIM
    c = np.concatenate([cos, cos, np.ones((n, pad))], 1)
    sa = np.concatenate([-sin, np.zeros((n, half + pad))], 1)
    sb = np.concatenate([np.zeros((n, half)), sin, np.zeros((n, pad))], 1)
    return [jnp.asarray(np.tile(t, (reps, LANES // HEAD_DIM)), F32) for t in (c, sa, sb)]


def _gate_params(w_s, b_s, lb):
    i = jnp.arange(lb)
    mask = (i[None, :] // CHUNK) <= (i[:, None] // CHUNK)
    w = jnp.where(mask[None], w_s[:, :lb, :lb], 0.0)
    wpair = jnp.concatenate([w[0::2], w[1::2]], axis=2).astype(BF16)
    bias = jnp.repeat(jnp.transpose(b_s[:, :lb]), CMLP_GROUP_DIM, axis=1)
    return wpair, bias


def kernel(x_prompt, x_sample, cache_win_k, cache_win_v, w_in, ln_v_g, ln_v_b, attn_sinks, w_spatial, b_spatial,
           norm_attn_g, norm_cmlp_g, w_out, ln1_g, ln1_b, w_gate_up, w_down, ln2_g, ln2_b):
    B, S, _ = x_prompt.shape
    Bd, L, _ = x_sample.shape
    assert cache_win_k.shape[2] == WINDOW and L == CHUNK and w_in.shape[0] == 1

    row = lambda a: a[0].reshape(1, -1)
    w_in_b, w_out_b = _to_bf16(w_in[0], w_out[0])

    wpair, bias = _gate_params(w_spatial[0], b_spatial[0], CMLP_BLOCK)

    def weights(wgu, wd):
        return _Weights(w_in=w_in_b, lnv_g=row(ln_v_g), lnv_b=row(ln_v_b), sinks=attn_sinks[0],
                        wpair=wpair, bias=bias, nga=row(norm_attn_g), ngc=row(norm_cmlp_g),
                        w_out=w_out_b, ln1g=row(ln1_g), ln1b=row(ln1_b),
                        wgu=wgu, wd=wd, ln2g=row(ln2_g), ln2b=row(ln2_b))

    T = PROMPT_TILE
    tps = S // T
    n_tiles = B * tps
    wp = weights(w_gate_up[0], w_down[0])
    tables = _rope_tables(np.arange(S))
    mix_tile = lambda s: jnp.minimum(s, n_tiles - 1)
    ffn_tile = lambda s: jnp.maximum(s - 1, 0)
    tab = pl.BlockSpec((T, LANES), lambda s: (mix_tile(s) % tps, 0))
    win = pl.BlockSpec((1, WINDOW, D_KV), lambda s: (mix_tile(s) // tps, 0, 0))
    hbm = pl.BlockSpec(memory_space=pl.ANY)
    y_p, kwin, vwin, wgu_b, wd_b = pl.pallas_call(
        functools.partial(_prompt_body, T, tps, n_tiles),
        grid=(n_tiles + 1,),
        in_specs=[pl.BlockSpec((T, D_MODEL), lambda s: (mix_tile(s), 0)), tab, tab, tab]
        + _weight_specs(wp, in_hbm=("wgu", "wd")),
        out_specs=[pl.BlockSpec((T, D_MODEL), lambda s: (ffn_tile(s), 0)), win, win, hbm, hbm],
        out_shape=[jax.ShapeDtypeStruct((B * S, D_MODEL), F32),
                   jax.ShapeDtypeStruct((B, WINDOW, D_KV), F32),
                   jax.ShapeDtypeStruct((B, WINDOW, D_KV), F32),
                   jax.ShapeDtypeStruct((D_MODEL, 2 * D_FF), BF16),
                   jax.ShapeDtypeStruct((D_FF, D_MODEL), BF16)],
        scratch_shapes=[pltpu.VMEM((2, N_KV_HEADS, WINDOW, 2 * LANES), BF16),
                        pltpu.VMEM((2, N_KV_HEADS, WINDOW, 2 * LANES), BF16),
                        pltpu.VMEM((2, T, D_MODEL), F32),
                        pltpu.VMEM((D_MODEL, 2 * D_FF), BF16),
                        pltpu.VMEM((D_FF, D_MODEL), BF16),
                        pltpu.VMEM((W_SLOTS, WGU_ROWS, 2 * D_FF), F32),
                        pltpu.VMEM((W_SLOTS, WD_ROWS, D_MODEL), F32),
                        pltpu.SemaphoreType.DMA((W_SLOTS,)),
                        pltpu.SemaphoreType.DMA((W_SLOTS,)),
                        pltpu.SemaphoreType.DMA((2,))],
        compiler_params=pltpu.CompilerParams(dimension_semantics=("arbitrary",), vmem_limit_bytes=VMEM_LIMIT),
        name="layer_prompt",
    )(x_prompt.reshape(B * S, D_MODEL), *tables, *wp)
    y_p = y_p.reshape(B, S, D_MODEL)

    NS = SAMPLE_SEQS
    Ts = NS * L
    ws = weights(wgu_b, wd_b)
    tables_s = _rope_tables(PAST_LEN + np.arange(L), reps=NS)
    n_st = Bd // NS
    mix_rows = lambda n, width: pl.BlockSpec((n, width), lambda s: (jnp.minimum(s, n_st - 1), 0))
    cache_rows = NS * D_KV
    cache_t = lambda a: jnp.transpose(a[0], (0, 2, 3, 1)).reshape(Bd * D_KV, WINDOW)
    y_s, k_s, v_s, vm_s = pl.pallas_call(
        functools.partial(_sample_body, n_st),
        grid=(n_st + 1,),
        in_specs=[mix_rows(Ts, D_MODEL)] + [_const_spec(t.shape) for t in tables_s]
        + [mix_rows(cache_rows, WINDOW), mix_rows(cache_rows, WINDOW)] + _weight_specs(ws, in_hbm=("wgu", "wd")),
        out_specs=[pl.BlockSpec((Ts, D_MODEL), lambda s: (jnp.maximum(s - 1, 0), 0)),
                   mix_rows(Ts * N_KV_HEADS, HEAD_DIM), mix_rows(Ts * N_KV_HEADS, HEAD_DIM),
                   mix_rows(Ts * CMLP_GROUPS, CMLP_GROUP_DIM)],
        out_shape=[jax.ShapeDtypeStruct((Bd * L, D_MODEL), F32),
                   jax.ShapeDtypeStruct((Bd * L * N_KV_HEADS, HEAD_DIM), F32),
                   jax.ShapeDtypeStruct((Bd * L * N_KV_HEADS, HEAD_DIM), F32),
                   jax.ShapeDtypeStruct((Bd * L * CMLP_GROUPS, CMLP_GROUP_DIM), F32)],
        scratch_shapes=[pltpu.VMEM((2, Ts, D_MODEL), F32),
                        pltpu.VMEM((D_MODEL, 2 * D_FF), BF16),
                        pltpu.VMEM((D_FF, D_MODEL), BF16),
                        pltpu.SemaphoreType.DMA((2,))],
        compiler_params=pltpu.CompilerParams(dimension_semantics=("arbitrary",), vmem_limit_bytes=VMEM_LIMIT),
        name="layer_sample",
    )(x_sample.reshape(Bd * L, D_MODEL), *tables_s,
      cache_t(cache_win_k), cache_t(cache_win_v), *ws)

    win_result = lambda a: jnp.transpose(a.reshape(B, N_KV_HEADS, HEAD_DIM, WINDOW), (0, 3, 1, 2))[None]
    return (y_p, y_s.reshape(Bd, L, D_MODEL),
            win_result(kwin), win_result(vwin),
            k_s.reshape(1, Bd, L, N_KV_HEADS, HEAD_DIM),
            v_s.reshape(1, Bd, L, N_KV_HEADS, HEAD_DIM),
            vm_s.reshape(1, Bd, L, CMLP_GROUPS, CMLP_GROUP_DIM))
```

```python
import collections
import functools

import jax
import jax.numpy as jnp
import numpy as np
from jax import lax
from jax.experimental import pallas as pl
from jax.experimental.pallas import tpu as pltpu

D_MODEL = 1024
CHUNK = 64
HEAD_DIM = 64
D_ATTN = 512
D_CMLP = 512
N_HEADS = 8
N_KV_HEADS = 2
GQA_GROUP = 4
D_KV = 128
WINDOW = 128
ROPE_DIM = 16
ROPE_THETA = 500000.0
CMLP_BLOCK = 128
CMLP_GROUPS = 8
CMLP_GROUP_DIM = 64
D_IN = 1792
D_FF = 2816
PAST_LEN = 1024
ALPHA = 2.0 ** 0.25
LN_EPS = 1e-5
NEG_INF = -1e30
LANES = 128
FF_CHUNK = 256
VMEM_LIMIT = 56 * 1024 * 1024
PROMPT_TILE = 512
SAMPLE_SEQS = 8
W_IN_ROWS, WGU_ROWS, WD_ROWS = 32, 16, 64
W_SLOTS = 8
W_CHUNKS_PER_PHASE = 8

F32 = jnp.float32
BF16 = jnp.bfloat16


def _layernorm(x, g, b):
    mu = jnp.mean(x, -1, keepdims=True)
    d = x - mu
    var = jnp.mean(d * d, -1, keepdims=True)
    return d * lax.rsqrt(var + LN_EPS) * g + b


def _rmsnorm(x, g):
    ms = jnp.mean(x * x, -1, keepdims=True)
    return x * lax.rsqrt(ms + LN_EPS) * g


def _dot(a, b):
    return jnp.dot(a, b, preferred_element_type=F32)


def _rope(x, c, sa, sb):
    out = []
    for i in range(x.shape[1] // LANES):
        s = x[:, LANES * i:LANES * (i + 1)]
        up = pltpu.roll(s, LANES - ROPE_DIM // 2, 1)
        dn = pltpu.roll(s, ROPE_DIM // 2, 1)
        out.append(s * c + up * sa + dn * sb)
    return out[0] if len(out) == 1 else jnp.concatenate(out, axis=1)


def _row_halves(n):
    return (slice(0, n // 2), slice(n // 2, n))


def _rep4(a):
    ar = pltpu.roll(a, HEAD_DIM, 1)
    first = lax.broadcasted_iota(jnp.int32, a.shape, 1) < HEAD_DIM
    g0 = jnp.where(first, a, ar).astype(BF16)
    g1 = jnp.where(first, ar, a).astype(BF16)
    return [jnp.concatenate([g0, g0], axis=1), jnp.concatenate([g1, g1], axis=1)]


def _attend_windows(windows, first_valid, sinks, q, k, v):
    T = q.shape[0]
    k_rep, v_rep = _rep4(k), _rep4(v)
    head_lane = lax.broadcasted_iota(jnp.int32, (CHUNK, GQA_GROUP * HEAD_DIM), 1) // HEAD_DIM
    key_idx = lax.broadcasted_iota(jnp.int32, (GQA_GROUP * CHUNK, WINDOW + CHUNK), 1)
    ao_rows = []
    for ci in range(T // CHUNK):
        ao_g = []
        for g in range(N_KV_HEADS):
            qc = q[CHUNK * ci:CHUNK * (ci + 1), 256 * g:256 * (g + 1)]
            qm = jnp.concatenate([jnp.where(head_lane == h, qc, 0.0) for h in range(GQA_GROUP)], axis=0).astype(BF16)
            kw, vw = windows(k_rep, v_rep, g, ci)
            s = lax.dot_general(qm, kw, (((1,), (1,)), ((), ())), preferred_element_type=F32)
            fv = first_valid(ci)
            if fv is not None:
                s = jnp.where(key_idx >= fv, s, NEG_INF)
            ps = []
            for h in range(GQA_GROUP):
                sh = s[CHUNK * h:CHUNK * (h + 1)]
                sink = sinks[GQA_GROUP * g + h]
                m = jnp.maximum(jnp.max(sh, -1, keepdims=True), sink)
                p = jnp.exp(sh - m)
                denom = jnp.sum(p, -1, keepdims=True) + jnp.exp(sink - m)
                ps.append(p * (1.0 / denom))
            pm = jnp.concatenate(ps, axis=0).astype(BF16)
            r = _dot(pm, vw)
            o = jnp.where(head_lane == 0, r[0:CHUNK], 0.0)
            for h in range(1, GQA_GROUP):
                o = o + jnp.where(head_lane == h, r[CHUNK * h:CHUNK * (h + 1)], 0.0)
            ao_g.append(o)
        ao_rows.append(jnp.concatenate(ao_g, axis=1))
        yield
    return jnp.concatenate(ao_rows, axis=0), (k_rep, v_rep)


def _mixer(x, c, sa, sb, attend, LB, w):
    T = x.shape[0]
    nb = T // LB
    xb = x.astype(BF16)
    uv = jax.nn.gelu(_dot(xb, w.w_in[:, D_ATTN + 2 * D_KV:D_IN]))
    yield
    q = _dot(xb, w.w_in[:, 0:D_ATTN])
    kv = _dot(xb, w.w_in[:, D_ATTN:D_ATTN + 2 * D_KV])

    k = _rope(kv[:, :D_KV], c, sa, sb)
    v = kv[:, D_KV:]
    q = _rope(q, c, sa, sb) * (HEAD_DIM ** -0.5)

    yield
    u = uv[:, :D_CMLP]
    vm = _layernorm(uv[:, D_CMLP:], w.lnv_g[...], w.lnv_b[...])

    lo_mask = lax.broadcasted_iota(jnp.int32, (T, LANES), 1) < CMLP_GROUP_DIM
    pad = [jnp.zeros((CMLP_BLOCK - LB, LANES), BF16)] if LB < CMLP_BLOCK else []
    gate_slabs = []
    for p in range(D_CMLP // LANES):
        slab = vm[:, LANES * p:LANES * (p + 1)]
        lo = jnp.where(lo_mask, slab, 0.0).astype(BF16)
        hi = jnp.where(lo_mask, 0.0, slab).astype(BF16)
        rhs = jnp.concatenate(
            [jnp.concatenate([lo[LB * b:LB * (b + 1)]] + pad + [hi[LB * b:LB * (b + 1)]] + pad, axis=0)
             for b in range(nb)], axis=1)
        o = _dot(w.wpair[p, 0:LB, :], rhs)
        gate_slabs.append(jnp.concatenate([o[:, LANES * b:LANES * (b + 1)] for b in range(nb)], axis=0))
    s_gate = jnp.concatenate(gate_slabs, axis=1) + jnp.concatenate([w.bias[0:LB, :]] * nb, axis=0)
    co = u * s_gate
    yield

    ao, carry = yield from attend(q, k, v)

    an = _rmsnorm(ao, w.nga[...]).astype(BF16)
    cn = _rmsnorm(co, w.ngc[...]).astype(BF16)
    h1 = []
    for r in _row_halves(T):
        mix = _dot(an[r], w.w_out[0:D_ATTN, :]) + _dot(cn[r], w.w_out[D_ATTN:, :])
        yield
        h1.append(_layernorm(ALPHA * x[r] + mix, w.ln1g[...], w.ln1b[...]))
    return jnp.concatenate(h1, axis=0), k, v, vm, carry


def _ffn(h1, w):
    hb = h1.astype(BF16)
    acts = []
    for lo_c in range(0, D_FF, FF_CHUNK):
        hi_c = min(lo_c + FF_CHUNK, D_FF)
        gt = _dot(hb, w.wgu[:, lo_c:hi_c])
        up = _dot(hb, w.wgu[:, D_FF + lo_c:D_FF + hi_c])
        acts.append((gt * jax.nn.sigmoid(gt) * up).astype(BF16))
        yield
    act = jnp.concatenate(acts, axis=1)
    y = []
    for r in _row_halves(h1.shape[0]):
        f = _dot(act[r], w.wd[...])
        yield
        y.append(_layernorm(ALPHA * h1[r] + f, w.ln2g[...], w.ln2b[...]))
    return jnp.concatenate(y, axis=0)


def _run(*gens):
    results = [None] * len(gens)
    live = list(range(len(gens)))
    while live:
        for i in list(live):
            try:
                next(gens[i])
            except StopIteration as done:
                results[i] = done.value
                live.remove(i)
    return results


_WEIGHT_NAMES = ("w_in", "lnv_g", "lnv_b", "sinks", "wpair", "bias", "nga", "ngc", "w_out", "ln1g", "ln1b",
                 "wgu", "wd", "ln2g", "ln2b")
_Weights = collections.namedtuple("_Weights", _WEIGHT_NAMES)
N_W = len(_WEIGHT_NAMES)


def _pipelined_step(step, n_tiles, h1_ref, w, make_mixer, store_y, store_mixer, fill_first=None, fill_extra=None):
    cur = step % 2
    prev = 1 - cur

    def phase(do_mixer, do_ffn, first=None, extra=None):
        if first:
            _run(first())
        gens = (([_ffn(h1_ref[prev], w)] if do_ffn else []) + ([extra()] if extra else [])
                + ([make_mixer()] if do_mixer else []))
        res = _run(*gens)
        if do_ffn:
            store_y(res[0])
        if do_mixer:
            h1_ref[cur] = res[-1][0]
            store_mixer(*res[-1][1:])

    pl.when(step == 0)(functools.partial(phase, True, False, fill_first, fill_extra))
    pl.when(jnp.logical_and(step > 0, step < n_tiles))(functools.partial(phase, True, True))
    pl.when(step == n_tiles)(functools.partial(phase, False, True))


def _load_convert(jobs, then):
    plan = [(job, c) for job in jobs for c in range(job[0].shape[0] // job[4])]
    depth = W_SLOTS - 1

    def copy(i):
        (src, _, stage, sem, rows), c = plan[i]
        return pltpu.make_async_copy(src.at[pl.ds(c * rows, rows), :], stage.at[c % W_SLOTS], sem.at[c % W_SLOTS])

    for i in range(min(depth, len(plan))):
        copy(i).start(priority=i % 2)
    for i, ((_, dst, stage, _, rows), c) in enumerate(plan):
        copy(i).wait()
        dst[c * rows:(c + 1) * rows, :] = stage[c % W_SLOTS].astype(BF16)
        if i + depth < len(plan):
            copy(i + depth).start(priority=(i + depth) % 2)
        if (i + 1) % W_CHUNKS_PER_PHASE == 0:
            yield
    then()


def _prompt_body(T, tiles_per_seq, n_tiles, *refs):
    x_ref, c_ref, sa_ref, sb_ref = refs[:4]
    w = _Weights(*refs[4:4 + N_W])
    (y_ref, k_out_ref, v_out_ref, w_in_out, w_out_out, wgu_out, wd_out, hk_ref, hv_ref, h1_ref,
     w_in_v, w_out_v, wgu_v, wd_v, stage_in, stage_gu, stage_d, sem_in, sem_gu, sem_d, sem_out) = refs[4 + N_W:]
    step = pl.program_id(0)

    out_copies = (pltpu.make_async_copy(w_in_v, w_in_out, sem_out.at[0]),
                  pltpu.make_async_copy(w_out_v, w_out_out, sem_out.at[1]),
                  pltpu.make_async_copy(wgu_v, wgu_out, sem_out.at[2]),
                  pltpu.make_async_copy(wd_v, wd_out, sem_out.at[3]))

    def convert_first(w_hbm=w):
        return _load_convert([(w_hbm.w_in, w_in_v, stage_in, sem_in, W_IN_ROWS)], out_copies[0].start)

    def start_other_out_copies():
        for cp in out_copies[1:]:
            cp.start()

    def convert_weights(w_hbm=w):
        jobs = [(w_hbm.w_out, w_out_v, stage_d, sem_d, WD_ROWS), (w_hbm.wgu, wgu_v, stage_gu, sem_gu, WGU_ROWS),
                (w_hbm.wd, wd_v, stage_d, sem_d, WD_ROWS)]
        return _load_convert(jobs, start_other_out_copies)

    @pl.when(step == n_tiles)
    def _():
        for cp in out_copies:
            cp.wait()

    w = w._replace(w_in=w_in_v, w_out=w_out_v, wgu=wgu_v, wd=wd_v)
    j = step % tiles_per_seq
    rd = j % 2
    wr = 1 - rd

    @pl.when(j == 0)
    def _():
        hk_ref[0] = jnp.zeros(hk_ref.shape[1:], BF16)
        hv_ref[0] = jnp.zeros(hv_ref.shape[1:], BF16)

    def first_valid(ci):
        if ci >= WINDOW // CHUNK:
            return None
        return jnp.where(j == 0, (WINDOW // CHUNK - ci) * CHUNK, 0)

    def make_mixer():
        hist_k = [hk_ref[rd, g] for g in range(N_KV_HEADS)]
        hist_v = [hv_ref[rd, g] for g in range(N_KV_HEADS)]

        def windows(k_rep, v_rep, g, ci):
            k_all = jnp.concatenate([hist_k[g], k_rep[g]], axis=0)
            v_all = jnp.concatenate([hist_v[g], v_rep[g]], axis=0)
            lo = CHUNK * ci
            return k_all[lo:lo + WINDOW + CHUNK], v_all[lo:lo + WINDOW + CHUNK]

        attend = functools.partial(_attend_windows, windows, first_valid, w.sinks)
        return _mixer(x_ref[...], c_ref[...], sa_ref[...], sb_ref[...], attend, CMLP_BLOCK, w)

    def store_y(y):
        y_ref[...] = y

    def store_mixer(k, v, _, carry):
        k_out_ref[0] = jnp.transpose(k[T - WINDOW:, :])
        v_out_ref[0] = jnp.transpose(v[T - WINDOW:, :])
        k_rep, v_rep = carry
        for g in range(N_KV_HEADS):
            hk_ref[wr, g] = k_rep[g][T - WINDOW:]
            hv_ref[wr, g] = v_rep[g][T - WINDOW:]

    _pipelined_step(step, n_tiles, h1_ref, w, make_mixer, store_y, store_mixer,
                    fill_first=convert_first, fill_extra=convert_weights)


def _scatter_heads(out_ref, a):
    n = a.shape[1] // HEAD_DIM
    for i in range(n):
        out_ref[pl.ds(i, a.shape[0], stride=n), :] = a[:, HEAD_DIM * i:HEAD_DIM * (i + 1)]


def _cache_rep4(ref):
    rows = [jnp.transpose(ref[D_KV * s:D_KV * (s + 1), :]) for s in range(ref.shape[0] // D_KV)]
    return _rep4(jnp.concatenate(rows, axis=0))


def _sample_body(n_tiles, *refs):
    x_ref, c_ref, sa_ref, sb_ref, ck_ref, cv_ref = refs[:6]
    w = _Weights(*refs[6:6 + N_W])
    y_ref, k_out_ref, v_out_ref, vm_out_ref, h1_ref, wgu_v, wd_v, sem = refs[6 + N_W:]
    step = pl.program_id(0)

    fetches = (pltpu.make_async_copy(w.wgu, wgu_v, sem.at[0]), pltpu.make_async_copy(w.wd, wd_v, sem.at[1]))

    @pl.when(step == 0)
    def _():
        for cp in fetches:
            cp.start()

    @pl.when(step == 1)
    def _():
        for cp in fetches:
            cp.wait()

    w = w._replace(wgu=wgu_v, wd=wd_v)

    def make_mixer():
        ck_rep, cv_rep = _cache_rep4(ck_ref), _cache_rep4(cv_ref)

        def windows(k_rep, v_rep, g, ci):
            cat = lambda cache, new: jnp.concatenate(
                [cache[g][WINDOW * ci:WINDOW * (ci + 1)], new[g][CHUNK * ci:CHUNK * (ci + 1)]], axis=0)
            return cat(ck_rep, k_rep), cat(cv_rep, v_rep)

        attend = functools.partial(_attend_windows, windows, lambda ci: None, w.sinks)
        return _mixer(x_ref[...], c_ref[...], sa_ref[...], sb_ref[...], attend, CHUNK, w)

    def store_y(y):
        y_ref[...] = y

    def store_mixer(k, v, vm, *_):
        _scatter_heads(k_out_ref, k)
        _scatter_heads(v_out_ref, v)
        _scatter_heads(vm_out_ref, vm)

    _pipelined_step(step, n_tiles, h1_ref, w, make_mixer, store_y, store_mixer)


def _const_spec(shape):
    nd = len(shape)
    return pl.BlockSpec(shape, lambda *_: (0,) * nd)


def _weight_specs(ws, in_hbm=()):
    spec = lambda name, a: (pl.BlockSpec(memory_space=pltpu.SMEM) if name == "sinks" else
                            pl.BlockSpec(memory_space=pl.ANY) if name in in_hbm else _const_spec(a.shape))
    return [spec(name, a) for name, a in zip(_WEIGHT_NAMES, ws)]


def _rope_tables(pos, reps=1):
    half = ROPE_DIM // 2
    inv = np.power(ROPE_THETA, -np.arange(half, dtype=np.float64) * (2.0 / ROPE_DIM))
    ang = np.asarray(pos, np.float64)[:, None] * inv[None, :]
    cos, sin = np.cos(ang), np.sin(ang)
    n = ang.shape[0]
    pad = HEAD_DIM - ROPE_DIM
    c = np.concatenate([cos, cos, np.ones((n, pad))], 1)
    sa = np.concatenate([-sin, np.zeros((n, half + pad))], 1)
    sb = np.concatenate([np.zeros((n, half)), sin, np.zeros((n, pad))], 1)
    return [jnp.asarray(np.tile(t, (reps, LANES // HEAD_DIM)), F32) for t in (c, sa, sb)]


def _gate_params(w_s, b_s, lb):
    i = jnp.arange(lb)
    mask = (i[None, :] // CHUNK) <= (i[:, None] // CHUNK)
    w = jnp.where(mask[None], w_s[:, :lb, :lb], 0.0)
    wpair = jnp.concatenate([w[0::2], w[1::2]], axis=2).astype(BF16)
    bias = jnp.repeat(jnp.transpose(b_s[:, :lb]), CMLP_GROUP_DIM, axis=1)
    return wpair, bias


def kernel(x_prompt, x_sample, cache_win_k, cache_win_v, w_in, ln_v_g, ln_v_b, attn_sinks, w_spatial, b_spatial,
           norm_attn_g, norm_cmlp_g, w_out, ln1_g, ln1_b, w_gate_up, w_down, ln2_g, ln2_b):
    B, S, _ = x_prompt.shape
    Bd, L, _ = x_sample.shape
    assert cache_win_k.shape[2] == WINDOW and L == CHUNK and w_in.shape[0] == 1

    row = lambda a: a[0].reshape(1, -1)
    wpair, bias = _gate_params(w_spatial[0], b_spatial[0], CMLP_BLOCK)

    def weights(w_in_, w_out_, wgu, wd):
        return _Weights(w_in=w_in_, lnv_g=row(ln_v_g), lnv_b=row(ln_v_b), sinks=attn_sinks[0],
                        wpair=wpair, bias=bias, nga=row(norm_attn_g), ngc=row(norm_cmlp_g),
                        w_out=w_out_, ln1g=row(ln1_g), ln1b=row(ln1_b),
                        wgu=wgu, wd=wd, ln2g=row(ln2_g), ln2b=row(ln2_b))

    T = PROMPT_TILE
    tps = S // T
    n_tiles = B * tps
    wp = weights(w_in[0], w_out[0], w_gate_up[0], w_down[0])
    tables = _rope_tables(np.arange(S))
    mix_tile = lambda s: jnp.minimum(s, n_tiles - 1)
    ffn_tile = lambda s: jnp.maximum(s - 1, 0)
    tab = pl.BlockSpec((T, LANES), lambda s: (mix_tile(s) % tps, 0))
    win = pl.BlockSpec((1, WINDOW, D_KV), lambda s: (mix_tile(s) // tps, 0, 0))
    hbm = pl.BlockSpec(memory_space=pl.ANY)
    y_p, kwin, vwin, w_in_b, w_out_b, wgu_b, wd_b = pl.pallas_call(
        functools.partial(_prompt_body, T, tps, n_tiles),
        grid=(n_tiles + 1,),
        in_specs=[pl.BlockSpec((T, D_MODEL), lambda s: (mix_tile(s), 0)), tab, tab, tab]
        + _weight_specs(wp, in_hbm=("w_in", "w_out", "wgu", "wd")),
        out_specs=[pl.BlockSpec((T, D_MODEL), lambda s: (ffn_tile(s), 0)), win, win, hbm, hbm, hbm, hbm],
        out_shape=[jax.ShapeDtypeStruct((B * S, D_MODEL), F32),
                   jax.ShapeDtypeStruct((B, WINDOW, D_KV), F32),
                   jax.ShapeDtypeStruct((B, WINDOW, D_KV), F32),
                   jax.ShapeDtypeStruct((D_MODEL, D_IN), BF16),
                   jax.ShapeDtypeStruct((D_MODEL, D_MODEL), BF16),
                   jax.ShapeDtypeStruct((D_MODEL, 2 * D_FF), BF16),
                   jax.ShapeDtypeStruct((D_FF, D_MODEL), BF16)],
        scratch_shapes=[pltpu.VMEM((2, N_KV_HEADS, WINDOW, 2 * LANES), BF16),
                        pltpu.VMEM((2, N_KV_HEADS, WINDOW, 2 * LANES), BF16),
                        pltpu.VMEM((2, T, D_MODEL), F32),
                        pltpu.VMEM((D_MODEL, D_IN), BF16),
                        pltpu.VMEM((D_MODEL, D_MODEL), BF16),
                        pltpu.VMEM((D_MODEL, 2 * D_FF), BF16),
                        pltpu.VMEM((D_FF, D_MODEL), BF16),
                        pltpu.VMEM((W_SLOTS, W_IN_ROWS, D_IN), F32),
                        pltpu.VMEM((W_SLOTS, WGU_ROWS, 2 * D_FF), F32),
                        pltpu.VMEM((W_SLOTS, WD_ROWS, D_MODEL), F32),
                        pltpu.SemaphoreType.DMA((W_SLOTS,)),
                        pltpu.SemaphoreType.DMA((W_SLOTS,)),
                        pltpu.SemaphoreType.DMA((W_SLOTS,)),
                        pltpu.SemaphoreType.DMA((4,))],
        compiler_params=pltpu.CompilerParams(dimension_semantics=("arbitrary",), vmem_limit_bytes=VMEM_LIMIT),
        name="layer_prompt",
    )(x_prompt.reshape(B * S, D_MODEL), *tables, *wp)
    y_p = y_p.reshape(B, S, D_MODEL)

    NS = SAMPLE_SEQS
    Ts = NS * L
    ws = weights(w_in_b, w_out_b, wgu_b, wd_b)
    tables_s = _rope_tables(PAST_LEN + np.arange(L), reps=NS)
    n_st = Bd // NS
    mix_rows = lambda n, width: pl.BlockSpec((n, width), lambda s: (jnp.minimum(s, n_st - 1), 0))
    cache_rows = NS * D_KV
    cache_t = lambda a: jnp.transpose(a[0], (0, 2, 3, 1)).reshape(Bd * D_KV, WINDOW)
    y_s, k_s, v_s, vm_s = pl.pallas_call(
        functools.partial(_sample_body, n_st),
        grid=(n_st + 1,),
        in_specs=[mix_rows(Ts, D_MODEL)] + [_const_spec(t.shape) for t in tables_s]
        + [mix_rows(cache_rows, WINDOW), mix_rows(cache_rows, WINDOW)] + _weight_specs(ws, in_hbm=("wgu", "wd")),
        out_specs=[pl.BlockSpec((Ts, D_MODEL), lambda s: (jnp.maximum(s - 1, 0), 0)),
                   mix_rows(Ts * N_KV_HEADS, HEAD_DIM), mix_rows(Ts * N_KV_HEADS, HEAD_DIM),
                   mix_rows(Ts * CMLP_GROUPS, CMLP_GROUP_DIM)],
        out_shape=[jax.ShapeDtypeStruct((Bd * L, D_MODEL), F32),
                   jax.ShapeDtypeStruct((Bd * L * N_KV_HEADS, HEAD_DIM), F32),
                   jax.ShapeDtypeStruct((Bd * L * N_KV_HEADS, HEAD_DIM), F32),
                   jax.ShapeDtypeStruct((Bd * L * CMLP_GROUPS, CMLP_GROUP_DIM), F32)],
        scratch_shapes=[pltpu.VMEM((2, Ts, D_MODEL), F32),
                        pltpu.VMEM((D_MODEL, 2 * D_FF), BF16),
                        pltpu.VMEM((D_FF, D_MODEL), BF16),
                        pltpu.SemaphoreType.DMA((2,))],
        compiler_params=pltpu.CompilerParams(dimension_semantics=("arbitrary",), vmem_limit_bytes=VMEM_LIMIT),
        name="layer_sample",
    )(x_sample.reshape(Bd * L, D_MODEL), *tables_s,
      cache_t(cache_win_k), cache_t(cache_win_v), *ws)

    win_result = lambda a: jnp.transpose(a.reshape(B, N_KV_HEADS, HEAD_DIM, WINDOW), (0, 3, 1, 2))[None]
    return (y_p, y_s.reshape(Bd, L, D_MODEL),
            win_result(kwin), win_result(vwin),
            k_s.reshape(1, Bd, L, N_KV_HEADS, HEAD_DIM),
            v_s.reshape(1, Bd, L, N_KV_HEADS, HEAD_DIM),
            vm_s.reshape(1, Bd, L, CMLP_GROUPS, CMLP_GROUP_DIM))
```

```python
import collections
import functools

import jax
import jax.numpy as jnp
import numpy as np
from jax import lax
from jax.experimental import pallas as pl
from jax.experimental.pallas import tpu as pltpu

D_MODEL = 1024
CHUNK = 64
HEAD_DIM = 64
D_ATTN = 512
D_CMLP = 512
N_HEADS = 8
N_KV_HEADS = 2
GQA_GROUP = 4
D_KV = 128
WINDOW = 128
ROPE_DIM = 16
ROPE_THETA = 500000.0
CMLP_BLOCK = 128
CMLP_GROUPS = 8
CMLP_GROUP_DIM = 64
D_IN = 1792
D_FF = 2816
PAST_LEN = 1024
ALPHA = 2.0 ** 0.25
LN_EPS = 1e-5
NEG_INF = -1e30
LANES = 128
FF_CHUNK = 256
VMEM_LIMIT = 56 * 1024 * 1024
CAST_STEPS = 4
PROMPT_TILE = 512
SAMPLE_SEQS = 8
WGU_ROWS, WD_ROWS = 16, 64
W_SLOTS = 8
W_CHUNKS_PER_PHASE = 8

F32 = jnp.float32
BF16 = jnp.bfloat16


def _layernorm(x, g, b):
    mu = jnp.mean(x, -1, keepdims=True)
    d = x - mu
    var = jnp.mean(d * d, -1, keepdims=True)
    return d * lax.rsqrt(var + LN_EPS) * g + b


def _rmsnorm(x, g):
    ms = jnp.mean(x * x, -1, keepdims=True)
    return x * lax.rsqrt(ms + LN_EPS) * g


def _dot(a, b):
    return jnp.dot(a, b, preferred_element_type=F32)


def _rope(x, c, sa, sb):
    out = []
    for i in range(x.shape[1] // LANES):
        s = x[:, LANES * i:LANES * (i + 1)]
        up = pltpu.roll(s, LANES - ROPE_DIM // 2, 1)
        dn = pltpu.roll(s, ROPE_DIM // 2, 1)
        out.append(s * c + up * sa + dn * sb)
    return out[0] if len(out) == 1 else jnp.concatenate(out, axis=1)


def _row_halves(n):
    return (slice(0, n // 2), slice(n // 2, n))


def _rep4(a):
    ar = pltpu.roll(a, HEAD_DIM, 1)
    first = lax.broadcasted_iota(jnp.int32, a.shape, 1) < HEAD_DIM
    g0 = jnp.where(first, a, ar).astype(BF16)
    g1 = jnp.where(first, ar, a).astype(BF16)
    return [jnp.concatenate([g0, g0], axis=1), jnp.concatenate([g1, g1], axis=1)]


def _attend_windows(windows, first_valid, sinks, q, k, v):
    T = q.shape[0]
    k_rep, v_rep = _rep4(k), _rep4(v)
    head_lane = lax.broadcasted_iota(jnp.int32, (CHUNK, GQA_GROUP * HEAD_DIM), 1) // HEAD_DIM
    key_idx = lax.broadcasted_iota(jnp.int32, (GQA_GROUP * CHUNK, WINDOW + CHUNK), 1)
    ao_rows = []
    for ci in range(T // CHUNK):
        ao_g = []
        for g in range(N_KV_HEADS):
            qc = q[CHUNK * ci:CHUNK * (ci + 1), 256 * g:256 * (g + 1)]
            qm = jnp.concatenate([jnp.where(head_lane == h, qc, 0.0) for h in range(GQA_GROUP)], axis=0).astype(BF16)
            kw, vw = windows(k_rep, v_rep, g, ci)
            s = lax.dot_general(qm, kw, (((1,), (1,)), ((), ())), preferred_element_type=F32)
            fv = first_valid(ci)
            if fv is not None:
                s = jnp.where(key_idx >= fv, s, NEG_INF)
            ps = []
            for h in range(GQA_GROUP):
                sh = s[CHUNK * h:CHUNK * (h + 1)]
                sink = sinks[GQA_GROUP * g + h]
                m = jnp.maximum(jnp.max(sh, -1, keepdims=True), sink)
                p = jnp.exp(sh - m)
                denom = jnp.sum(p, -1, keepdims=True) + jnp.exp(sink - m)
                ps.append(p * (1.0 / denom))
            pm = jnp.concatenate(ps, axis=0).astype(BF16)
            r = _dot(pm, vw)
            o = jnp.where(head_lane == 0, r[0:CHUNK], 0.0)
            for h in range(1, GQA_GROUP):
                o = o + jnp.where(head_lane == h, r[CHUNK * h:CHUNK * (h + 1)], 0.0)
            ao_g.append(o)
        ao_rows.append(jnp.concatenate(ao_g, axis=1))
        yield
    return jnp.concatenate(ao_rows, axis=0), (k_rep, v_rep)


def _mixer(x, c, sa, sb, attend, LB, w):
    T = x.shape[0]
    nb = T // LB
    xb = x.astype(BF16)
    uv = jax.nn.gelu(_dot(xb, w.w_in[:, D_ATTN + 2 * D_KV:D_IN]))
    yield
    q = _dot(xb, w.w_in[:, 0:D_ATTN])
    kv = _dot(xb, w.w_in[:, D_ATTN:D_ATTN + 2 * D_KV])

    k = _rope(kv[:, :D_KV], c, sa, sb)
    v = kv[:, D_KV:]
    q = _rope(q, c, sa, sb) * (HEAD_DIM ** -0.5)

    yield
    u = uv[:, :D_CMLP]
    vm = _layernorm(uv[:, D_CMLP:], w.lnv_g[...], w.lnv_b[...])

    lo_mask = lax.broadcasted_iota(jnp.int32, (T, LANES), 1) < CMLP_GROUP_DIM
    pad = [jnp.zeros((CMLP_BLOCK - LB, LANES), BF16)] if LB < CMLP_BLOCK else []
    gate_slabs = []
    for p in range(D_CMLP // LANES):
        slab = vm[:, LANES * p:LANES * (p + 1)]
        lo = jnp.where(lo_mask, slab, 0.0).astype(BF16)
        hi = jnp.where(lo_mask, 0.0, slab).astype(BF16)
        rhs = jnp.concatenate(
            [jnp.concatenate([lo[LB * b:LB * (b + 1)]] + pad + [hi[LB * b:LB * (b + 1)]] + pad, axis=0)
             for b in range(nb)], axis=1)
        o = _dot(w.wpair[p, 0:LB, :], rhs)
        gate_slabs.append(jnp.concatenate([o[:, LANES * b:LANES * (b + 1)] for b in range(nb)], axis=0))
    s_gate = jnp.concatenate(gate_slabs, axis=1) + jnp.concatenate([w.bias[0:LB, :]] * nb, axis=0)
    co = u * s_gate
    yield

    ao, carry = yield from attend(q, k, v)

    an = _rmsnorm(ao, w.nga[...]).astype(BF16)
    cn = _rmsnorm(co, w.ngc[...]).astype(BF16)
    h1 = []
    for r in _row_halves(T):
        mix = _dot(an[r], w.w_out[0:D_ATTN, :]) + _dot(cn[r], w.w_out[D_ATTN:, :])
        yield
        h1.append(_layernorm(ALPHA * x[r] + mix, w.ln1g[...], w.ln1b[...]))
    return jnp.concatenate(h1, axis=0), k, v, vm, carry


def _ffn(h1, w):
    hb = h1.astype(BF16)
    acts = []
    for lo_c in range(0, D_FF, FF_CHUNK):
        hi_c = min(lo_c + FF_CHUNK, D_FF)
        gt = _dot(hb, w.wgu[:, lo_c:hi_c])
        up = _dot(hb, w.wgu[:, D_FF + lo_c:D_FF + hi_c])
        acts.append((gt * jax.nn.sigmoid(gt) * up).astype(BF16))
        yield
    act = jnp.concatenate(acts, axis=1)
    y = []
    for r in _row_halves(h1.shape[0]):
        f = _dot(act[r], w.wd[...])
        yield
        y.append(_layernorm(ALPHA * h1[r] + f, w.ln2g[...], w.ln2b[...]))
    return jnp.concatenate(y, axis=0)


def _run(*gens):
    results = [None] * len(gens)
    live = list(range(len(gens)))
    while live:
        for i in list(live):
            try:
                next(gens[i])
            except StopIteration as done:
                results[i] = done.value
                live.remove(i)
    return results


_WEIGHT_NAMES = ("w_in", "lnv_g", "lnv_b", "sinks", "wpair", "bias", "nga", "ngc", "w_out", "ln1g", "ln1b",
                 "wgu", "wd", "ln2g", "ln2b")
_Weights = collections.namedtuple("_Weights", _WEIGHT_NAMES)
N_W = len(_WEIGHT_NAMES)


def _pipelined_step(step, n_tiles, h1_ref, w, make_mixer, store_y, store_mixer, fill_extra=None):
    cur = step % 2
    prev = 1 - cur

    def phase(do_mixer, do_ffn, extra=None):
        gens = (([_ffn(h1_ref[prev], w)] if do_ffn else []) + ([extra()] if extra else [])
                + ([make_mixer()] if do_mixer else []))
        res = _run(*gens)
        if do_ffn:
            store_y(res[0])
        if do_mixer:
            h1_ref[cur] = res[-1][0]
            store_mixer(*res[-1][1:])

    pl.when(step == 0)(functools.partial(phase, True, False, fill_extra))
    pl.when(jnp.logical_and(step > 0, step < n_tiles))(functools.partial(phase, True, True))
    pl.when(step == n_tiles)(functools.partial(phase, False, True))


def _load_convert(jobs, then):
    plan = [(job, c) for job in jobs for c in range(job[0].shape[0] // job[4])]
    depth = W_SLOTS - 1

    def copy(i):
        (src, _, stage, sem, rows), c = plan[i]
        return pltpu.make_async_copy(src.at[pl.ds(c * rows, rows), :], stage.at[c % W_SLOTS], sem.at[c % W_SLOTS])

    for i in range(min(depth, len(plan))):
        copy(i).start(priority=i % 2)
    for i, ((_, dst, stage, _, rows), c) in enumerate(plan):
        copy(i).wait()
        dst[c * rows:(c + 1) * rows, :] = stage[c % W_SLOTS].astype(BF16)
        if i + depth < len(plan):
            copy(i + depth).start(priority=(i + depth) % 2)
        if (i + 1) % W_CHUNKS_PER_PHASE == 0:
            yield
    then()


def _prompt_body(T, tiles_per_seq, n_tiles, *refs):
    x_ref, c_ref, sa_ref, sb_ref = refs[:4]
    w = _Weights(*refs[4:4 + N_W])
    (y_ref, k_out_ref, v_out_ref, wgu_out, wd_out, hk_ref, hv_ref, h1_ref,
     wgu_v, wd_v, stage_gu, stage_d, sem_gu, sem_d, sem_out) = refs[4 + N_W:]
    step = pl.program_id(0)

    out_copies = (pltpu.make_async_copy(wgu_v, wgu_out, sem_out.at[0]),
                  pltpu.make_async_copy(wd_v, wd_out, sem_out.at[1]))

    def start_out_copies():
        for cp in out_copies:
            cp.start()

    def convert_weights(w_hbm=w):
        jobs = [(w_hbm.wgu, wgu_v, stage_gu, sem_gu, WGU_ROWS), (w_hbm.wd, wd_v, stage_d, sem_d, WD_ROWS)]
        return _load_convert(jobs, start_out_copies)

    @pl.when(step == n_tiles)
    def _():
        for cp in out_copies:
            cp.wait()

    w = w._replace(wgu=wgu_v, wd=wd_v)
    j = step % tiles_per_seq
    rd = j % 2
    wr = 1 - rd

    @pl.when(j == 0)
    def _():
        hk_ref[0] = jnp.zeros(hk_ref.shape[1:], BF16)
        hv_ref[0] = jnp.zeros(hv_ref.shape[1:], BF16)

    def first_valid(ci):
        if ci >= WINDOW // CHUNK:
            return None
        return jnp.where(j == 0, (WINDOW // CHUNK - ci) * CHUNK, 0)

    def make_mixer():
        hist_k = [hk_ref[rd, g] for g in range(N_KV_HEADS)]
        hist_v = [hv_ref[rd, g] for g in range(N_KV_HEADS)]

        def windows(k_rep, v_rep, g, ci):
            k_all = jnp.concatenate([hist_k[g], k_rep[g]], axis=0)
            v_all = jnp.concatenate([hist_v[g], v_rep[g]], axis=0)
            lo = CHUNK * ci
            return k_all[lo:lo + WINDOW + CHUNK], v_all[lo:lo + WINDOW + CHUNK]

        attend = functools.partial(_attend_windows, windows, first_valid, w.sinks)
        rows = pl.ds(pl.multiple_of(j * T, T), T)
        return _mixer(x_ref[...], c_ref[rows, :], sa_ref[rows, :], sb_ref[rows, :], attend, CMLP_BLOCK, w)

    def store_y(y):
        y_ref[...] = y

    def store_mixer(k, v, _, carry):
        k_out_ref[0] = jnp.transpose(k[T - WINDOW:, :])
        v_out_ref[0] = jnp.transpose(v[T - WINDOW:, :])
        k_rep, v_rep = carry
        for g in range(N_KV_HEADS):
            hk_ref[wr, g] = k_rep[g][T - WINDOW:]
            hv_ref[wr, g] = v_rep[g][T - WINDOW:]

    _pipelined_step(step, n_tiles, h1_ref, w, make_mixer, store_y, store_mixer, fill_extra=convert_weights)


def _scatter_heads(out_ref, a):
    n = a.shape[1] // HEAD_DIM
    for i in range(n):
        out_ref[pl.ds(i, a.shape[0], stride=n), :] = a[:, HEAD_DIM * i:HEAD_DIM * (i + 1)]


def _cache_rep4(ref):
    rows = [jnp.transpose(ref[D_KV * s:D_KV * (s + 1), :]) for s in range(ref.shape[0] // D_KV)]
    return _rep4(jnp.concatenate(rows, axis=0))


def _sample_body(n_tiles, *refs):
    x_ref, c_ref, sa_ref, sb_ref, ck_ref, cv_ref = refs[:6]
    w = _Weights(*refs[6:6 + N_W])
    y_ref, k_out_ref, v_out_ref, vm_out_ref, h1_ref, wgu_v, wd_v, sem = refs[6 + N_W:]
    step = pl.program_id(0)

    fetches = (pltpu.make_async_copy(w.wgu, wgu_v, sem.at[0]), pltpu.make_async_copy(w.wd, wd_v, sem.at[1]))

    @pl.when(step == 0)
    def _():
        for cp in fetches:
            cp.start()

    @pl.when(step == 1)
    def _():
        for cp in fetches:
            cp.wait()

    w = w._replace(wgu=wgu_v, wd=wd_v)

    def make_mixer():
        ck_rep, cv_rep = _cache_rep4(ck_ref), _cache_rep4(cv_ref)

        def windows(k_rep, v_rep, g, ci):
            cat = lambda cache, new: jnp.concatenate(
                [cache[g][WINDOW * ci:WINDOW * (ci + 1)], new[g][CHUNK * ci:CHUNK * (ci + 1)]], axis=0)
            return cat(ck_rep, k_rep), cat(cv_rep, v_rep)

        attend = functools.partial(_attend_windows, windows, lambda ci: None, w.sinks)
        return _mixer(x_ref[...], c_ref[...], sa_ref[...], sb_ref[...], attend, CHUNK, w)

    def store_y(y):
        y_ref[...] = y

    def store_mixer(k, v, vm, *_):
        _scatter_heads(k_out_ref, k)
        _scatter_heads(v_out_ref, v)
        _scatter_heads(vm_out_ref, vm)

    _pipelined_step(step, n_tiles, h1_ref, w, make_mixer, store_y, store_mixer)


def _cast_body(*refs):
    n = len(refs) // 2
    for src, dst in zip(refs[:n], refs[n:]):
        dst[...] = src[...].astype(BF16)


def _to_bf16(*ws):
    specs = [pl.BlockSpec((a.shape[0] // CAST_STEPS, a.shape[1]), lambda i: (i, 0)) for a in ws]
    return pl.pallas_call(
        _cast_body,
        grid=(CAST_STEPS,),
        in_specs=specs,
        out_specs=specs,
        out_shape=[jax.ShapeDtypeStruct(a.shape, BF16) for a in ws],
        compiler_params=pltpu.CompilerParams(dimension_semantics=("arbitrary",), vmem_limit_bytes=VMEM_LIMIT),
        name="cast_weights",
    )(*ws)


def _const_spec(shape):
    nd = len(shape)
    return pl.BlockSpec(shape, lambda *_: (0,) * nd)


def _weight_specs(ws, in_hbm=()):
    spec = lambda name, a: (pl.BlockSpec(memory_space=pltpu.SMEM) if name == "sinks" else
                            pl.BlockSpec(memory_space=pl.ANY) if name in in_hbm else _const_spec(a.shape))
    return [spec(name, a) for name, a in zip(_WEIGHT_NAMES, ws)]


def _rope_tables(pos, reps=1):
    half = ROPE_DIM // 2
    inv = np.power(ROPE_THETA, -np.arange(half, dtype=np.float64) * (2.0 / ROPE_DIM))
    ang = np.asarray(pos, np.float64)[:, None] * inv[None, :]
    cos, sin = np.cos(ang), np.sin(ang)
    n = ang.shape[0]
    pad = HEAD_DIM - ROPE_DIM
    c = np.concatenate([cos, cos, np.ones((n, pad))], 1)
    sa = np.concatenate([-sin, np.zeros((n, half + pad))], 1)
    sb = np.concatenate([np.zeros((n, half)), sin, np.zeros((n, pad))], 1)
    return [jnp.asarray(np.tile(t, (reps, LANES // HEAD_DIM)), F32) for t in (c, sa, sb)]


def _gate_params(w_s, b_s, lb):
    i = jnp.arange(lb)
    mask = (i[None, :] // CHUNK) <= (i[:, None] // CHUNK)
    w = jnp.where(mask[None], w_s[:, :lb, :lb], 0.0)
    wpair = jnp.concatenate([w[0::2], w[1::2]], axis=2).astype(BF16)
    bias = jnp.repeat(jnp.transpose(b_s[:, :lb]), CMLP_GROUP_DIM, axis=1)
    return wpair, bias


def kernel(x_prompt, x_sample, cache_win_k, cache_win_v, w_in, ln_v_g, ln_v_b, attn_sinks, w_spatial, b_spatial,
           norm_attn_g, norm_cmlp_g, w_out, ln1_g, ln1_b, w_gate_up, w_down, ln2_g, ln2_b):
    B, S, _ = x_prompt.shape
    Bd, L, _ = x_sample.shape
    assert cache_win_k.shape[2] == WINDOW and L == CHUNK and w_in.shape[0] == 1

    row = lambda a: a[0].reshape(1, -1)
    w_in_b, w_out_b = _to_bf16(w_in[0], w_out[0])

    wpair, bias = _gate_params(w_spatial[0], b_spatial[0], CMLP_BLOCK)

    def weights(wgu, wd):
        return _Weights(w_in=w_in_b, lnv_g=row(ln_v_g), lnv_b=row(ln_v_b), sinks=attn_sinks[0],
                        wpair=wpair, bias=bias, nga=row(norm_attn_g), ngc=row(norm_cmlp_g),
                        w_out=w_out_b, ln1g=row(ln1_g), ln1b=row(ln1_b),
                        wgu=wgu, wd=wd, ln2g=row(ln2_g), ln2b=row(ln2_b))

    T = PROMPT_TILE
    tps = S // T
    n_tiles = B * tps
    wp = weights(w_gate_up[0], w_down[0])
    tables = _rope_tables(np.arange(S))
    mix_tile = lambda s: jnp.minimum(s, n_tiles - 1)
    ffn_tile = lambda s: jnp.maximum(s - 1, 0)
    tab = _const_spec((S, LANES))
    win = pl.BlockSpec((1, WINDOW, D_KV), lambda s: (mix_tile(s) // tps, 0, 0))
    hbm = pl.BlockSpec(memory_space=pl.ANY)
    y_p, kwin, vwin, wgu_b, wd_b = pl.pallas_call(
        functools.partial(_prompt_body, T, tps, n_tiles),
        grid=(n_tiles + 1,),
        in_specs=[pl.BlockSpec((T, D_MODEL), lambda s: (mix_tile(s), 0)), tab, tab, tab]
        + _weight_specs(wp, in_hbm=("wgu", "wd")),
        out_specs=[pl.BlockSpec((T, D_MODEL), lambda s: (ffn_tile(s), 0)), win, win, hbm, hbm],
        out_shape=[jax.ShapeDtypeStruct((B * S, D_MODEL), F32),
                   jax.ShapeDtypeStruct((B, WINDOW, D_KV), F32),
                   jax.ShapeDtypeStruct((B, WINDOW, D_KV), F32),
                   jax.ShapeDtypeStruct((D_MODEL, 2 * D_FF), BF16),
                   jax.ShapeDtypeStruct((D_FF, D_MODEL), BF16)],
        scratch_shapes=[pltpu.VMEM((2, N_KV_HEADS, WINDOW, 2 * LANES), BF16),
                        pltpu.VMEM((2, N_KV_HEADS, WINDOW, 2 * LANES), BF16),
                        pltpu.VMEM((2, T, D_MODEL), F32),
                        pltpu.VMEM((D_MODEL, 2 * D_FF), BF16),
                        pltpu.VMEM((D_FF, D_MODEL), BF16),
                        pltpu.VMEM((W_SLOTS, WGU_ROWS, 2 * D_FF), F32),
                        pltpu.VMEM((W_SLOTS, WD_ROWS, D_MODEL), F32),
                        pltpu.SemaphoreType.DMA((W_SLOTS,)),
                        pltpu.SemaphoreType.DMA((W_SLOTS,)),
                        pltpu.SemaphoreType.DMA((2,))],
        compiler_params=pltpu.CompilerParams(dimension_semantics=("arbitrary",), vmem_limit_bytes=VMEM_LIMIT),
        name="layer_prompt",
    )(x_prompt.reshape(B * S, D_MODEL), *tables, *wp)
    y_p = y_p.reshape(B, S, D_MODEL)

    NS = SAMPLE_SEQS
    Ts = NS * L
    ws = weights(wgu_b, wd_b)
    tables_s = _rope_tables(PAST_LEN + np.arange(L), reps=NS)
    n_st = Bd // NS
    mix_rows = lambda n, width: pl.BlockSpec((n, width), lambda s: (jnp.minimum(s, n_st - 1), 0))
    cache_rows = NS * D_KV
    cache_t = lambda a: jnp.transpose(a[0], (0, 2, 3, 1)).reshape(Bd * D_KV, WINDOW)
    y_s, k_s, v_s, vm_s = pl.pallas_call(
        functools.partial(_sample_body, n_st),
        grid=(n_st + 1,),
        in_specs=[mix_rows(Ts, D_MODEL)] + [_const_spec(t.shape) for t in tables_s]
        + [mix_rows(cache_rows, WINDOW), mix_rows(cache_rows, WINDOW)] + _weight_specs(ws, in_hbm=("wgu", "wd")),
        out_specs=[pl.BlockSpec((Ts, D_MODEL), lambda s: (jnp.maximum(s - 1, 0), 0)),
                   mix_rows(Ts * N_KV_HEADS, HEAD_DIM), mix_rows(Ts * N_KV_HEADS, HEAD_DIM),
                   mix_rows(Ts * CMLP_GROUPS, CMLP_GROUP_DIM)],
        out_shape=[jax.ShapeDtypeStruct((Bd * L, D_MODEL), F32),
                   jax.ShapeDtypeStruct((Bd * L * N_KV_HEADS, HEAD_DIM), F32),
                   jax.ShapeDtypeStruct((Bd * L * N_KV_HEADS, HEAD_DIM), F32),
                   jax.ShapeDtypeStruct((Bd * L * CMLP_GROUPS, CMLP_GROUP_DIM), F32)],
        scratch_shapes=[pltpu.VMEM((2, Ts, D_MODEL), F32),
                        pltpu.VMEM((D_MODEL, 2 * D_FF), BF16),
                        pltpu.VMEM((D_FF, D_MODEL), BF16),
                        pltpu.SemaphoreType.DMA((2,))],
        compiler_params=pltpu.CompilerParams(dimension_semantics=("arbitrary",), vmem_limit_bytes=VMEM_LIMIT),
        name="layer_sample",
    )(x_sample.reshape(Bd * L, D_MODEL), *tables_s,
      cache_t(cache_win_k), cache_t(cache_win_v), *ws)

    win_result = lambda a: jnp.transpose(a.reshape(B, N_KV_HEADS, HEAD_DIM, WINDOW), (0, 3, 1, 2))[None]
    return (y_p, y_s.reshape(Bd, L, D_MODEL),
            win_result(kwin), win_result(vwin),
            k_s.reshape(1, Bd, L, N_KV_HEADS, HEAD_DIM),
            v_s.reshape(1, Bd, L, N_KV_HEADS, HEAD_DIM),
            vm_s.reshape(1, Bd, L, CMLP_GROUPS, CMLP_GROUP_DIM))
```

```python
import collections
import functools

import jax
import jax.numpy as jnp
import numpy as np
from jax import lax
from jax.experimental import pallas as pl
from jax.experimental.pallas import tpu as pltpu

D_MODEL = 1024
CHUNK = 64
HEAD_DIM = 64
D_ATTN = 512
D_CMLP = 512
N_HEADS = 8
N_KV_HEADS = 2
GQA_GROUP = 4
D_KV = 128
WINDOW = 128
ROPE_DIM = 16
ROPE_THETA = 500000.0
CMLP_BLOCK = 128
CMLP_GROUPS = 8
CMLP_GROUP_DIM = 64
D_IN = 1792
D_FF = 2816
PAST_LEN = 1024
ALPHA = 2.0 ** 0.25
LN_EPS = 1e-5
NEG_INF = -1e30
LANES = 128
FF_CHUNK = 256
VMEM_LIMIT = 56 * 1024 * 1024
CAST_STEPS = 4
PROMPT_TILE = 512
SAMPLE_SEQS = 4
WGU_ROWS, WD_ROWS = 16, 64
W_SLOTS = 8
W_CHUNKS_PER_PHASE = 8

F32 = jnp.float32
BF16 = jnp.bfloat16


def _layernorm(x, g, b):
    mu = jnp.mean(x, -1, keepdims=True)
    d = x - mu
    var = jnp.mean(d * d, -1, keepdims=True)
    return d * lax.rsqrt(var + LN_EPS) * g + b


def _rmsnorm(x, g):
    ms = jnp.mean(x * x, -1, keepdims=True)
    return x * lax.rsqrt(ms + LN_EPS) * g


def _dot(a, b):
    return jnp.dot(a, b, preferred_element_type=F32)


def _rope(x, c, sa, sb):
    out = []
    for i in range(x.shape[1] // LANES):
        s = x[:, LANES * i:LANES * (i + 1)]
        up = pltpu.roll(s, LANES - ROPE_DIM // 2, 1)
        dn = pltpu.roll(s, ROPE_DIM // 2, 1)
        out.append(s * c + up * sa + dn * sb)
    return out[0] if len(out) == 1 else jnp.concatenate(out, axis=1)


def _row_halves(n):
    return (slice(0, n // 2), slice(n // 2, n))


def _rep4(a):
    ar = pltpu.roll(a, HEAD_DIM, 1)
    first = lax.broadcasted_iota(jnp.int32, a.shape, 1) < HEAD_DIM
    g0 = jnp.where(first, a, ar).astype(BF16)
    g1 = jnp.where(first, ar, a).astype(BF16)
    return [jnp.concatenate([g0, g0], axis=1), jnp.concatenate([g1, g1], axis=1)]


def _attend_windows(windows, first_valid, sinks, q, k, v):
    T = q.shape[0]
    k_rep, v_rep = _rep4(k), _rep4(v)
    head_lane = lax.broadcasted_iota(jnp.int32, (CHUNK, GQA_GROUP * HEAD_DIM), 1) // HEAD_DIM
    key_idx = lax.broadcasted_iota(jnp.int32, (GQA_GROUP * CHUNK, WINDOW + CHUNK), 1)
    ao_rows = []
    for ci in range(T // CHUNK):
        ao_g = []
        for g in range(N_KV_HEADS):
            qc = q[CHUNK * ci:CHUNK * (ci + 1), 256 * g:256 * (g + 1)]
            qm = jnp.concatenate([jnp.where(head_lane == h, qc, 0.0) for h in range(GQA_GROUP)], axis=0).astype(BF16)
            kw, vw = windows(k_rep, v_rep, g, ci)
            s = lax.dot_general(qm, kw, (((1,), (1,)), ((), ())), preferred_element_type=F32)
            fv = first_valid(ci)
            if fv is not None:
                s = jnp.where(key_idx >= fv, s, NEG_INF)
            ps = []
            for h in range(GQA_GROUP):
                sh = s[CHUNK * h:CHUNK * (h + 1)]
                sink = sinks[GQA_GROUP * g + h]
                m = jnp.maximum(jnp.max(sh, -1, keepdims=True), sink)
                p = jnp.exp(sh - m)
                denom = jnp.sum(p, -1, keepdims=True) + jnp.exp(sink - m)
                ps.append(p * (1.0 / denom))
            pm = jnp.concatenate(ps, axis=0).astype(BF16)
            r = _dot(pm, vw)
            o = jnp.where(head_lane == 0, r[0:CHUNK], 0.0)
            for h in range(1, GQA_GROUP):
                o = o + jnp.where(head_lane == h, r[CHUNK * h:CHUNK * (h + 1)], 0.0)
            ao_g.append(o)
        ao_rows.append(jnp.concatenate(ao_g, axis=1))
        yield
    return jnp.concatenate(ao_rows, axis=0), (k_rep, v_rep)


def _mixer(x, c, sa, sb, attend, LB, w):
    T = x.shape[0]
    nb = T // LB
    xb = x.astype(BF16)
    uv = jax.nn.gelu(_dot(xb, w.w_in[:, D_ATTN + 2 * D_KV:D_IN]))
    yield
    q = _dot(xb, w.w_in[:, 0:D_ATTN])
    kv = _dot(xb, w.w_in[:, D_ATTN:D_ATTN + 2 * D_KV])

    k = _rope(kv[:, :D_KV], c, sa, sb)
    v = kv[:, D_KV:]
    q = _rope(q, c, sa, sb) * (HEAD_DIM ** -0.5)

    yield
    u = uv[:, :D_CMLP]
    vm = _layernorm(uv[:, D_CMLP:], w.lnv_g[...], w.lnv_b[...])

    lo_mask = lax.broadcasted_iota(jnp.int32, (T, LANES), 1) < CMLP_GROUP_DIM
    pad = [jnp.zeros((CMLP_BLOCK - LB, LANES), BF16)] if LB < CMLP_BLOCK else []
    gate_slabs = []
    for p in range(D_CMLP // LANES):
        slab = vm[:, LANES * p:LANES * (p + 1)]
        lo = jnp.where(lo_mask, slab, 0.0).astype(BF16)
        hi = jnp.where(lo_mask, 0.0, slab).astype(BF16)
        rhs = jnp.concatenate(
            [jnp.concatenate([lo[LB * b:LB * (b + 1)]] + pad + [hi[LB * b:LB * (b + 1)]] + pad, axis=0)
             for b in range(nb)], axis=1)
        o = _dot(w.wpair[p, 0:LB, :], rhs)
        gate_slabs.append(jnp.concatenate([o[:, LANES * b:LANES * (b + 1)] for b in range(nb)], axis=0))
    s_gate = jnp.concatenate(gate_slabs, axis=1) + jnp.concatenate([w.bias[0:LB, :]] * nb, axis=0)
    co = u * s_gate
    yield

    ao, carry = yield from attend(q, k, v)

    an = _rmsnorm(ao, w.nga[...]).astype(BF16)
    cn = _rmsnorm(co, w.ngc[...]).astype(BF16)
    h1 = []
    for r in _row_halves(T):
        mix = _dot(an[r], w.w_out[0:D_ATTN, :]) + _dot(cn[r], w.w_out[D_ATTN:, :])
        yield
        h1.append(_layernorm(ALPHA * x[r] + mix, w.ln1g[...], w.ln1b[...]))
    return jnp.concatenate(h1, axis=0), k, v, vm, carry


def _ffn(h1, w):
    hb = h1.astype(BF16)
    acts = []
    for lo_c in range(0, D_FF, FF_CHUNK):
        hi_c = min(lo_c + FF_CHUNK, D_FF)
        gt = _dot(hb, w.wgu[:, lo_c:hi_c])
        up = _dot(hb, w.wgu[:, D_FF + lo_c:D_FF + hi_c])
        acts.append((gt * jax.nn.sigmoid(gt) * up).astype(BF16))
        yield
    act = jnp.concatenate(acts, axis=1)
    y = []
    for r in _row_halves(h1.shape[0]):
        f = _dot(act[r], w.wd[...])
        yield
        y.append(_layernorm(ALPHA * h1[r] + f, w.ln2g[...], w.ln2b[...]))
    return jnp.concatenate(y, axis=0)


def _run(*gens):
    results = [None] * len(gens)
    live = list(range(len(gens)))
    while live:
        for i in list(live):
            try:
                next(gens[i])
            except StopIteration as done:
                results[i] = done.value
                live.remove(i)
    return results


_WEIGHT_NAMES = ("w_in", "lnv_g", "lnv_b", "sinks", "wpair", "bias", "nga", "ngc", "w_out", "ln1g", "ln1b",
                 "wgu", "wd", "ln2g", "ln2b")
_Weights = collections.namedtuple("_Weights", _WEIGHT_NAMES)
N_W = len(_WEIGHT_NAMES)


def _pipelined_step(step, n_tiles, h1_ref, w, make_mixer, store_y, store_mixer, fill_extra=None):
    cur = step % 2
    prev = 1 - cur

    def phase(do_mixer, do_ffn, extra=None):
        gens = (([_ffn(h1_ref[prev], w)] if do_ffn else []) + ([extra()] if extra else [])
                + ([make_mixer()] if do_mixer else []))
        res = _run(*gens)
        if do_ffn:
            store_y(res[0])
        if do_mixer:
            h1_ref[cur] = res[-1][0]
            store_mixer(*res[-1][1:])

    pl.when(step == 0)(functools.partial(phase, True, False, fill_extra))
    pl.when(jnp.logical_and(step > 0, step < n_tiles))(functools.partial(phase, True, True))
    pl.when(step == n_tiles)(functools.partial(phase, False, True))


def _load_convert(jobs, then):
    plan = [(job, c) for job in jobs for c in range(job[0].shape[0] // job[4])]
    depth = W_SLOTS - 1

    def copy(i):
        (src, _, stage, sem, rows), c = plan[i]
        return pltpu.make_async_copy(src.at[pl.ds(c * rows, rows), :], stage.at[c % W_SLOTS], sem.at[c % W_SLOTS])

    for i in range(min(depth, len(plan))):
        copy(i).start(priority=i % 2)
    for i, ((_, dst, stage, _, rows), c) in enumerate(plan):
        copy(i).wait()
        dst[c * rows:(c + 1) * rows, :] = stage[c % W_SLOTS].astype(BF16)
        if i + depth < len(plan):
            copy(i + depth).start(priority=(i + depth) % 2)
        if (i + 1) % W_CHUNKS_PER_PHASE == 0:
            yield
    then()


def _prompt_body(T, tiles_per_seq, n_tiles, *refs):
    x_ref, c_ref, sa_ref, sb_ref = refs[:4]
    w = _Weights(*refs[4:4 + N_W])
    (y_ref, k_out_ref, v_out_ref, w_out_out, wgu_out, wd_out, hk_ref, hv_ref, h1_ref,
     w_out_v, wgu_v, wd_v, stage_gu, stage_d, sem_gu, sem_d, sem_out) = refs[4 + N_W:]
    step = pl.program_id(0)

    out_copies = (pltpu.make_async_copy(w_out_v, w_out_out, sem_out.at[0]),
                  pltpu.make_async_copy(wgu_v, wgu_out, sem_out.at[1]),
                  pltpu.make_async_copy(wd_v, wd_out, sem_out.at[2]))

    def start_out_copies():
        for cp in out_copies:
            cp.start()

    def convert_weights(w_hbm=w):
        jobs = [(w_hbm.w_out, w_out_v, stage_d, sem_d, WD_ROWS), (w_hbm.wgu, wgu_v, stage_gu, sem_gu, WGU_ROWS),
                (w_hbm.wd, wd_v, stage_d, sem_d, WD_ROWS)]
        return _load_convert(jobs, start_out_copies)

    @pl.when(step == n_tiles)
    def _():
        for cp in out_copies:
            cp.wait()

    w = w._replace(w_out=w_out_v, wgu=wgu_v, wd=wd_v)
    j = step % tiles_per_seq
    rd = j % 2
    wr = 1 - rd

    @pl.when(j == 0)
    def _():
        hk_ref[0] = jnp.zeros(hk_ref.shape[1:], BF16)
        hv_ref[0] = jnp.zeros(hv_ref.shape[1:], BF16)

    def first_valid(ci):
        if ci >= WINDOW // CHUNK:
            return None
        return jnp.where(j == 0, (WINDOW // CHUNK - ci) * CHUNK, 0)

    def make_mixer():
        hist_k = [hk_ref[rd, g] for g in range(N_KV_HEADS)]
        hist_v = [hv_ref[rd, g] for g in range(N_KV_HEADS)]

        def windows(k_rep, v_rep, g, ci):
            k_all = jnp.concatenate([hist_k[g], k_rep[g]], axis=0)
            v_all = jnp.concatenate([hist_v[g], v_rep[g]], axis=0)
            lo = CHUNK * ci
            return k_all[lo:lo + WINDOW + CHUNK], v_all[lo:lo + WINDOW + CHUNK]

        attend = functools.partial(_attend_windows, windows, first_valid, w.sinks)
        return _mixer(x_ref[...], c_ref[...], sa_ref[...], sb_ref[...], attend, CMLP_BLOCK, w)

    def store_y(y):
        y_ref[...] = y

    def store_mixer(k, v, _, carry):
        k_out_ref[0] = jnp.transpose(k[T - WINDOW:, :])
        v_out_ref[0] = jnp.transpose(v[T - WINDOW:, :])
        k_rep, v_rep = carry
        for g in range(N_KV_HEADS):
            hk_ref[wr, g] = k_rep[g][T - WINDOW:]
            hv_ref[wr, g] = v_rep[g][T - WINDOW:]

    _pipelined_step(step, n_tiles, h1_ref, w, make_mixer, store_y, store_mixer, fill_extra=convert_weights)


def _scatter_heads(out_ref, a):
    n = a.shape[1] // HEAD_DIM
    for i in range(n):
        out_ref[pl.ds(i, a.shape[0], stride=n), :] = a[:, HEAD_DIM * i:HEAD_DIM * (i + 1)]


def _cache_rep4(ref):
    rows = [jnp.transpose(ref[D_KV * s:D_KV * (s + 1), :]) for s in range(ref.shape[0] // D_KV)]
    return _rep4(jnp.concatenate(rows, axis=0))


def _sample_body(n_tiles, *refs):
    x_ref, c_ref, sa_ref, sb_ref, ck_ref, cv_ref = refs[:6]
    w = _Weights(*refs[6:6 + N_W])
    y_ref, k_out_ref, v_out_ref, vm_out_ref, h1_ref, wgu_v, wd_v, sem = refs[6 + N_W:]
    step = pl.program_id(0)

    fetches = (pltpu.make_async_copy(w.wgu, wgu_v, sem.at[0]), pltpu.make_async_copy(w.wd, wd_v, sem.at[1]))

    @pl.when(step == 0)
    def _():
        for cp in fetches:
            cp.start()

    @pl.when(step == 1)
    def _():
        for cp in fetches:
            cp.wait()

    w = w._replace(wgu=wgu_v, wd=wd_v)

    def make_mixer():
        ck_rep, cv_rep = _cache_rep4(ck_ref), _cache_rep4(cv_ref)

        def windows(k_rep, v_rep, g, ci):
            cat = lambda cache, new: jnp.concatenate(
                [cache[g][WINDOW * ci:WINDOW * (ci + 1)], new[g][CHUNK * ci:CHUNK * (ci + 1)]], axis=0)
            return cat(ck_rep, k_rep), cat(cv_rep, v_rep)

        attend = functools.partial(_attend_windows, windows, lambda ci: None, w.sinks)
        return _mixer(x_ref[...], c_ref[...], sa_ref[...], sb_ref[...], attend, CHUNK, w)

    def store_y(y):
        y_ref[...] = y

    def store_mixer(k, v, vm, *_):
        _scatter_heads(k_out_ref, k)
        _scatter_heads(v_out_ref, v)
        _scatter_heads(vm_out_ref, vm)

    _pipelined_step(step, n_tiles, h1_ref, w, make_mixer, store_y, store_mixer)


def _cast_body(*refs):
    n = len(refs) // 2
    for src, dst in zip(refs[:n], refs[n:]):
        dst[...] = src[...].astype(BF16)


def _to_bf16(*ws):
    specs = [pl.BlockSpec((a.shape[0] // CAST_STEPS, a.shape[1]), lambda i: (i, 0)) for a in ws]
    return pl.pallas_call(
        _cast_body,
        grid=(CAST_STEPS,),
        in_specs=specs,
        out_specs=specs,
        out_shape=[jax.ShapeDtypeStruct(a.shape, BF16) for a in ws],
        compiler_params=pltpu.CompilerParams(dimension_semantics=("arbitrary",), vmem_limit_bytes=VMEM_LIMIT),
        name="cast_weights",
    )(*ws)


def _const_spec(shape):
    nd = len(shape)
    return pl.BlockSpec(shape, lambda *_: (0,) * nd)


def _weight_specs(ws, in_hbm=()):
    spec = lambda name, a: (pl.BlockSpec(memory_space=pltpu.SMEM) if name == "sinks" else
                            pl.BlockSpec(memory_space=pl.ANY) if name in in_hbm else _const_spec(a.shape))
    return [spec(name, a) for name, a in zip(_WEIGHT_NAMES, ws)]


def _rope_tables(pos, reps=1):
    half = ROPE_DIM // 2
    inv = np.power(ROPE_THETA, -np.arange(half, dtype=np.float64) * (2.0 / ROPE_DIM))
    ang = np.asarray(pos, np.float64)[:, None] * inv[None, :]
    cos, sin = np.cos(ang), np.sin(ang)
    n = ang.shape[0]
    pad = HEAD_DIM - ROPE_DIM
    c = np.concatenate([cos, cos, np.ones((n, pad))], 1)
    sa = np.concatenate([-sin, np.zeros((n, half + pad))], 1)
    sb = np.concatenate([np.zeros((n, half)), sin, np.zeros((n, pad))], 1)
    return [jnp.asarray(np.tile(t, (reps, LANES // HEAD_DIM)), F32) for t in (c, sa, sb)]


def _gate_params(w_s, b_s, lb):
    i = jnp.arange(lb)
    mask = (i[None, :] // CHUNK) <= (i[:, None] // CHUNK)
    w = jnp.where(mask[None], w_s[:, :lb, :lb], 0.0)
    wpair = jnp.concatenate([w[0::2], w[1::2]], axis=2).astype(BF16)
    bias = jnp.repeat(jnp.transpose(b_s[:, :lb]), CMLP_GROUP_DIM, axis=1)
    return wpair, bias


def kernel(x_prompt, x_sample, cache_win_k, cache_win_v, w_in, ln_v_g, ln_v_b, attn_sinks, w_spatial, b_spatial,
           norm_attn_g, norm_cmlp_g, w_out, ln1_g, ln1_b, w_gate_up, w_down, ln2_g, ln2_b):
    B, S, _ = x_prompt.shape
    Bd, L, _ = x_sample.shape
    assert cache_win_k.shape[2] == WINDOW and L == CHUNK and w_in.shape[0] == 1

    row = lambda a: a[0].reshape(1, -1)
    w_in_b, = _to_bf16(w_in[0])

    wpair, bias = _gate_params(w_spatial[0], b_spatial[0], CMLP_BLOCK)

    def weights(w_out_, wgu, wd):
        return _Weights(w_in=w_in_b, lnv_g=row(ln_v_g), lnv_b=row(ln_v_b), sinks=attn_sinks[0],
                        wpair=wpair, bias=bias, nga=row(norm_attn_g), ngc=row(norm_cmlp_g),
                        w_out=w_out_, ln1g=row(ln1_g), ln1b=row(ln1_b),
                        wgu=wgu, wd=wd, ln2g=row(ln2_g), ln2b=row(ln2_b))

    T = PROMPT_TILE
    tps = S // T
    n_tiles = B * tps
    wp = weights(w_out[0], w_gate_up[0], w_down[0])
    tables = _rope_tables(np.arange(S))
    mix_tile = lambda s: jnp.minimum(s, n_tiles - 1)
    ffn_tile = lambda s: jnp.maximum(s - 1, 0)
    tab = pl.BlockSpec((T, LANES), lambda s: (mix_tile(s) % tps, 0))
    win = pl.BlockSpec((1, WINDOW, D_KV), lambda s: (mix_tile(s) // tps, 0, 0))
    hbm = pl.BlockSpec(memory_space=pl.ANY)
    y_p, kwin, vwin, w_out_b, wgu_b, wd_b = pl.pallas_call(
        functools.partial(_prompt_body, T, tps, n_tiles),
        grid=(n_tiles + 1,),
        in_specs=[pl.BlockSpec((T, D_MODEL), lambda s: (mix_tile(s), 0)), tab, tab, tab]
        + _weight_specs(wp, in_hbm=("w_out", "wgu", "wd")),
        out_specs=[pl.BlockSpec((T, D_MODEL), lambda s: (ffn_tile(s), 0)), win, win, hbm, hbm, hbm],
        out_shape=[jax.ShapeDtypeStruct((B * S, D_MODEL), F32),
                   jax.ShapeDtypeStruct((B, WINDOW, D_KV), F32),
                   jax.ShapeDtypeStruct((B, WINDOW, D_KV), F32),
                   jax.ShapeDtypeStruct((D_MODEL, D_MODEL), BF16),
                   jax.ShapeDtypeStruct((D_MODEL, 2 * D_FF), BF16),
                   jax.ShapeDtypeStruct((D_FF, D_MODEL), BF16)],
        scratch_shapes=[pltpu.VMEM((2, N_KV_HEADS, WINDOW, 2 * LANES), BF16),
                        pltpu.VMEM((2, N_KV_HEADS, WINDOW, 2 * LANES), BF16),
                        pltpu.VMEM((2, T, D_MODEL), F32),
                        pltpu.VMEM((D_MODEL, D_MODEL), BF16),
                        pltpu.VMEM((D_MODEL, 2 * D_FF), BF16),
                        pltpu.VMEM((D_FF, D_MODEL), BF16),
                        pltpu.VMEM((W_SLOTS, WGU_ROWS, 2 * D_FF), F32),
                        pltpu.VMEM((W_SLOTS, WD_ROWS, D_MODEL), F32),
                        pltpu.SemaphoreType.DMA((W_SLOTS,)),
                        pltpu.SemaphoreType.DMA((W_SLOTS,)),
                        pltpu.SemaphoreType.DMA((3,))],
        compiler_params=pltpu.CompilerParams(dimension_semantics=("arbitrary",), vmem_limit_bytes=VMEM_LIMIT),
        name="layer_prompt",
    )(x_prompt.reshape(B * S, D_MODEL), *tables, *wp)
    y_p = y_p.reshape(B, S, D_MODEL)

    NS = SAMPLE_SEQS
    Ts = NS * L
    ws = weights(w_out_b, wgu_b, wd_b)
    tables_s = _rope_tables(PAST_LEN + np.arange(L), reps=NS)
    n_st = Bd // NS
    mix_rows = lambda n, width: pl.BlockSpec((n, width), lambda s: (jnp.minimum(s, n_st - 1), 0))
    cache_rows = NS * D_KV
    cache_t = lambda a: jnp.transpose(a[0], (0, 2, 3, 1)).reshape(Bd * D_KV, WINDOW)
    y_s, k_s, v_s, vm_s = pl.pallas_call(
        functools.partial(_sample_body, n_st),
        grid=(n_st + 1,),
        in_specs=[mix_rows(Ts, D_MODEL)] + [_const_spec(t.shape) for t in tables_s]
        + [mix_rows(cache_rows, WINDOW), mix_rows(cache_rows, WINDOW)] + _weight_specs(ws, in_hbm=("wgu", "wd")),
        out_specs=[pl.BlockSpec((Ts, D_MODEL), lambda s: (jnp.maximum(s - 1, 0), 0)),
                   mix_rows(Ts * N_KV_HEADS, HEAD_DIM), mix_rows(Ts * N_KV_HEADS, HEAD_DIM),
                   mix_rows(Ts * CMLP_GROUPS, CMLP_GROUP_DIM)],
        out_shape=[jax.ShapeDtypeStruct((Bd * L, D_MODEL), F32),
                   jax.ShapeDtypeStruct((Bd * L * N_KV_HEADS, HEAD_DIM), F32),
                   jax.ShapeDtypeStruct((Bd * L * N_KV_HEADS, HEAD_DIM), F32),
                   jax.ShapeDtypeStruct((Bd * L * CMLP_GROUPS, CMLP_GROUP_DIM), F32)],
        scratch_shapes=[pltpu.VMEM((2, Ts, D_MODEL), F32),
                        pltpu.VMEM((D_MODEL, 2 * D_FF), BF16),
                        pltpu.VMEM((D_FF, D_MODEL), BF16),
                        pltpu.SemaphoreType.DMA((2,))],
        compiler_params=pltpu.CompilerParams(dimension_semantics=("arbitrary",), vmem_limit_bytes=VMEM_LIMIT),
        name="layer_sample",
    )(x_sample.reshape(Bd * L, D_MODEL), *tables_s,
      cache_t(cache_win_k), cache_t(cache_win_v), *ws)

    win_result = lambda a: jnp.transpose(a.reshape(B, N_KV_HEADS, HEAD_DIM, WINDOW), (0, 3, 1, 2))[None]
    return (y_p, y_s.reshape(Bd, L, D_MODEL),
            win_result(kwin), win_result(vwin),
            k_s.reshape(1, Bd, L, N_KV_HEADS, HEAD_DIM),
            v_s.reshape(1, Bd, L, N_KV_HEADS, HEAD_DIM),
            vm_s.reshape(1, Bd, L, CMLP_GROUPS, CMLP_GROUP_DIM))
```

```python
import collections
import functools

import jax
import jax.numpy as jnp
import numpy as np
from jax import lax
from jax.experimental import pallas as pl
from jax.experimental.pallas import tpu as pltpu

D_MODEL = 1024
CHUNK = 64
HEAD_DIM = 64
D_ATTN = 512
D_CMLP = 512
N_HEADS = 8
N_KV_HEADS = 2
GQA_GROUP = 4
D_KV = 128
WINDOW = 128
ROPE_DIM = 16
ROPE_THETA = 500000.0
CMLP_BLOCK = 128
CMLP_GROUPS = 8
CMLP_GROUP_DIM = 64
D_IN = 1792
D_FF = 2816
PAST_LEN = 1024
ALPHA = 2.0 ** 0.25
LN_EPS = 1e-5
NEG_INF = -1e30
LANES = 128
FF_CHUNK = 256
VMEM_LIMIT = 56 * 1024 * 1024
CAST_STEPS = 4
PROMPT_TILE = 512
SAMPLE_SEQS = 4
WGU_ROWS, WD_ROWS = 16, 64
W_SLOTS = 8
W_CHUNKS_PER_PHASE = 8

F32 = jnp.float32
BF16 = jnp.bfloat16


def _layernorm(x, g, b):
    mu = jnp.mean(x, -1, keepdims=True)
    d = x - mu
    var = jnp.mean(d * d, -1, keepdims=True)
    return d * lax.rsqrt(var + LN_EPS) * g + b


def _rmsnorm(x, g):
    ms = jnp.mean(x * x, -1, keepdims=True)
    return x * lax.rsqrt(ms + LN_EPS) * g


def _dot(a, b):
    return jnp.dot(a, b, preferred_element_type=F32)


def _rope(x, c, sa, sb):
    out = []
    for i in range(x.shape[1] // LANES):
        s = x[:, LANES * i:LANES * (i + 1)]
        up = pltpu.roll(s, LANES - ROPE_DIM // 2, 1)
        dn = pltpu.roll(s, ROPE_DIM // 2, 1)
        out.append(s * c + up * sa + dn * sb)
    return out[0] if len(out) == 1 else jnp.concatenate(out, axis=1)


def _row_halves(n):
    return (slice(0, n // 2), slice(n // 2, n))


def _rep4(a):
    ar = pltpu.roll(a, HEAD_DIM, 1)
    first = lax.broadcasted_iota(jnp.int32, a.shape, 1) < HEAD_DIM
    g0 = jnp.where(first, a, ar).astype(BF16)
    g1 = jnp.where(first, ar, a).astype(BF16)
    return [jnp.concatenate([g0, g0], axis=1), jnp.concatenate([g1, g1], axis=1)]


def _attend_windows(windows, first_valid, sinks, q, k, v):
    T = q.shape[0]
    k_rep, v_rep = _rep4(k), _rep4(v)
    head_lane = lax.broadcasted_iota(jnp.int32, (CHUNK, GQA_GROUP * HEAD_DIM), 1) // HEAD_DIM
    key_idx = lax.broadcasted_iota(jnp.int32, (GQA_GROUP * CHUNK, WINDOW + CHUNK), 1)
    ao_rows = []
    for ci in range(T // CHUNK):
        ao_g = []
        for g in range(N_KV_HEADS):
            qc = q[CHUNK * ci:CHUNK * (ci + 1), 256 * g:256 * (g + 1)]
            qm = jnp.concatenate([jnp.where(head_lane == h, qc, 0.0) for h in range(GQA_GROUP)], axis=0).astype(BF16)
            kw, vw = windows(k_rep, v_rep, g, ci)
            s = lax.dot_general(qm, kw, (((1,), (1,)), ((), ())), preferred_element_type=F32)
            fv = first_valid(ci)
            if fv is not None:
                s = jnp.where(key_idx >= fv, s, NEG_INF)
            ps = []
            for h in range(GQA_GROUP):
                sh = s[CHUNK * h:CHUNK * (h + 1)]
                sink = sinks[GQA_GROUP * g + h]
                m = jnp.maximum(jnp.max(sh, -1, keepdims=True), sink)
                p = jnp.exp(sh - m)
                denom = jnp.sum(p, -1, keepdims=True) + jnp.exp(sink - m)
                ps.append(p * (1.0 / denom))
            pm = jnp.concatenate(ps, axis=0).astype(BF16)
            r = _dot(pm, vw)
            o = jnp.where(head_lane == 0, r[0:CHUNK], 0.0)
            for h in range(1, GQA_GROUP):
                o = o + jnp.where(head_lane == h, r[CHUNK * h:CHUNK * (h + 1)], 0.0)
            ao_g.append(o)
        ao_rows.append(jnp.concatenate(ao_g, axis=1))
        yield
    return jnp.concatenate(ao_rows, axis=0), (k_rep, v_rep)


def _mixer(x, c, sa, sb, attend, LB, w):
    T = x.shape[0]
    nb = T // LB
    xb = x.astype(BF16)
    uv = jax.nn.gelu(_dot(xb, w.w_in[:, D_ATTN + 2 * D_KV:D_IN]))
    yield
    q = _dot(xb, w.w_in[:, 0:D_ATTN])
    kv = _dot(xb, w.w_in[:, D_ATTN:D_ATTN + 2 * D_KV])

    k = _rope(kv[:, :D_KV], c, sa, sb)
    v = kv[:, D_KV:]
    q = _rope(q, c, sa, sb) * (HEAD_DIM ** -0.5)

    yield
    u = uv[:, :D_CMLP]
    vm = _layernorm(uv[:, D_CMLP:], w.lnv_g[...], w.lnv_b[...])

    lo_mask = lax.broadcasted_iota(jnp.int32, (T, LANES), 1) < CMLP_GROUP_DIM
    pad = [jnp.zeros((CMLP_BLOCK - LB, LANES), BF16)] if LB < CMLP_BLOCK else []
    gate_slabs = []
    for p in range(D_CMLP // LANES):
        slab = vm[:, LANES * p:LANES * (p + 1)]
        lo = jnp.where(lo_mask, slab, 0.0).astype(BF16)
        hi = jnp.where(lo_mask, 0.0, slab).astype(BF16)
        rhs = jnp.concatenate(
            [jnp.concatenate([lo[LB * b:LB * (b + 1)]] + pad + [hi[LB * b:LB * (b + 1)]] + pad, axis=0)
             for b in range(nb)], axis=1)
        o = _dot(w.wpair[p, 0:LB, :], rhs)
        gate_slabs.append(jnp.concatenate([o[:, LANES * b:LANES * (b + 1)] for b in range(nb)], axis=0))
    s_gate = jnp.concatenate(gate_slabs, axis=1) + jnp.concatenate([w.bias[0:LB, :]] * nb, axis=0)
    co = u * s_gate
    yield

    ao, carry = yield from attend(q, k, v)

    an = _rmsnorm(ao, w.nga[...]).astype(BF16)
    cn = _rmsnorm(co, w.ngc[...]).astype(BF16)
    h1 = []
    for r in _row_halves(T):
        mix = _dot(an[r], w.w_out[0:D_ATTN, :]) + _dot(cn[r], w.w_out[D_ATTN:, :])
        yield
        h1.append(_layernorm(ALPHA * x[r] + mix, w.ln1g[...], w.ln1b[...]))
    return jnp.concatenate(h1, axis=0), k, v, vm, carry


def _ffn(h1, w):
    hb = h1.astype(BF16)
    acts = []
    for lo_c in range(0, D_FF, FF_CHUNK):
        hi_c = min(lo_c + FF_CHUNK, D_FF)
        gt = _dot(hb, w.wgu[:, lo_c:hi_c])
        up = _dot(hb, w.wgu[:, D_FF + lo_c:D_FF + hi_c])
        acts.append((gt * jax.nn.sigmoid(gt) * up).astype(BF16))
        yield
    act = jnp.concatenate(acts, axis=1)
    y = []
    for r in _row_halves(h1.shape[0]):
        f = _dot(act[r], w.wd[...])
        yield
        y.append(_layernorm(ALPHA * h1[r] + f, w.ln2g[...], w.ln2b[...]))
    return jnp.concatenate(y, axis=0)


def _run(*gens):
    results = [None] * len(gens)
    live = list(range(len(gens)))
    while live:
        for i in list(live):
            try:
                next(gens[i])
            except StopIteration as done:
                results[i] = done.value
                live.remove(i)
    return results


_WEIGHT_NAMES = ("w_in", "lnv_g", "lnv_b", "sinks", "wpair", "bias", "nga", "ngc", "w_out", "ln1g", "ln1b",
                 "wgu", "wd", "ln2g", "ln2b")
_Weights = collections.namedtuple("_Weights", _WEIGHT_NAMES)
N_W = len(_WEIGHT_NAMES)


def _pipelined_step(step, n_tiles, h1_ref, w, make_mixer, store_y, store_mixer, fill_extra=None):
    cur = step % 2
    prev = 1 - cur

    def phase(do_mixer, do_ffn, extra=None):
        gens = (([_ffn(h1_ref[prev], w)] if do_ffn else []) + ([extra()] if extra else [])
                + ([make_mixer()] if do_mixer else []))
        res = _run(*gens)
        if do_ffn:
            store_y(res[0])
        if do_mixer:
            h1_ref[cur] = res[-1][0]
            store_mixer(*res[-1][1:])

    pl.when(step == 0)(functools.partial(phase, True, False, fill_extra))
    pl.when(jnp.logical_and(step > 0, step < n_tiles))(functools.partial(phase, True, True))
    pl.when(step == n_tiles)(functools.partial(phase, False, True))


def _load_convert(jobs, then):
    plan = [(job, c) for job in jobs for c in range(job[0].shape[0] // job[4])]
    depth = W_SLOTS - 1

    def copy(i):
        (src, _, stage, sem, rows), c = plan[i]
        return pltpu.make_async_copy(src.at[pl.ds(c * rows, rows), :], stage.at[c % W_SLOTS], sem.at[c % W_SLOTS])

    for i in range(min(depth, len(plan))):
        copy(i).start(priority=i % 2)
    for i, ((_, dst, stage, _, rows), c) in enumerate(plan):
        copy(i).wait()
        dst[c * rows:(c + 1) * rows, :] = stage[c % W_SLOTS].astype(BF16)
        if i + depth < len(plan):
            copy(i + depth).start(priority=(i + depth) % 2)
        if (i + 1) % W_CHUNKS_PER_PHASE == 0:
            yield
    then()


def _prompt_body(T, tiles_per_seq, n_tiles, *refs):
    x_ref, c_ref, sa_ref, sb_ref = refs[:4]
    w = _Weights(*refs[4:4 + N_W])
    (y_ref, k_out_ref, v_out_ref, w_out_out, wgu_out, wd_out, hk_ref, hv_ref, h1_ref,
     w_out_v, wgu_v, wd_v, stage_gu, stage_d, sem_gu, sem_d, sem_out) = refs[4 + N_W:]
    step = pl.program_id(0)

    out_copies = (pltpu.make_async_copy(w_out_v, w_out_out, sem_out.at[0]),
                  pltpu.make_async_copy(wgu_v, wgu_out, sem_out.at[1]),
                  pltpu.make_async_copy(wd_v, wd_out, sem_out.at[2]))

    def start_out_copies():
        for cp in out_copies:
            cp.start()

    def convert_weights(w_hbm=w):
        jobs = [(w_hbm.w_out, w_out_v, stage_d, sem_d, WD_ROWS), (w_hbm.wgu, wgu_v, stage_gu, sem_gu, WGU_ROWS),
                (w_hbm.wd, wd_v, stage_d, sem_d, WD_ROWS)]
        return _load_convert(jobs, start_out_copies)

    @pl.when(step == n_tiles)
    def _():
        for cp in out_copies:
            cp.wait()

    w = w._replace(w_out=w_out_v, wgu=wgu_v, wd=wd_v)
    j = step % tiles_per_seq
    rd = j % 2
    wr = 1 - rd

    @pl.when(j == 0)
    def _():
        hk_ref[0] = jnp.zeros(hk_ref.shape[1:], BF16)
        hv_ref[0] = jnp.zeros(hv_ref.shape[1:], BF16)

    def first_valid(ci):
        if ci >= WINDOW // CHUNK:
            return None
        return jnp.where(j == 0, (WINDOW // CHUNK - ci) * CHUNK, 0)

    def make_mixer():
        hist_k = [hk_ref[rd, g] for g in range(N_KV_HEADS)]
        hist_v = [hv_ref[rd, g] for g in range(N_KV_HEADS)]

        def windows(k_rep, v_rep, g, ci):
            k_all = jnp.concatenate([hist_k[g], k_rep[g]], axis=0)
            v_all = jnp.concatenate([hist_v[g], v_rep[g]], axis=0)
            lo = CHUNK * ci
            return k_all[lo:lo + WINDOW + CHUNK], v_all[lo:lo + WINDOW + CHUNK]

        attend = functools.partial(_attend_windows, windows, first_valid, w.sinks)
        return _mixer(x_ref[...], c_ref[...], sa_ref[...], sb_ref[...], attend, CMLP_BLOCK, w)

    def store_y(y):
        y_ref[...] = y

    def store_mixer(k, v, _, carry):
        k_out_ref[0] = jnp.transpose(k[T - WINDOW:, :])
        v_out_ref[0] = jnp.transpose(v[T - WINDOW:, :])
        k_rep, v_rep = carry
        for g in range(N_KV_HEADS):
            hk_ref[wr, g] = k_rep[g][T - WINDOW:]
            hv_ref[wr, g] = v_rep[g][T - WINDOW:]

    _pipelined_step(step, n_tiles, h1_ref, w, make_mixer, store_y, store_mixer, fill_extra=convert_weights)


def _scatter_heads(out_ref, a):
    n = a.shape[1] // HEAD_DIM
    for i in range(n):
        out_ref[pl.ds(i, a.shape[0], stride=n), :] = a[:, HEAD_DIM * i:HEAD_DIM * (i + 1)]


def _cache_rep4(ref):
    rows = [jnp.transpose(ref[D_KV * s:D_KV * (s + 1), :]) for s in range(ref.shape[0] // D_KV)]
    return _rep4(jnp.concatenate(rows, axis=0))


def _sample_body(n_tiles, *refs):
    x_ref, c_ref, sa_ref, sb_ref, ck_ref, cv_ref = refs[:6]
    w = _Weights(*refs[6:6 + N_W])
    y_ref, k_out_ref, v_out_ref, vm_out_ref, h1_ref, wgu_v, wd_v, sem = refs[6 + N_W:]
    step = pl.program_id(0)

    fetches = (pltpu.make_async_copy(w.wgu, wgu_v, sem.at[0]), pltpu.make_async_copy(w.wd, wd_v, sem.at[1]))

    @pl.when(step == 0)
    def _():
        for cp in fetches:
            cp.start()

    @pl.when(step == 1)
    def _():
        for cp in fetches:
            cp.wait()

    w = w._replace(wgu=wgu_v, wd=wd_v)

    def make_mixer():
        ck_rep, cv_rep = _cache_rep4(ck_ref), _cache_rep4(cv_ref)

        def windows(k_rep, v_rep, g, ci):
            cat = lambda cache, new: jnp.concatenate(
                [cache[g][WINDOW * ci:WINDOW * (ci + 1)], new[g][CHUNK * ci:CHUNK * (ci + 1)]], axis=0)
            return cat(ck_rep, k_rep), cat(cv_rep, v_rep)

        attend = functools.partial(_attend_windows, windows, lambda ci: None, w.sinks)
        return _mixer(x_ref[...], c_ref[...], sa_ref[...], sb_ref[...], attend, CHUNK, w)

    def store_y(y):
        y_ref[...] = y

    def store_mixer(k, v, vm, *_):
        _scatter_heads(k_out_ref, k)
        _scatter_heads(v_out_ref, v)
        _scatter_heads(vm_out_ref, vm)

    _pipelined_step(step, n_tiles, h1_ref, w, make_mixer, store_y, store_mixer)


def _cast_body(*refs):
    n = len(refs) // 2
    for src, dst in zip(refs[:n], refs[n:]):
        dst[...] = src[...].astype(BF16)


def _to_bf16(*ws):
    specs = [pl.BlockSpec((a.shape[0] // CAST_STEPS, a.shape[1]), lambda i: (i, 0)) for a in ws]
    return pl.pallas_call(
        _cast_body,
        grid=(CAST_STEPS,),
        in_specs=specs,
        out_specs=specs,
        out_shape=[jax.ShapeDtypeStruct(a.shape, BF16) for a in ws],
        compiler_params=pltpu.CompilerParams(dimension_semantics=("arbitrary",), vmem_limit_bytes=VMEM_LIMIT),
        name="cast_weights",
    )(*ws)


def _const_spec(shape):
    nd = len(shape)
    return pl.BlockSpec(shape, lambda *_: (0,) * nd)


def _weight_specs(ws, in_hbm=()):
    spec = lambda name, a: (pl.BlockSpec(memory_space=pltpu.SMEM) if name == "sinks" else
                            pl.BlockSpec(memory_space=pl.ANY) if name in in_hbm else _const_spec(a.shape))
    return [spec(name, a) for name, a in zip(_WEIGHT_NAMES, ws)]


def _rope_tables(pos, reps=1):
    half = ROPE_DIM // 2
    inv = np.power(ROPE_THETA, -np.arange(half, dtype=np.float64) * (2.0 / ROPE_DIM))
    ang = np.asarray(pos, np.float64)[:, None] * inv[None, :]
    cos, sin = np.cos(ang), np.sin(ang)
    n = ang.shape[0]
    pad = HEAD_DIM - ROPE_DIM
    c = np.concatenate([cos, cos, np.ones((n, pad))], 1)
    sa = np.concatenate([-sin, np.zeros((n, half + pad))], 1)
    sb = np.concatenate([np.zeros((n, half)), sin, np.zeros((n, pad))], 1)
    return [jnp.asarray(np.tile(t, (reps, LANES // HEAD_DIM)), F32) for t in (c, sa, sb)]


def _gate_params(w_s, b_s, lb):
    i = np.arange(lb)
    mask = (i[None, :] // CHUNK) <= (i[:, None] // CHUNK)
    w = jnp.where(mask[None], w_s[:, :lb, :lb], 0.0).reshape(CMLP_GROUPS // 2, 2, lb, lb)
    wpair = jnp.transpose(w, (0, 2, 1, 3)).reshape(CMLP_GROUPS // 2, lb, 2 * lb).astype(BF16)
    bias = jnp.repeat(jnp.transpose(b_s[:, :lb]), CMLP_GROUP_DIM, axis=1)
    return wpair, bias


def kernel(x_prompt, x_sample, cache_win_k, cache_win_v, w_in, ln_v_g, ln_v_b, attn_sinks, w_spatial, b_spatial,
           norm_attn_g, norm_cmlp_g, w_out, ln1_g, ln1_b, w_gate_up, w_down, ln2_g, ln2_b):
    B, S, _ = x_prompt.shape
    Bd, L, _ = x_sample.shape
    assert cache_win_k.shape[2] == WINDOW and L == CHUNK and w_in.shape[0] == 1

    row = lambda a: a[0].reshape(1, -1)
    w_in_b, = _to_bf16(w_in[0])

    wpair, bias = _gate_params(w_spatial[0], b_spatial[0], CMLP_BLOCK)

    def weights(w_out_, wgu, wd):
        return _Weights(w_in=w_in_b, lnv_g=row(ln_v_g), lnv_b=row(ln_v_b), sinks=attn_sinks[0],
                        wpair=wpair, bias=bias, nga=row(norm_attn_g), ngc=row(norm_cmlp_g),
                        w_out=w_out_, ln1g=row(ln1_g), ln1b=row(ln1_b),
                        wgu=wgu, wd=wd, ln2g=row(ln2_g), ln2b=row(ln2_b))

    T = PROMPT_TILE
    tps = S // T
    n_tiles = B * tps
    wp = weights(w_out[0], w_gate_up[0], w_down[0])
    tables = _rope_tables(np.arange(S))
    mix_tile = lambda s: jnp.minimum(s, n_tiles - 1)
    ffn_tile = lambda s: jnp.maximum(s - 1, 0)
    tab = pl.BlockSpec((T, LANES), lambda s: (mix_tile(s) % tps, 0))
    win = pl.BlockSpec((1, WINDOW, D_KV), lambda s: (mix_tile(s) // tps, 0, 0))
    hbm = pl.BlockSpec(memory_space=pl.ANY)
    y_p, kwin, vwin, w_out_b, wgu_b, wd_b = pl.pallas_call(
        functools.partial(_prompt_body, T, tps, n_tiles),
        grid=(n_tiles + 1,),
        in_specs=[pl.BlockSpec((T, D_MODEL), lambda s: (mix_tile(s), 0)), tab, tab, tab]
        + _weight_specs(wp, in_hbm=("w_out", "wgu", "wd")),
        out_specs=[pl.BlockSpec((T, D_MODEL), lambda s: (ffn_tile(s), 0)), win, win, hbm, hbm, hbm],
        out_shape=[jax.ShapeDtypeStruct((B * S, D_MODEL), F32),
                   jax.ShapeDtypeStruct((B, WINDOW, D_KV), F32),
                   jax.ShapeDtypeStruct((B, WINDOW, D_KV), F32),
                   jax.ShapeDtypeStruct((D_MODEL, D_MODEL), BF16),
                   jax.ShapeDtypeStruct((D_MODEL, 2 * D_FF), BF16),
                   jax.ShapeDtypeStruct((D_FF, D_MODEL), BF16)],
        scratch_shapes=[pltpu.VMEM((2, N_KV_HEADS, WINDOW, 2 * LANES), BF16),
                        pltpu.VMEM((2, N_KV_HEADS, WINDOW, 2 * LANES), BF16),
                        pltpu.VMEM((2, T, D_MODEL), F32),
                        pltpu.VMEM((D_MODEL, D_MODEL), BF16),
                        pltpu.VMEM((D_MODEL, 2 * D_FF), BF16),
                        pltpu.VMEM((D_FF, D_MODEL), BF16),
                        pltpu.VMEM((W_SLOTS, WGU_ROWS, 2 * D_FF), F32),
                        pltpu.VMEM((W_SLOTS, WD_ROWS, D_MODEL), F32),
                        pltpu.SemaphoreType.DMA((W_SLOTS,)),
                        pltpu.SemaphoreType.DMA((W_SLOTS,)),
                        pltpu.SemaphoreType.DMA((3,))],
        compiler_params=pltpu.CompilerParams(dimension_semantics=("arbitrary",), vmem_limit_bytes=VMEM_LIMIT),
        name="layer_prompt",
    )(x_prompt.reshape(B * S, D_MODEL), *tables, *wp)
    y_p = y_p.reshape(B, S, D_MODEL)

    NS = SAMPLE_SEQS
    Ts = NS * L
    ws = weights(w_out_b, wgu_b, wd_b)
    tables_s = _rope_tables(PAST_LEN + np.arange(L), reps=NS)
    n_st = Bd // NS
    mix_rows = lambda n, width: pl.BlockSpec((n, width), lambda s: (jnp.minimum(s, n_st - 1), 0))
    cache_rows = NS * D_KV
    cache_t = lambda a: jnp.transpose(a[0], (0, 2, 3, 1)).reshape(Bd * D_KV, WINDOW)
    y_s, k_s, v_s, vm_s = pl.pallas_call(
        functools.partial(_sample_body, n_st),
        grid=(n_st + 1,),
        in_specs=[mix_rows(Ts, D_MODEL)] + [_const_spec(t.shape) for t in tables_s]
        + [mix_rows(cache_rows, WINDOW), mix_rows(cache_rows, WINDOW)] + _weight_specs(ws, in_hbm=("wgu", "wd")),
        out_specs=[pl.BlockSpec((Ts, D_MODEL), lambda s: (jnp.maximum(s - 1, 0), 0)),
                   mix_rows(Ts * N_KV_HEADS, HEAD_DIM), mix_rows(Ts * N_KV_HEADS, HEAD_DIM),
                   mix_rows(Ts * CMLP_GROUPS, CMLP_GROUP_DIM)],
        out_shape=[jax.ShapeDtypeStruct((Bd * L, D_MODEL), F32),
                   jax.ShapeDtypeStruct((Bd * L * N_KV_HEADS, HEAD_DIM), F32),
                   jax.ShapeDtypeStruct((Bd * L * N_KV_HEADS, HEAD_DIM), F32),
                   jax.ShapeDtypeStruct((Bd * L * CMLP_GROUPS, CMLP_GROUP_DIM), F32)],
        scratch_shapes=[pltpu.VMEM((2, Ts, D_MODEL), F32),
                        pltpu.VMEM((D_MODEL, 2 * D_FF), BF16),
                        pltpu.VMEM((D_FF, D_MODEL), BF16),
                        pltpu.SemaphoreType.DMA((2,))],
        compiler_params=pltpu.CompilerParams(dimension_semantics=("arbitrary",), vmem_limit_bytes=VMEM_LIMIT),
        name="layer_sample",
    )(x_sample.reshape(Bd * L, D_MODEL), *tables_s,
      cache_t(cache_win_k), cache_t(cache_win_v), *ws)

    win_result = lambda a: jnp.transpose(a.reshape(B, N_KV_HEADS, HEAD_DIM, WINDOW), (0, 3, 1, 2))[None]
    return (y_p, y_s.reshape(Bd, L, D_MODEL),
            win_result(kwin), win_result(vwin),
            k_s.reshape(1, Bd, L, N_KV_HEADS, HEAD_DIM),
            v_s.reshape(1, Bd, L, N_KV_HEADS, HEAD_DIM),
            vm_s.reshape(1, Bd, L, CMLP_GROUPS, CMLP_GROUP_DIM))
```

```python
import collections
import functools

import jax
import jax.numpy as jnp
import numpy as np
from jax import lax
from jax.experimental import pallas as pl
from jax.experimental.pallas import tpu as pltpu

D_MODEL = 1024
CHUNK = 64
HEAD_DIM = 64
D_ATTN = 512
D_CMLP = 512
N_HEADS = 8
N_KV_HEADS = 2
GQA_GROUP = 4
D_KV = 128
WINDOW = 128
ROPE_DIM = 16
ROPE_THETA = 500000.0
CMLP_BLOCK = 128
CMLP_GROUPS = 8
CMLP_GROUP_DIM = 64
D_IN = 1792
D_FF = 2816
PAST_LEN = 1024
ALPHA = 2.0 ** 0.25
LN_EPS = 1e-5
NEG_INF = -1e30
LANES = 128
FF_CHUNK = 256
VMEM_LIMIT = 56 * 1024 * 1024
CAST_STEPS = 4
PROMPT_TILE = 512
SAMPLE_SEQS = 4
WGU_ROWS, WD_ROWS = 16, 64
W_SLOTS = 8
W_CHUNKS_PER_PHASE = 8

F32 = jnp.float32
BF16 = jnp.bfloat16


def _layernorm(x, g, b):
    mu = jnp.mean(x, -1, keepdims=True)
    d = x - mu
    var = jnp.mean(d * d, -1, keepdims=True)
    return d * lax.rsqrt(var + LN_EPS) * g + b


def _rmsnorm(x, g):
    ms = jnp.mean(x * x, -1, keepdims=True)
    return x * lax.rsqrt(ms + LN_EPS) * g


def _dot(a, b):
    return jnp.dot(a, b, preferred_element_type=F32)


def _rope(x, c, sa, sb):
    out = []
    for i in range(x.shape[1] // LANES):
        s = x[:, LANES * i:LANES * (i + 1)]
        up = pltpu.roll(s, LANES - ROPE_DIM // 2, 1)
        dn = pltpu.roll(s, ROPE_DIM // 2, 1)
        out.append(s * c + up * sa + dn * sb)
    return out[0] if len(out) == 1 else jnp.concatenate(out, axis=1)


def _row_halves(n):
    return (slice(0, n // 2), slice(n // 2, n))


def _rep4(a):
    ar = pltpu.roll(a, HEAD_DIM, 1)
    first = lax.broadcasted_iota(jnp.int32, a.shape, 1) < HEAD_DIM
    g0 = jnp.where(first, a, ar).astype(BF16)
    g1 = jnp.where(first, ar, a).astype(BF16)
    return [jnp.concatenate([g0, g0], axis=1), jnp.concatenate([g1, g1], axis=1)]


def _attend_windows(windows, first_valid, sinks, q, k, v):
    T = q.shape[0]
    k_rep, v_rep = _rep4(k), _rep4(v)
    head_lane = lax.broadcasted_iota(jnp.int32, (CHUNK, GQA_GROUP * HEAD_DIM), 1) // HEAD_DIM
    key_idx = lax.broadcasted_iota(jnp.int32, (GQA_GROUP * CHUNK, WINDOW + CHUNK), 1)
    ao_rows = []
    for ci in range(T // CHUNK):
        ao_g = []
        for g in range(N_KV_HEADS):
            qc = q[CHUNK * ci:CHUNK * (ci + 1), 256 * g:256 * (g + 1)]
            qm = jnp.concatenate([jnp.where(head_lane == h, qc, 0.0) for h in range(GQA_GROUP)], axis=0).astype(BF16)
            kw, vw = windows(k_rep, v_rep, g, ci)
            s = lax.dot_general(qm, kw, (((1,), (1,)), ((), ())), preferred_element_type=F32)
            fv = first_valid(ci)
            if fv is not None:
                s = jnp.where(key_idx >= fv, s, NEG_INF)
            ps = []
            for h in range(GQA_GROUP):
                sh = s[CHUNK * h:CHUNK * (h + 1)]
                sink = sinks[GQA_GROUP * g + h]
                m = jnp.maximum(jnp.max(sh, -1, keepdims=True), sink)
                p = jnp.exp(sh - m)
                denom = jnp.sum(p, -1, keepdims=True) + jnp.exp(sink - m)
                ps.append(p * (1.0 / denom))
            pm = jnp.concatenate(ps, axis=0).astype(BF16)
            r = _dot(pm, vw)
            o = jnp.where(head_lane == 0, r[0:CHUNK], 0.0)
            for h in range(1, GQA_GROUP):
                o = o + jnp.where(head_lane == h, r[CHUNK * h:CHUNK * (h + 1)], 0.0)
            ao_g.append(o)
        ao_rows.append(jnp.concatenate(ao_g, axis=1))
        yield
    return jnp.concatenate(ao_rows, axis=0), (k_rep, v_rep)


def _mixer(x, c, sa, sb, attend, LB, w):
    T = x.shape[0]
    nb = T // LB
    xb = x.astype(BF16)
    uv = jax.nn.gelu(_dot(xb, w.w_in[:, D_ATTN + 2 * D_KV:D_IN]))
    yield
    q = _dot(xb, w.w_in[:, 0:D_ATTN])
    kv = _dot(xb, w.w_in[:, D_ATTN:D_ATTN + 2 * D_KV])

    k = _rope(kv[:, :D_KV], c, sa, sb)
    v = kv[:, D_KV:]
    q = _rope(q, c, sa, sb) * (HEAD_DIM ** -0.5)

    yield
    u = uv[:, :D_CMLP]
    vm = _layernorm(uv[:, D_CMLP:], w.lnv_g[...], w.lnv_b[...])

    lo_mask = lax.broadcasted_iota(jnp.int32, (T, LANES), 1) < CMLP_GROUP_DIM
    pad = [jnp.zeros((CMLP_BLOCK - LB, LANES), BF16)] if LB < CMLP_BLOCK else []
    gate_slabs = []
    for p in range(D_CMLP // LANES):
        slab = vm[:, LANES * p:LANES * (p + 1)]
        lo = jnp.where(lo_mask, slab, 0.0).astype(BF16)
        hi = jnp.where(lo_mask, 0.0, slab).astype(BF16)
        rhs = jnp.concatenate(
            [jnp.concatenate([lo[LB * b:LB * (b + 1)]] + pad + [hi[LB * b:LB * (b + 1)]] + pad, axis=0)
             for b in range(nb)], axis=1)
        w_pair = jnp.concatenate([w.wpair[2 * p, 0:LB, :], w.wpair[2 * p + 1, 0:LB, :]], axis=1)
        o = _dot(w_pair, rhs)
        gate_slabs.append(jnp.concatenate([o[:, LANES * b:LANES * (b + 1)] for b in range(nb)], axis=0))
    s_gate = jnp.concatenate(gate_slabs, axis=1) + jnp.concatenate([w.bias[0:LB, :]] * nb, axis=0)
    co = u * s_gate
    yield

    ao, carry = yield from attend(q, k, v)

    an = _rmsnorm(ao, w.nga[...]).astype(BF16)
    cn = _rmsnorm(co, w.ngc[...]).astype(BF16)
    h1 = []
    for r in _row_halves(T):
        mix = _dot(an[r], w.w_out[0:D_ATTN, :]) + _dot(cn[r], w.w_out[D_ATTN:, :])
        yield
        h1.append(_layernorm(ALPHA * x[r] + mix, w.ln1g[...], w.ln1b[...]))
    return jnp.concatenate(h1, axis=0), k, v, vm, carry


def _ffn(h1, w):
    hb = h1.astype(BF16)
    acts = []
    for lo_c in range(0, D_FF, FF_CHUNK):
        hi_c = min(lo_c + FF_CHUNK, D_FF)
        gt = _dot(hb, w.wgu[:, lo_c:hi_c])
        up = _dot(hb, w.wgu[:, D_FF + lo_c:D_FF + hi_c])
        acts.append((gt * jax.nn.sigmoid(gt) * up).astype(BF16))
        yield
    act = jnp.concatenate(acts, axis=1)
    y = []
    for r in _row_halves(h1.shape[0]):
        f = _dot(act[r], w.wd[...])
        yield
        y.append(_layernorm(ALPHA * h1[r] + f, w.ln2g[...], w.ln2b[...]))
    return jnp.concatenate(y, axis=0)


def _run(*gens):
    results = [None] * len(gens)
    live = list(range(len(gens)))
    while live:
        for i in list(live):
            try:
                next(gens[i])
            except StopIteration as done:
                results[i] = done.value
                live.remove(i)
    return results


_WEIGHT_NAMES = ("w_in", "lnv_g", "lnv_b", "sinks", "wpair", "bias", "nga", "ngc", "w_out", "ln1g", "ln1b",
                 "wgu", "wd", "ln2g", "ln2b")
_Weights = collections.namedtuple("_Weights", _WEIGHT_NAMES)
N_W = len(_WEIGHT_NAMES)


def _pipelined_step(step, n_tiles, h1_ref, w, make_mixer, store_y, store_mixer, fill_extra=None):
    cur = step % 2
    prev = 1 - cur

    def phase(do_mixer, do_ffn, extra=None):
        gens = (([_ffn(h1_ref[prev], w)] if do_ffn else []) + ([extra()] if extra else [])
                + ([make_mixer()] if do_mixer else []))
        res = _run(*gens)
        if do_ffn:
            store_y(res[0])
        if do_mixer:
            h1_ref[cur] = res[-1][0]
            store_mixer(*res[-1][1:])

    pl.when(step == 0)(functools.partial(phase, True, False, fill_extra))
    pl.when(jnp.logical_and(step > 0, step < n_tiles))(functools.partial(phase, True, True))
    pl.when(step == n_tiles)(functools.partial(phase, False, True))


def _load_convert(jobs, then):
    plan = [(job, c) for job in jobs for c in range(job[0].shape[0] // job[4])]
    depth = W_SLOTS - 1

    def copy(i):
        (src, _, stage, sem, rows), c = plan[i]
        return pltpu.make_async_copy(src.at[pl.ds(c * rows, rows), :], stage.at[c % W_SLOTS], sem.at[c % W_SLOTS])

    for i in range(min(depth, len(plan))):
        copy(i).start(priority=i % 2)
    for i, ((_, dst, stage, _, rows), c) in enumerate(plan):
        copy(i).wait()
        dst[c * rows:(c + 1) * rows, :] = stage[c % W_SLOTS].astype(BF16)
        if i + depth < len(plan):
            copy(i + depth).start(priority=(i + depth) % 2)
        if (i + 1) % W_CHUNKS_PER_PHASE == 0:
            yield
    then()


def _prompt_body(T, tiles_per_seq, n_tiles, *refs):
    x_ref, c_ref, sa_ref, sb_ref = refs[:4]
    w = _Weights(*refs[4:4 + N_W])
    (y_ref, k_out_ref, v_out_ref, w_out_out, wgu_out, wd_out, hk_ref, hv_ref, h1_ref,
     w_out_v, wgu_v, wd_v, stage_gu, stage_d, sem_gu, sem_d, sem_out) = refs[4 + N_W:]
    step = pl.program_id(0)

    out_copies = (pltpu.make_async_copy(w_out_v, w_out_out, sem_out.at[0]),
                  pltpu.make_async_copy(wgu_v, wgu_out, sem_out.at[1]),
                  pltpu.make_async_copy(wd_v, wd_out, sem_out.at[2]))

    def start_out_copies():
        for cp in out_copies:
            cp.start()

    def convert_weights(w_hbm=w):
        jobs = [(w_hbm.w_out, w_out_v, stage_d, sem_d, WD_ROWS), (w_hbm.wgu, wgu_v, stage_gu, sem_gu, WGU_ROWS),
                (w_hbm.wd, wd_v, stage_d, sem_d, WD_ROWS)]
        return _load_convert(jobs, start_out_copies)

    @pl.when(step == n_tiles)
    def _():
        for cp in out_copies:
            cp.wait()

    w = w._replace(w_out=w_out_v, wgu=wgu_v, wd=wd_v)
    j = step % tiles_per_seq
    rd = j % 2
    wr = 1 - rd

    @pl.when(j == 0)
    def _():
        hk_ref[0] = jnp.zeros(hk_ref.shape[1:], BF16)
        hv_ref[0] = jnp.zeros(hv_ref.shape[1:], BF16)

    def first_valid(ci):
        if ci >= WINDOW // CHUNK:
            return None
        return jnp.where(j == 0, (WINDOW // CHUNK - ci) * CHUNK, 0)

    def make_mixer():
        hist_k = [hk_ref[rd, g] for g in range(N_KV_HEADS)]
        hist_v = [hv_ref[rd, g] for g in range(N_KV_HEADS)]

        def windows(k_rep, v_rep, g, ci):
            k_all = jnp.concatenate([hist_k[g], k_rep[g]], axis=0)
            v_all = jnp.concatenate([hist_v[g], v_rep[g]], axis=0)
            lo = CHUNK * ci
            return k_all[lo:lo + WINDOW + CHUNK], v_all[lo:lo + WINDOW + CHUNK]

        attend = functools.partial(_attend_windows, windows, first_valid, w.sinks)
        return _mixer(x_ref[...], c_ref[...], sa_ref[...], sb_ref[...], attend, CMLP_BLOCK, w)

    def store_y(y):
        y_ref[...] = y

    def store_mixer(k, v, _, carry):
        k_out_ref[0] = jnp.transpose(k[T - WINDOW:, :])
        v_out_ref[0] = jnp.transpose(v[T - WINDOW:, :])
        k_rep, v_rep = carry
        for g in range(N_KV_HEADS):
            hk_ref[wr, g] = k_rep[g][T - WINDOW:]
            hv_ref[wr, g] = v_rep[g][T - WINDOW:]

    _pipelined_step(step, n_tiles, h1_ref, w, make_mixer, store_y, store_mixer, fill_extra=convert_weights)


def _scatter_heads(out_ref, a):
    n = a.shape[1] // HEAD_DIM
    for i in range(n):
        out_ref[pl.ds(i, a.shape[0], stride=n), :] = a[:, HEAD_DIM * i:HEAD_DIM * (i + 1)]


def _cache_rep4(ref):
    rows = [jnp.transpose(ref[D_KV * s:D_KV * (s + 1), :]) for s in range(ref.shape[0] // D_KV)]
    return _rep4(jnp.concatenate(rows, axis=0))


def _sample_body(n_tiles, *refs):
    x_ref, c_ref, sa_ref, sb_ref, ck_ref, cv_ref = refs[:6]
    w = _Weights(*refs[6:6 + N_W])
    y_ref, k_out_ref, v_out_ref, vm_out_ref, h1_ref, wgu_v, wd_v, sem = refs[6 + N_W:]
    step = pl.program_id(0)

    fetches = (pltpu.make_async_copy(w.wgu, wgu_v, sem.at[0]), pltpu.make_async_copy(w.wd, wd_v, sem.at[1]))

    @pl.when(step == 0)
    def _():
        for cp in fetches:
            cp.start()

    @pl.when(step == 1)
    def _():
        for cp in fetches:
            cp.wait()

    w = w._replace(wgu=wgu_v, wd=wd_v)

    def make_mixer():
        ck_rep, cv_rep = _cache_rep4(ck_ref), _cache_rep4(cv_ref)

        def windows(k_rep, v_rep, g, ci):
            cat = lambda cache, new: jnp.concatenate(
                [cache[g][WINDOW * ci:WINDOW * (ci + 1)], new[g][CHUNK * ci:CHUNK * (ci + 1)]], axis=0)
            return cat(ck_rep, k_rep), cat(cv_rep, v_rep)

        attend = functools.partial(_attend_windows, windows, lambda ci: None, w.sinks)
        return _mixer(x_ref[...], c_ref[...], sa_ref[...], sb_ref[...], attend, CHUNK, w)

    def store_y(y):
        y_ref[...] = y

    def store_mixer(k, v, vm, *_):
        _scatter_heads(k_out_ref, k)
        _scatter_heads(v_out_ref, v)
        _scatter_heads(vm_out_ref, vm)

    _pipelined_step(step, n_tiles, h1_ref, w, make_mixer, store_y, store_mixer)


def _cast_body(*refs):
    n = len(refs) // 2
    for src, dst in zip(refs[:n], refs[n:]):
        dst[...] = src[...].astype(BF16)


def _to_bf16(*ws):
    specs = [pl.BlockSpec((a.shape[0] // CAST_STEPS, a.shape[1]), lambda i: (i, 0)) for a in ws]
    return pl.pallas_call(
        _cast_body,
        grid=(CAST_STEPS,),
        in_specs=specs,
        out_specs=specs,
        out_shape=[jax.ShapeDtypeStruct(a.shape, BF16) for a in ws],
        compiler_params=pltpu.CompilerParams(dimension_semantics=("arbitrary",), vmem_limit_bytes=VMEM_LIMIT),
        name="cast_weights",
    )(*ws)


def _const_spec(shape):
    nd = len(shape)
    return pl.BlockSpec(shape, lambda *_: (0,) * nd)


def _weight_specs(ws, in_hbm=()):
    spec = lambda name, a: (pl.BlockSpec(memory_space=pltpu.SMEM) if name == "sinks" else
                            pl.BlockSpec(memory_space=pl.ANY) if name in in_hbm else _const_spec(a.shape))
    return [spec(name, a) for name, a in zip(_WEIGHT_NAMES, ws)]


def _rope_tables(pos, reps=1):
    half = ROPE_DIM // 2
    inv = np.power(ROPE_THETA, -np.arange(half, dtype=np.float64) * (2.0 / ROPE_DIM))
    ang = np.asarray(pos, np.float64)[:, None] * inv[None, :]
    cos, sin = np.cos(ang), np.sin(ang)
    n = ang.shape[0]
    pad = HEAD_DIM - ROPE_DIM
    c = np.concatenate([cos, cos, np.ones((n, pad))], 1)
    sa = np.concatenate([-sin, np.zeros((n, half + pad))], 1)
    sb = np.concatenate([np.zeros((n, half)), sin, np.zeros((n, pad))], 1)
    return [jnp.asarray(np.tile(t, (reps, LANES // HEAD_DIM)), F32) for t in (c, sa, sb)]


def _gate_params(w_s, b_s, lb):
    i = np.arange(lb)
    mask = (i[None, :] // CHUNK) <= (i[:, None] // CHUNK)
    wpair = jnp.where(mask[None], w_s[:, :lb, :lb], 0.0).astype(BF16)
    bias = jnp.repeat(jnp.transpose(b_s[:, :lb]), CMLP_GROUP_DIM, axis=1)
    return wpair, bias


def kernel(x_prompt, x_sample, cache_win_k, cache_win_v, w_in, ln_v_g, ln_v_b, attn_sinks, w_spatial, b_spatial,
           norm_attn_g, norm_cmlp_g, w_out, ln1_g, ln1_b, w_gate_up, w_down, ln2_g, ln2_b):
    B, S, _ = x_prompt.shape
    Bd, L, _ = x_sample.shape
    assert cache_win_k.shape[2] == WINDOW and L == CHUNK and w_in.shape[0] == 1

    row = lambda a: a[0].reshape(1, -1)
    w_in_b, = _to_bf16(w_in[0])

    wpair, bias = _gate_params(w_spatial[0], b_spatial[0], CMLP_BLOCK)

    def weights(w_out_, wgu, wd):
        return _Weights(w_in=w_in_b, lnv_g=row(ln_v_g), lnv_b=row(ln_v_b), sinks=attn_sinks[0],
                        wpair=wpair, bias=bias, nga=row(norm_attn_g), ngc=row(norm_cmlp_g),
                        w_out=w_out_, ln1g=row(ln1_g), ln1b=row(ln1_b),
                        wgu=wgu, wd=wd, ln2g=row(ln2_g), ln2b=row(ln2_b))

    T = PROMPT_TILE
    tps = S // T
    n_tiles = B * tps
    wp = weights(w_out[0], w_gate_up[0], w_down[0])
    tables = _rope_tables(np.arange(S))
    mix_tile = lambda s: jnp.minimum(s, n_tiles - 1)
    ffn_tile = lambda s: jnp.maximum(s - 1, 0)
    tab = pl.BlockSpec((T, LANES), lambda s: (mix_tile(s) % tps, 0))
    win = pl.BlockSpec((1, WINDOW, D_KV), lambda s: (mix_tile(s) // tps, 0, 0))
    hbm = pl.BlockSpec(memory_space=pl.ANY)
    y_p, kwin, vwin, w_out_b, wgu_b, wd_b = pl.pallas_call(
        functools.partial(_prompt_body, T, tps, n_tiles),
        grid=(n_tiles + 1,),
        in_specs=[pl.BlockSpec((T, D_MODEL), lambda s: (mix_tile(s), 0)), tab, tab, tab]
        + _weight_specs(wp, in_hbm=("w_out", "wgu", "wd")),
        out_specs=[pl.BlockSpec((T, D_MODEL), lambda s: (ffn_tile(s), 0)), win, win, hbm, hbm, hbm],
        out_shape=[jax.ShapeDtypeStruct((B * S, D_MODEL), F32),
                   jax.ShapeDtypeStruct((B, WINDOW, D_KV), F32),
                   jax.ShapeDtypeStruct((B, WINDOW, D_KV), F32),
                   jax.ShapeDtypeStruct((D_MODEL, D_MODEL), BF16),
                   jax.ShapeDtypeStruct((D_MODEL, 2 * D_FF), BF16),
                   jax.ShapeDtypeStruct((D_FF, D_MODEL), BF16)],
        scratch_shapes=[pltpu.VMEM((2, N_KV_HEADS, WINDOW, 2 * LANES), BF16),
                        pltpu.VMEM((2, N_KV_HEADS, WINDOW, 2 * LANES), BF16),
                        pltpu.VMEM((2, T, D_MODEL), F32),
                        pltpu.VMEM((D_MODEL, D_MODEL), BF16),
                        pltpu.VMEM((D_MODEL, 2 * D_FF), BF16),
                        pltpu.VMEM((D_FF, D_MODEL), BF16),
                        pltpu.VMEM((W_SLOTS, WGU_ROWS, 2 * D_FF), F32),
                        pltpu.VMEM((W_SLOTS, WD_ROWS, D_MODEL), F32),
                        pltpu.SemaphoreType.DMA((W_SLOTS,)),
                        pltpu.SemaphoreType.DMA((W_SLOTS,)),
                        pltpu.SemaphoreType.DMA((3,))],
        compiler_params=pltpu.CompilerParams(dimension_semantics=("arbitrary",), vmem_limit_bytes=VMEM_LIMIT),
        name="layer_prompt",
    )(x_prompt.reshape(B * S, D_MODEL), *tables, *wp)
    y_p = y_p.reshape(B, S, D_MODEL)

    NS = SAMPLE_SEQS
    Ts = NS * L
    ws = weights(w_out_b, wgu_b, wd_b)
    tables_s = _rope_tables(PAST_LEN + np.arange(L), reps=NS)
    n_st = Bd // NS
    mix_rows = lambda n, width: pl.BlockSpec((n, width), lambda s: (jnp.minimum(s, n_st - 1), 0))
    cache_rows = NS * D_KV
    cache_t = lambda a: jnp.transpose(a[0], (0, 2, 3, 1)).reshape(Bd * D_KV, WINDOW)
    y_s, k_s, v_s, vm_s = pl.pallas_call(
        functools.partial(_sample_body, n_st),
        grid=(n_st + 1,),
        in_specs=[mix_rows(Ts, D_MODEL)] + [_const_spec(t.shape) for t in tables_s]
        + [mix_rows(cache_rows, WINDOW), mix_rows(cache_rows, WINDOW)] + _weight_specs(ws, in_hbm=("wgu", "wd")),
        out_specs=[pl.BlockSpec((Ts, D_MODEL), lambda s: (jnp.maximum(s - 1, 0), 0)),
                   mix_rows(Ts * N_KV_HEADS, HEAD_DIM), mix_rows(Ts * N_KV_HEADS, HEAD_DIM),
                   mix_rows(Ts * CMLP_GROUPS, CMLP_GROUP_DIM)],
        out_shape=[jax.ShapeDtypeStruct((Bd * L, D_MODEL), F32),
                   jax.ShapeDtypeStruct((Bd * L * N_KV_HEADS, HEAD_DIM), F32),
                   jax.ShapeDtypeStruct((Bd * L * N_KV_HEADS, HEAD_DIM), F32),
                   jax.ShapeDtypeStruct((Bd * L * CMLP_GROUPS, CMLP_GROUP_DIM), F32)],
        scratch_shapes=[pltpu.VMEM((2, Ts, D_MODEL), F32),
                        pltpu.VMEM((D_MODEL, 2 * D_FF), BF16),
                        pltpu.VMEM((D_FF, D_MODEL), BF16),
                        pltpu.SemaphoreType.DMA((2,))],
        compiler_params=pltpu.CompilerParams(dimension_semantics=("arbitrary",), vmem_limit_bytes=VMEM_LIMIT),
        name="layer_sample",
    )(x_sample.reshape(Bd * L, D_MODEL), *tables_s,
      cache_t(cache_win_k), cache_t(cache_win_v), *ws)

    win_result = lambda a: jnp.transpose(a.reshape(B, N_KV_HEADS, HEAD_DIM, WINDOW), (0, 3, 1, 2))[None]
    return (y_p, y_s.reshape(Bd, L, D_MODEL),
            win_result(kwin), win_result(vwin),
            k_s.reshape(1, Bd, L, N_KV_HEADS, HEAD_DIM),
            v_s.reshape(1, Bd, L, N_KV_HEADS, HEAD_DIM),
            vm_s.reshape(1, Bd, L, CMLP_GROUPS, CMLP_GROUP_DIM))
```

```python
import collections
import functools

import jax
import jax.numpy as jnp
import numpy as np
from jax import lax
from jax.experimental import pallas as pl
from jax.experimental.pallas import tpu as pltpu

D_MODEL = 1024
CHUNK = 64
HEAD_DIM = 64
D_ATTN = 512
D_CMLP = 512
N_HEADS = 8
N_KV_HEADS = 2
GQA_GROUP = 4
D_KV = 128
WINDOW = 128
ROPE_DIM = 16
ROPE_THETA = 500000.0
CMLP_BLOCK = 128
CMLP_GROUPS = 8
CMLP_GROUP_DIM = 64
D_IN = 1792
D_FF = 2816
PAST_LEN = 1024
ALPHA = 2.0 ** 0.25
LN_EPS = 1e-5
NEG_INF = -1e30
LANES = 128
FF_CHUNK = 256
VMEM_LIMIT = 56 * 1024 * 1024
CAST_STEPS = 2
PROMPT_TILE = 512
SAMPLE_SEQS = 4
WGU_ROWS, WD_ROWS = 16, 64
W_SLOTS = 8
W_CHUNKS_PER_PHASE = 8

F32 = jnp.float32
BF16 = jnp.bfloat16


def _layernorm(x, g, b):
    mu = jnp.mean(x, -1, keepdims=True)
    d = x - mu
    var = jnp.mean(d * d, -1, keepdims=True)
    return d * lax.rsqrt(var + LN_EPS) * g + b


def _rmsnorm(x, g):
    ms = jnp.mean(x * x, -1, keepdims=True)
    return x * lax.rsqrt(ms + LN_EPS) * g


def _dot(a, b):
    return jnp.dot(a, b, preferred_element_type=F32)


def _rope(x, c, sa, sb):
    out = []
    for i in range(x.shape[1] // LANES):
        s = x[:, LANES * i:LANES * (i + 1)]
        up = pltpu.roll(s, LANES - ROPE_DIM // 2, 1)
        dn = pltpu.roll(s, ROPE_DIM // 2, 1)
        out.append(s * c + up * sa + dn * sb)
    return out[0] if len(out) == 1 else jnp.concatenate(out, axis=1)


def _row_halves(n):
    return (slice(0, n // 2), slice(n // 2, n))


def _rep4(a):
    ar = pltpu.roll(a, HEAD_DIM, 1)
    first = lax.broadcasted_iota(jnp.int32, a.shape, 1) < HEAD_DIM
    g0 = jnp.where(first, a, ar).astype(BF16)
    g1 = jnp.where(first, ar, a).astype(BF16)
    return [jnp.concatenate([g0, g0], axis=1), jnp.concatenate([g1, g1], axis=1)]


def _attend_windows(windows, first_valid, sinks, q, k, v):
    T = q.shape[0]
    k_rep, v_rep = _rep4(k), _rep4(v)
    head_lane = lax.broadcasted_iota(jnp.int32, (CHUNK, GQA_GROUP * HEAD_DIM), 1) // HEAD_DIM
    key_idx = lax.broadcasted_iota(jnp.int32, (GQA_GROUP * CHUNK, WINDOW + CHUNK), 1)
    ao_rows = []
    for ci in range(T // CHUNK):
        ao_g = []
        for g in range(N_KV_HEADS):
            qc = q[CHUNK * ci:CHUNK * (ci + 1), 256 * g:256 * (g + 1)]
            qm = jnp.concatenate([jnp.where(head_lane == h, qc, 0.0) for h in range(GQA_GROUP)], axis=0).astype(BF16)
            kw, vw = windows(k_rep, v_rep, g, ci)
            s = lax.dot_general(qm, kw, (((1,), (1,)), ((), ())), preferred_element_type=F32)
            fv = first_valid(ci)
            if fv is not None:
                s = jnp.where(key_idx >= fv, s, NEG_INF)
            ps = []
            for h in range(GQA_GROUP):
                sh = s[CHUNK * h:CHUNK * (h + 1)]
                sink = sinks[GQA_GROUP * g + h]
                m = jnp.maximum(jnp.max(sh, -1, keepdims=True), sink)
                p = jnp.exp(sh - m)
                denom = jnp.sum(p, -1, keepdims=True) + jnp.exp(sink - m)
                ps.append(p * (1.0 / denom))
            pm = jnp.concatenate(ps, axis=0).astype(BF16)
            r = _dot(pm, vw)
            o = jnp.where(head_lane == 0, r[0:CHUNK], 0.0)
            for h in range(1, GQA_GROUP):
                o = o + jnp.where(head_lane == h, r[CHUNK * h:CHUNK * (h + 1)], 0.0)
            ao_g.append(o)
        ao_rows.append(jnp.concatenate(ao_g, axis=1))
        yield
    return jnp.concatenate(ao_rows, axis=0), (k_rep, v_rep)


def _mixer(x, c, sa, sb, attend, LB, w):
    T = x.shape[0]
    nb = T // LB
    xb = x.astype(BF16)
    uv = jax.nn.gelu(_dot(xb, w.w_in[:, D_ATTN + 2 * D_KV:D_IN]))
    yield
    q = _dot(xb, w.w_in[:, 0:D_ATTN])
    kv = _dot(xb, w.w_in[:, D_ATTN:D_ATTN + 2 * D_KV])

    k = _rope(kv[:, :D_KV], c, sa, sb)
    v = kv[:, D_KV:]
    q = _rope(q, c, sa, sb) * (HEAD_DIM ** -0.5)

    yield
    u = uv[:, :D_CMLP]
    vm = _layernorm(uv[:, D_CMLP:], w.lnv_g[...], w.lnv_b[...])

    lo_mask = lax.broadcasted_iota(jnp.int32, (T, LANES), 1) < CMLP_GROUP_DIM
    pad = [jnp.zeros((CMLP_BLOCK - LB, LANES), BF16)] if LB < CMLP_BLOCK else []
    gate_slabs = []
    for p in range(D_CMLP // LANES):
        slab = vm[:, LANES * p:LANES * (p + 1)]
        lo = jnp.where(lo_mask, slab, 0.0).astype(BF16)
        hi = jnp.where(lo_mask, 0.0, slab).astype(BF16)
        rhs = jnp.concatenate(
            [jnp.concatenate([lo[LB * b:LB * (b + 1)]] + pad + [hi[LB * b:LB * (b + 1)]] + pad, axis=0)
             for b in range(nb)], axis=1)
        o = _dot(w.wpair[p, 0:LB, :], rhs)
        gate_slabs.append(jnp.concatenate([o[:, LANES * b:LANES * (b + 1)] for b in range(nb)], axis=0))
    s_gate = jnp.concatenate(gate_slabs, axis=1) + jnp.concatenate([w.bias[0:LB, :]] * nb, axis=0)
    co = u * s_gate
    yield

    ao, carry = yield from attend(q, k, v)

    an = _rmsnorm(ao, w.nga[...]).astype(BF16)
    cn = _rmsnorm(co, w.ngc[...]).astype(BF16)
    h1 = []
    for r in _row_halves(T):
        mix = _dot(an[r], w.w_out[0:D_ATTN, :]) + _dot(cn[r], w.w_out[D_ATTN:, :])
        yield
        h1.append(_layernorm(ALPHA * x[r] + mix, w.ln1g[...], w.ln1b[...]))
    return jnp.concatenate(h1, axis=0), k, v, vm, carry


def _ffn(h1, w):
    hb = h1.astype(BF16)
    acts = []
    for lo_c in range(0, D_FF, FF_CHUNK):
        hi_c = min(lo_c + FF_CHUNK, D_FF)
        gt = _dot(hb, w.wgu[:, lo_c:hi_c])
        up = _dot(hb, w.wgu[:, D_FF + lo_c:D_FF + hi_c])
        acts.append((gt * jax.nn.sigmoid(gt) * up).astype(BF16))
        yield
    act = jnp.concatenate(acts, axis=1)
    y = []
    for r in _row_halves(h1.shape[0]):
        f = _dot(act[r], w.wd[...])
        yield
        y.append(_layernorm(ALPHA * h1[r] + f, w.ln2g[...], w.ln2b[...]))
    return jnp.concatenate(y, axis=0)


def _run(*gens):
    results = [None] * len(gens)
    live = list(range(len(gens)))
    while live:
        for i in list(live):
            try:
                next(gens[i])
            except StopIteration as done:
                results[i] = done.value
                live.remove(i)
    return results


_WEIGHT_NAMES = ("w_in", "lnv_g", "lnv_b", "sinks", "wpair", "bias", "nga", "ngc", "w_out", "ln1g", "ln1b",
                 "wgu", "wd", "ln2g", "ln2b")
_Weights = collections.namedtuple("_Weights", _WEIGHT_NAMES)
N_W = len(_WEIGHT_NAMES)


def _pipelined_step(step, n_tiles, h1_ref, w, make_mixer, store_y, store_mixer, fill_extra=None):
    cur = step % 2
    prev = 1 - cur

    def phase(do_mixer, do_ffn, extra=None):
        gens = (([_ffn(h1_ref[prev], w)] if do_ffn else []) + ([extra()] if extra else [])
                + ([make_mixer()] if do_mixer else []))
        res = _run(*gens)
        if do_ffn:
            store_y(res[0])
        if do_mixer:
            h1_ref[cur] = res[-1][0]
            store_mixer(*res[-1][1:])

    pl.when(step == 0)(functools.partial(phase, True, False, fill_extra))
    pl.when(jnp.logical_and(step > 0, step < n_tiles))(functools.partial(phase, True, True))
    pl.when(step == n_tiles)(functools.partial(phase, False, True))


def _load_convert(jobs, then):
    plan = [(job, c) for job in jobs for c in range(job[0].shape[0] // job[4])]
    depth = W_SLOTS - 1

    def copy(i):
        (src, _, stage, sem, rows), c = plan[i]
        return pltpu.make_async_copy(src.at[pl.ds(c * rows, rows), :], stage.at[c % W_SLOTS], sem.at[c % W_SLOTS])

    for i in range(min(depth, len(plan))):
        copy(i).start(priority=i % 2)
    for i, ((_, dst, stage, _, rows), c) in enumerate(plan):
        copy(i).wait()
        dst[c * rows:(c + 1) * rows, :] = stage[c % W_SLOTS].astype(BF16)
        if i + depth < len(plan):
            copy(i + depth).start(priority=(i + depth) % 2)
        if (i + 1) % W_CHUNKS_PER_PHASE == 0:
            yield
    then()


def _prompt_body(T, tiles_per_seq, n_tiles, *refs):
    x_ref, c_ref, sa_ref, sb_ref = refs[:4]
    w = _Weights(*refs[4:4 + N_W])
    (y_ref, k_out_ref, v_out_ref, w_out_out, wgu_out, wd_out, hk_ref, hv_ref, h1_ref,
     w_out_v, wgu_v, wd_v, stage_gu, stage_d, sem_gu, sem_d, sem_out) = refs[4 + N_W:]
    step = pl.program_id(0)

    out_copies = (pltpu.make_async_copy(w_out_v, w_out_out, sem_out.at[0]),
                  pltpu.make_async_copy(wgu_v, wgu_out, sem_out.at[1]),
                  pltpu.make_async_copy(wd_v, wd_out, sem_out.at[2]))

    def start_out_copies():
        for cp in out_copies:
            cp.start()

    def convert_weights(w_hbm=w):
        jobs = [(w_hbm.w_out, w_out_v, stage_d, sem_d, WD_ROWS), (w_hbm.wgu, wgu_v, stage_gu, sem_gu, WGU_ROWS),
                (w_hbm.wd, wd_v, stage_d, sem_d, WD_ROWS)]
        return _load_convert(jobs, start_out_copies)

    @pl.when(step == n_tiles)
    def _():
        for cp in out_copies:
            cp.wait()

    w = w._replace(w_out=w_out_v, wgu=wgu_v, wd=wd_v)
    j = step % tiles_per_seq
    rd = j % 2
    wr = 1 - rd

    @pl.when(j == 0)
    def _():
        hk_ref[0] = jnp.zeros(hk_ref.shape[1:], BF16)
        hv_ref[0] = jnp.zeros(hv_ref.shape[1:], BF16)

    def first_valid(ci):
        if ci >= WINDOW // CHUNK:
            return None
        return jnp.where(j == 0, (WINDOW // CHUNK - ci) * CHUNK, 0)

    def make_mixer():
        hist_k = [hk_ref[rd, g] for g in range(N_KV_HEADS)]
        hist_v = [hv_ref[rd, g] for g in range(N_KV_HEADS)]

        def windows(k_rep, v_rep, g, ci):
            k_all = jnp.concatenate([hist_k[g], k_rep[g]], axis=0)
            v_all = jnp.concatenate([hist_v[g], v_rep[g]], axis=0)
            lo = CHUNK * ci
            return k_all[lo:lo + WINDOW + CHUNK], v_all[lo:lo + WINDOW + CHUNK]

        attend = functools.partial(_attend_windows, windows, first_valid, w.sinks)
        return _mixer(x_ref[...], c_ref[...], sa_ref[...], sb_ref[...], attend, CMLP_BLOCK, w)

    def store_y(y):
        y_ref[...] = y

    def store_mixer(k, v, _, carry):
        k_out_ref[0] = jnp.transpose(k[T - WINDOW:, :])
        v_out_ref[0] = jnp.transpose(v[T - WINDOW:, :])
        k_rep, v_rep = carry
        for g in range(N_KV_HEADS):
            hk_ref[wr, g] = k_rep[g][T - WINDOW:]
            hv_ref[wr, g] = v_rep[g][T - WINDOW:]

    _pipelined_step(step, n_tiles, h1_ref, w, make_mixer, store_y, store_mixer, fill_extra=convert_weights)


def _scatter_heads(out_ref, a):
    n = a.shape[1] // HEAD_DIM
    for i in range(n):
        out_ref[pl.ds(i, a.shape[0], stride=n), :] = a[:, HEAD_DIM * i:HEAD_DIM * (i + 1)]


def _cache_rep4(ref):
    rows = [jnp.transpose(ref[D_KV * s:D_KV * (s + 1), :]) for s in range(ref.shape[0] // D_KV)]
    return _rep4(jnp.concatenate(rows, axis=0))


def _sample_body(n_tiles, *refs):
    x_ref, c_ref, sa_ref, sb_ref, ck_ref, cv_ref = refs[:6]
    w = _Weights(*refs[6:6 + N_W])
    y_ref, k_out_ref, v_out_ref, vm_out_ref, h1_ref, wgu_v, wd_v, sem = refs[6 + N_W:]
    step = pl.program_id(0)

    fetches = (pltpu.make_async_copy(w.wgu, wgu_v, sem.at[0]), pltpu.make_async_copy(w.wd, wd_v, sem.at[1]))

    @pl.when(step == 0)
    def _():
        for cp in fetches:
            cp.start()

    @pl.when(step == 1)
    def _():
        for cp in fetches:
            cp.wait()

    w = w._replace(wgu=wgu_v, wd=wd_v)

    def make_mixer():
        ck_rep, cv_rep = _cache_rep4(ck_ref), _cache_rep4(cv_ref)

        def windows(k_rep, v_rep, g, ci):
            cat = lambda cache, new: jnp.concatenate(
                [cache[g][WINDOW * ci:WINDOW * (ci + 1)], new[g][CHUNK * ci:CHUNK * (ci + 1)]], axis=0)
            return cat(ck_rep, k_rep), cat(cv_rep, v_rep)

        attend = functools.partial(_attend_windows, windows, lambda ci: None, w.sinks)
        return _mixer(x_ref[...], c_ref[...], sa_ref[...], sb_ref[...], attend, CHUNK, w)

    def store_y(y):
        y_ref[...] = y

    def store_mixer(k, v, vm, *_):
        _scatter_heads(k_out_ref, k)
        _scatter_heads(v_out_ref, v)
        _scatter_heads(vm_out_ref, vm)

    _pipelined_step(step, n_tiles, h1_ref, w, make_mixer, store_y, store_mixer)


def _cast_body(*refs):
    n = len(refs) // 2
    for src, dst in zip(refs[:n], refs[n:]):
        dst[...] = src[...].astype(BF16)


def _to_bf16(*ws):
    specs = [pl.BlockSpec((a.shape[0] // CAST_STEPS, a.shape[1]), lambda i: (i, 0)) for a in ws]
    return pl.pallas_call(
        _cast_body,
        grid=(CAST_STEPS,),
        in_specs=specs,
        out_specs=specs,
        out_shape=[jax.ShapeDtypeStruct(a.shape, BF16) for a in ws],
        compiler_params=pltpu.CompilerParams(dimension_semantics=("arbitrary",), vmem_limit_bytes=VMEM_LIMIT),
        name="cast_weights",
    )(*ws)


def _const_spec(shape):
    nd = len(shape)
    return pl.BlockSpec(shape, lambda *_: (0,) * nd)


def _weight_specs(ws, in_hbm=()):
    spec = lambda name, a: (pl.BlockSpec(memory_space=pltpu.SMEM) if name == "sinks" else
                            pl.BlockSpec(memory_space=pl.ANY) if name in in_hbm else _const_spec(a.shape))
    return [spec(name, a) for name, a in zip(_WEIGHT_NAMES, ws)]


def _rope_tables(pos, reps=1):
    half = ROPE_DIM // 2
    inv = np.power(ROPE_THETA, -np.arange(half, dtype=np.float64) * (2.0 / ROPE_DIM))
    ang = np.asarray(pos, np.float64)[:, None] * inv[None, :]
    cos, sin = np.cos(ang), np.sin(ang)
    n = ang.shape[0]
    pad = HEAD_DIM - ROPE_DIM
    c = np.concatenate([cos, cos, np.ones((n, pad))], 1)
    sa = np.concatenate([-sin, np.zeros((n, half + pad))], 1)
    sb = np.concatenate([np.zeros((n, half)), sin, np.zeros((n, pad))], 1)
    return [jnp.asarray(np.tile(t, (reps, LANES // HEAD_DIM)), F32) for t in (c, sa, sb)]


def _gate_params(w_s, b_s, lb):
    i = np.arange(lb)
    mask = (i[None, :] // CHUNK) <= (i[:, None] // CHUNK)
    w = jnp.where(mask[None], w_s[:, :lb, :lb], 0.0).reshape(CMLP_GROUPS // 2, 2, lb, lb)
    wpair = jnp.transpose(w, (0, 2, 1, 3)).reshape(CMLP_GROUPS // 2, lb, 2 * lb).astype(BF16)
    bias = jnp.repeat(jnp.transpose(b_s[:, :lb]), CMLP_GROUP_DIM, axis=1)
    return wpair, bias


def kernel(x_prompt, x_sample, cache_win_k, cache_win_v, w_in, ln_v_g, ln_v_b, attn_sinks, w_spatial, b_spatial,
           norm_attn_g, norm_cmlp_g, w_out, ln1_g, ln1_b, w_gate_up, w_down, ln2_g, ln2_b):
    B, S, _ = x_prompt.shape
    Bd, L, _ = x_sample.shape
    assert cache_win_k.shape[2] == WINDOW and L == CHUNK and w_in.shape[0] == 1

    row = lambda a: a[0].reshape(1, -1)
    w_in_b, = _to_bf16(w_in[0])

    wpair, bias = _gate_params(w_spatial[0], b_spatial[0], CMLP_BLOCK)

    def weights(w_out_, wgu, wd):
        return _Weights(w_in=w_in_b, lnv_g=row(ln_v_g), lnv_b=row(ln_v_b), sinks=attn_sinks[0],
                        wpair=wpair, bias=bias, nga=row(norm_attn_g), ngc=row(norm_cmlp_g),
                        w_out=w_out_, ln1g=row(ln1_g), ln1b=row(ln1_b),
                        wgu=wgu, wd=wd, ln2g=row(ln2_g), ln2b=row(ln2_b))

    T = PROMPT_TILE
    tps = S // T
    n_tiles = B * tps
    wp = weights(w_out[0], w_gate_up[0], w_down[0])
    tables = _rope_tables(np.arange(S))
    mix_tile = lambda s: jnp.minimum(s, n_tiles - 1)
    ffn_tile = lambda s: jnp.maximum(s - 1, 0)
    tab = pl.BlockSpec((T, LANES), lambda s: (mix_tile(s) % tps, 0))
    win = pl.BlockSpec((1, WINDOW, D_KV), lambda s: (mix_tile(s) // tps, 0, 0))
    hbm = pl.BlockSpec(memory_space=pl.ANY)
    y_p, kwin, vwin, w_out_b, wgu_b, wd_b = pl.pallas_call(
        functools.partial(_prompt_body, T, tps, n_tiles),
        grid=(n_tiles + 1,),
        in_specs=[pl.BlockSpec((T, D_MODEL), lambda s: (mix_tile(s), 0)), tab, tab, tab]
        + _weight_specs(wp, in_hbm=("w_out", "wgu", "wd")),
        out_specs=[pl.BlockSpec((T, D_MODEL), lambda s: (ffn_tile(s), 0)), win, win, hbm, hbm, hbm],
        out_shape=[jax.ShapeDtypeStruct((B * S, D_MODEL), F32),
                   jax.ShapeDtypeStruct((B, WINDOW, D_KV), F32),
                   jax.ShapeDtypeStruct((B, WINDOW, D_KV), F32),
                   jax.ShapeDtypeStruct((D_MODEL, D_MODEL), BF16),
                   jax.ShapeDtypeStruct((D_MODEL, 2 * D_FF), BF16),
                   jax.ShapeDtypeStruct((D_FF, D_MODEL), BF16)],
        scratch_shapes=[pltpu.VMEM((2, N_KV_HEADS, WINDOW, 2 * LANES), BF16),
                        pltpu.VMEM((2, N_KV_HEADS, WINDOW, 2 * LANES), BF16),
                        pltpu.VMEM((2, T, D_MODEL), F32),
                        pltpu.VMEM((D_MODEL, D_MODEL), BF16),
                        pltpu.VMEM((D_MODEL, 2 * D_FF), BF16),
                        pltpu.VMEM((D_FF, D_MODEL), BF16),
                        pltpu.VMEM((W_SLOTS, WGU_ROWS, 2 * D_FF), F32),
                        pltpu.VMEM((W_SLOTS, WD_ROWS, D_MODEL), F32),
                        pltpu.SemaphoreType.DMA((W_SLOTS,)),
                        pltpu.SemaphoreType.DMA((W_SLOTS,)),
                        pltpu.SemaphoreType.DMA((3,))],
        compiler_params=pltpu.CompilerParams(dimension_semantics=("arbitrary",), vmem_limit_bytes=VMEM_LIMIT),
        name="layer_prompt",
    )(x_prompt.reshape(B * S, D_MODEL), *tables, *wp)
    y_p = y_p.reshape(B, S, D_MODEL)

    NS = SAMPLE_SEQS
    Ts = NS * L
    ws = weights(w_out_b, wgu_b, wd_b)
    tables_s = _rope_tables(PAST_LEN + np.arange(L), reps=NS)
    n_st = Bd // NS
    mix_rows = lambda n, width: pl.BlockSpec((n, width), lambda s: (jnp.minimum(s, n_st - 1), 0))
    cache_rows = NS * D_KV
    cache_t = lambda a: jnp.transpose(a[0], (0, 2, 3, 1)).reshape(Bd * D_KV, WINDOW)
    y_s, k_s, v_s, vm_s = pl.pallas_call(
        functools.partial(_sample_body, n_st),
        grid=(n_st + 1,),
        in_specs=[mix_rows(Ts, D_MODEL)] + [_const_spec(t.shape) for t in tables_s]
        + [mix_rows(cache_rows, WINDOW), mix_rows(cache_rows, WINDOW)] + _weight_specs(ws, in_hbm=("wgu", "wd")),
        out_specs=[pl.BlockSpec((Ts, D_MODEL), lambda s: (jnp.maximum(s - 1, 0), 0)),
                   mix_rows(Ts * N_KV_HEADS, HEAD_DIM), mix_rows(Ts * N_KV_HEADS, HEAD_DIM),
                   mix_rows(Ts * CMLP_GROUPS, CMLP_GROUP_DIM)],
        out_shape=[jax.ShapeDtypeStruct((Bd * L, D_MODEL), F32),
                   jax.ShapeDtypeStruct((Bd * L * N_KV_HEADS, HEAD_DIM), F32),
                   jax.ShapeDtypeStruct((Bd * L * N_KV_HEADS, HEAD_DIM), F32),
                   jax.ShapeDtypeStruct((Bd * L * CMLP_GROUPS, CMLP_GROUP_DIM), F32)],
        scratch_shapes=[pltpu.VMEM((2, Ts, D_MODEL), F32),
                        pltpu.VMEM((D_MODEL, 2 * D_FF), BF16),
                        pltpu.VMEM((D_FF, D_MODEL), BF16),
                        pltpu.SemaphoreType.DMA((2,))],
        compiler_params=pltpu.CompilerParams(dimension_semantics=("arbitrary",), vmem_limit_bytes=VMEM_LIMIT),
        name="layer_sample",
    )(x_sample.reshape(Bd * L, D_MODEL), *tables_s,
      cache_t(cache_win_k), cache_t(cache_win_v), *ws)

    win_result = lambda a: jnp.transpose(a.reshape(B, N_KV_HEADS, HEAD_DIM, WINDOW), (0, 3, 1, 2))[None]
    return (y_p, y_s.reshape(Bd, L, D_MODEL),
            win_result(kwin), win_result(vwin),
            k_s.reshape(1, Bd, L, N_KV_HEADS, HEAD_DIM),
            v_s.reshape(1, Bd, L, N_KV_HEADS, HEAD_DIM),
            vm_s.reshape(1, Bd, L, CMLP_GROUPS, CMLP_GROUP_DIM))
```

```python
import collections
import functools

import jax
import jax.numpy as jnp
import numpy as np
from jax import lax
from jax.experimental import pallas as pl
from jax.experimental.pallas import tpu as pltpu

D_MODEL = 1024
CHUNK = 64
HEAD_DIM = 64
D_ATTN = 512
D_CMLP = 512
N_HEADS = 8
N_KV_HEADS = 2
GQA_GROUP = 4
D_KV = 128
WINDOW = 128
ROPE_DIM = 16
ROPE_THETA = 500000.0
CMLP_BLOCK = 128
CMLP_GROUPS = 8
CMLP_GROUP_DIM = 64
D_IN = 1792
D_FF = 2816
PAST_LEN = 1024
ALPHA = 2.0 ** 0.25
LN_EPS = 1e-5
NEG_INF = -1e30
LANES = 128
FF_CHUNK = 256
VMEM_LIMIT = 56 * 1024 * 1024
CAST_STEPS = 1
PROMPT_TILE = 512
SAMPLE_SEQS = 4
WGU_ROWS, WD_ROWS = 16, 64
W_SLOTS = 8
W_CHUNKS_PER_PHASE = 8

F32 = jnp.float32
BF16 = jnp.bfloat16


def _layernorm(x, g, b):
    mu = jnp.mean(x, -1, keepdims=True)
    d = x - mu
    var = jnp.mean(d * d, -1, keepdims=True)
    return d * lax.rsqrt(var + LN_EPS) * g + b


def _rmsnorm(x, g):
    ms = jnp.mean(x * x, -1, keepdims=True)
    return x * lax.rsqrt(ms + LN_EPS) * g


def _dot(a, b):
    return jnp.dot(a, b, preferred_element_type=F32)


def _rope(x, c, sa, sb):
    out = []
    for i in range(x.shape[1] // LANES):
        s = x[:, LANES * i:LANES * (i + 1)]
        up = pltpu.roll(s, LANES - ROPE_DIM // 2, 1)
        dn = pltpu.roll(s, ROPE_DIM // 2, 1)
        out.append(s * c + up * sa + dn * sb)
    return out[0] if len(out) == 1 else jnp.concatenate(out, axis=1)


def _row_halves(n):
    return (slice(0, n // 2), slice(n // 2, n))


def _rep4(a):
    ar = pltpu.roll(a, HEAD_DIM, 1)
    first = lax.broadcasted_iota(jnp.int32, a.shape, 1) < HEAD_DIM
    g0 = jnp.where(first, a, ar).astype(BF16)
    g1 = jnp.where(first, ar, a).astype(BF16)
    return [jnp.concatenate([g0, g0], axis=1), jnp.concatenate([g1, g1], axis=1)]


def _attend_windows(windows, first_valid, sinks, q, k, v):
    T = q.shape[0]
    k_rep, v_rep = _rep4(k), _rep4(v)
    head_lane = lax.broadcasted_iota(jnp.int32, (CHUNK, GQA_GROUP * HEAD_DIM), 1) // HEAD_DIM
    key_idx = lax.broadcasted_iota(jnp.int32, (GQA_GROUP * CHUNK, WINDOW + CHUNK), 1)
    ao_rows = []
    for ci in range(T // CHUNK):
        ao_g = []
        for g in range(N_KV_HEADS):
            qc = q[CHUNK * ci:CHUNK * (ci + 1), 256 * g:256 * (g + 1)]
            qm = jnp.concatenate([jnp.where(head_lane == h, qc, 0.0) for h in range(GQA_GROUP)], axis=0).astype(BF16)
            kw, vw = windows(k_rep, v_rep, g, ci)
            s = lax.dot_general(qm, kw, (((1,), (1,)), ((), ())), preferred_element_type=F32)
            fv = first_valid(ci)
            if fv is not None:
                s = jnp.where(key_idx >= fv, s, NEG_INF)
            ps = []
            for h in range(GQA_GROUP):
                sh = s[CHUNK * h:CHUNK * (h + 1)]
                sink = sinks[GQA_GROUP * g + h]
                m = jnp.maximum(jnp.max(sh, -1, keepdims=True), sink)
                p = jnp.exp(sh - m)
                denom = jnp.sum(p, -1, keepdims=True) + jnp.exp(sink - m)
                ps.append(p * (1.0 / denom))
            pm = jnp.concatenate(ps, axis=0).astype(BF16)
            r = _dot(pm, vw)
            o = jnp.where(head_lane == 0, r[0:CHUNK], 0.0)
            for h in range(1, GQA_GROUP):
                o = o + jnp.where(head_lane == h, r[CHUNK * h:CHUNK * (h + 1)], 0.0)
            ao_g.append(o)
        ao_rows.append(jnp.concatenate(ao_g, axis=1))
        yield
    return jnp.concatenate(ao_rows, axis=0), (k_rep, v_rep)


def _mixer(x, c, sa, sb, attend, LB, w):
    T = x.shape[0]
    nb = T // LB
    xb = x.astype(BF16)
    uv = jax.nn.gelu(_dot(xb, w.w_in[:, D_ATTN + 2 * D_KV:D_IN]))
    yield
    q = _dot(xb, w.w_in[:, 0:D_ATTN])
    kv = _dot(xb, w.w_in[:, D_ATTN:D_ATTN + 2 * D_KV])

    k = _rope(kv[:, :D_KV], c, sa, sb)
    v = kv[:, D_KV:]
    q = _rope(q, c, sa, sb) * (HEAD_DIM ** -0.5)

    yield
    u = uv[:, :D_CMLP]
    vm = _layernorm(uv[:, D_CMLP:], w.lnv_g[...], w.lnv_b[...])

    lo_mask = lax.broadcasted_iota(jnp.int32, (T, LANES), 1) < CMLP_GROUP_DIM
    pad = [jnp.zeros((CMLP_BLOCK - LB, LANES), BF16)] if LB < CMLP_BLOCK else []
    gate_slabs = []
    for p in range(D_CMLP // LANES):
        slab = vm[:, LANES * p:LANES * (p + 1)]
        lo = jnp.where(lo_mask, slab, 0.0).astype(BF16)
        hi = jnp.where(lo_mask, 0.0, slab).astype(BF16)
        rhs = jnp.concatenate(
            [jnp.concatenate([lo[LB * b:LB * (b + 1)]] + pad + [hi[LB * b:LB * (b + 1)]] + pad, axis=0)
             for b in range(nb)], axis=1)
        w_pair = jnp.concatenate([w.wpair[2 * p, 0:LB, :], w.wpair[2 * p + 1, 0:LB, :]], axis=1)
        o = _dot(w_pair, rhs)
        gate_slabs.append(jnp.concatenate([o[:, LANES * b:LANES * (b + 1)] for b in range(nb)], axis=0))
    s_gate = jnp.concatenate(gate_slabs, axis=1) + jnp.concatenate([w.bias[0:LB, :]] * nb, axis=0)
    co = u * s_gate
    yield

    ao, carry = yield from attend(q, k, v)

    an = _rmsnorm(ao, w.nga[...]).astype(BF16)
    cn = _rmsnorm(co, w.ngc[...]).astype(BF16)
    h1 = []
    for r in _row_halves(T):
        mix = _dot(an[r], w.w_out[0:D_ATTN, :]) + _dot(cn[r], w.w_out[D_ATTN:, :])
        yield
        h1.append(_layernorm(ALPHA * x[r] + mix, w.ln1g[...], w.ln1b[...]))
    return jnp.concatenate(h1, axis=0), k, v, vm, carry


def _ffn(h1, w):
    hb = h1.astype(BF16)
    acts = []
    for lo_c in range(0, D_FF, FF_CHUNK):
        hi_c = min(lo_c + FF_CHUNK, D_FF)
        gt = _dot(hb, w.wgu[:, lo_c:hi_c])
        up = _dot(hb, w.wgu[:, D_FF + lo_c:D_FF + hi_c])
        acts.append((gt * jax.nn.sigmoid(gt) * up).astype(BF16))
        yield
    act = jnp.concatenate(acts, axis=1)
    y = []
    for r in _row_halves(h1.shape[0]):
        f = _dot(act[r], w.wd[...])
        yield
        y.append(_layernorm(ALPHA * h1[r] + f, w.ln2g[...], w.ln2b[...]))
    return jnp.concatenate(y, axis=0)


def _run(*gens):
    results = [None] * len(gens)
    live = list(range(len(gens)))
    while live:
        for i in list(live):
            try:
                next(gens[i])
            except StopIteration as done:
                results[i] = done.value
                live.remove(i)
    return results


_WEIGHT_NAMES = ("w_in", "lnv_g", "lnv_b", "sinks", "wpair", "bias", "nga", "ngc", "w_out", "ln1g", "ln1b",
                 "wgu", "wd", "ln2g", "ln2b")
_Weights = collections.namedtuple("_Weights", _WEIGHT_NAMES)
N_W = len(_WEIGHT_NAMES)


def _pipelined_step(step, n_tiles, h1_ref, w, make_mixer, store_y, store_mixer, fill_extra=None):
    cur = step % 2
    prev = 1 - cur

    def phase(do_mixer, do_ffn, extra=None):
        gens = (([_ffn(h1_ref[prev], w)] if do_ffn else []) + ([extra()] if extra else [])
                + ([make_mixer()] if do_mixer else []))
        res = _run(*gens)
        if do_ffn:
            store_y(res[0])
        if do_mixer:
            h1_ref[cur] = res[-1][0]
            store_mixer(*res[-1][1:])

    pl.when(step == 0)(functools.partial(phase, True, False, fill_extra))
    pl.when(jnp.logical_and(step > 0, step < n_tiles))(functools.partial(phase, True, True))
    pl.when(step == n_tiles)(functools.partial(phase, False, True))


def _load_convert(jobs, then):
    plan = [(job, c) for job in jobs for c in range(job[0].shape[0] // job[4])]
    depth = W_SLOTS - 1

    def copy(i):
        (src, _, stage, sem, rows), c = plan[i]
        return pltpu.make_async_copy(src.at[pl.ds(c * rows, rows), :], stage.at[c % W_SLOTS], sem.at[c % W_SLOTS])

    for i in range(min(depth, len(plan))):
        copy(i).start(priority=i % 2)
    for i, ((_, dst, stage, _, rows), c) in enumerate(plan):
        copy(i).wait()
        dst[c * rows:(c + 1) * rows, :] = stage[c % W_SLOTS].astype(BF16)
        if i + depth < len(plan):
            copy(i + depth).start(priority=(i + depth) % 2)
        if (i + 1) % W_CHUNKS_PER_PHASE == 0:
            yield
    then()


def _prompt_body(T, tiles_per_seq, n_tiles, *refs):
    x_ref, c_ref, sa_ref, sb_ref = refs[:4]
    w = _Weights(*refs[4:4 + N_W])
    (y_ref, k_out_ref, v_out_ref, w_out_out, wgu_out, wd_out, hk_ref, hv_ref, h1_ref,
     w_out_v, wgu_v, wd_v, stage_gu, stage_d, sem_gu, sem_d, sem_out) = refs[4 + N_W:]
    step = pl.program_id(0)

    out_copies = (pltpu.make_async_copy(w_out_v, w_out_out, sem_out.at[0]),
                  pltpu.make_async_copy(wgu_v, wgu_out, sem_out.at[1]),
                  pltpu.make_async_copy(wd_v, wd_out, sem_out.at[2]))

    def start_out_copies():
        for cp in out_copies:
            cp.start()

    def convert_weights(w_hbm=w):
        jobs = [(w_hbm.w_out, w_out_v, stage_d, sem_d, WD_ROWS), (w_hbm.wgu, wgu_v, stage_gu, sem_gu, WGU_ROWS),
                (w_hbm.wd, wd_v, stage_d, sem_d, WD_ROWS)]
        return _load_convert(jobs, start_out_copies)

    @pl.when(step == n_tiles)
    def _():
        for cp in out_copies:
            cp.wait()

    w = w._replace(w_out=w_out_v, wgu=wgu_v, wd=wd_v)
    j = step % tiles_per_seq
    rd = j % 2
    wr = 1 - rd

    @pl.when(j == 0)
    def _():
        hk_ref[0] = jnp.zeros(hk_ref.shape[1:], BF16)
        hv_ref[0] = jnp.zeros(hv_ref.shape[1:], BF16)

    def first_valid(ci):
        if ci >= WINDOW // CHUNK:
            return None
        return jnp.where(j == 0, (WINDOW // CHUNK - ci) * CHUNK, 0)

    def make_mixer():
        hist_k = [hk_ref[rd, g] for g in range(N_KV_HEADS)]
        hist_v = [hv_ref[rd, g] for g in range(N_KV_HEADS)]

        def windows(k_rep, v_rep, g, ci):
            k_all = jnp.concatenate([hist_k[g], k_rep[g]], axis=0)
            v_all = jnp.concatenate([hist_v[g], v_rep[g]], axis=0)
            lo = CHUNK * ci
            return k_all[lo:lo + WINDOW + CHUNK], v_all[lo:lo + WINDOW + CHUNK]

        attend = functools.partial(_attend_windows, windows, first_valid, w.sinks)
        return _mixer(x_ref[...], c_ref[...], sa_ref[...], sb_ref[...], attend, CMLP_BLOCK, w)

    def store_y(y):
        y_ref[...] = y

    def store_mixer(k, v, _, carry):
        k_out_ref[0] = jnp.transpose(k[T - WINDOW:, :])
        v_out_ref[0] = jnp.transpose(v[T - WINDOW:, :])
        k_rep, v_rep = carry
        for g in range(N_KV_HEADS):
            hk_ref[wr, g] = k_rep[g][T - WINDOW:]
            hv_ref[wr, g] = v_rep[g][T - WINDOW:]

    _pipelined_step(step, n_tiles, h1_ref, w, make_mixer, store_y, store_mixer, fill_extra=convert_weights)


def _scatter_heads(out_ref, a):
    n = a.shape[1] // HEAD_DIM
    for i in range(n):
        out_ref[pl.ds(i, a.shape[0], stride=n), :] = a[:, HEAD_DIM * i:HEAD_DIM * (i + 1)]


def _cache_rep4(ref):
    rows = [jnp.transpose(ref[D_KV * s:D_KV * (s + 1), :]) for s in range(ref.shape[0] // D_KV)]
    return _rep4(jnp.concatenate(rows, axis=0))


def _sample_body(n_tiles, *refs):
    x_ref, c_ref, sa_ref, sb_ref, ck_ref, cv_ref = refs[:6]
    w = _Weights(*refs[6:6 + N_W])
    y_ref, k_out_ref, v_out_ref, vm_out_ref, h1_ref, wgu_v, wd_v, sem = refs[6 + N_W:]
    step = pl.program_id(0)

    fetches = (pltpu.make_async_copy(w.wgu, wgu_v, sem.at[0]), pltpu.make_async_copy(w.wd, wd_v, sem.at[1]))

    @pl.when(step == 0)
    def _():
        for cp in fetches:
            cp.start()

    @pl.when(step == 1)
    def _():
        for cp in fetches:
            cp.wait()

    w = w._replace(wgu=wgu_v, wd=wd_v)

    def make_mixer():
        ck_rep, cv_rep = _cache_rep4(ck_ref), _cache_rep4(cv_ref)

        def windows(k_rep, v_rep, g, ci):
            cat = lambda cache, new: jnp.concatenate(
                [cache[g][WINDOW * ci:WINDOW * (ci + 1)], new[g][CHUNK * ci:CHUNK * (ci + 1)]], axis=0)
            return cat(ck_rep, k_rep), cat(cv_rep, v_rep)

        attend = functools.partial(_attend_windows, windows, lambda ci: None, w.sinks)
        return _mixer(x_ref[...], c_ref[...], sa_ref[...], sb_ref[...], attend, CHUNK, w)

    def store_y(y):
        y_ref[...] = y

    def store_mixer(k, v, vm, *_):
        _scatter_heads(k_out_ref, k)
        _scatter_heads(v_out_ref, v)
        _scatter_heads(vm_out_ref, vm)

    _pipelined_step(step, n_tiles, h1_ref, w, make_mixer, store_y, store_mixer)


def _cast_body(*refs):
    n = len(refs) // 2
    for src, dst in zip(refs[:n], refs[n:]):
        dst[...] = src[...].astype(BF16)


def _to_bf16(*ws):
    specs = [pl.BlockSpec((a.shape[0] // CAST_STEPS, a.shape[1]), lambda i: (i, 0)) for a in ws]
    return pl.pallas_call(
        _cast_body,
        grid=(CAST_STEPS,),
        in_specs=specs,
        out_specs=specs,
        out_shape=[jax.ShapeDtypeStruct(a.shape, BF16) for a in ws],
        compiler_params=pltpu.CompilerParams(dimension_semantics=("arbitrary",), vmem_limit_bytes=VMEM_LIMIT),
        name="cast_weights",
    )(*ws)


def _const_spec(shape):
    nd = len(shape)
    return pl.BlockSpec(shape, lambda *_: (0,) * nd)


def _weight_specs(ws, in_hbm=()):
    spec = lambda name, a: (pl.BlockSpec(memory_space=pltpu.SMEM) if name == "sinks" else
                            pl.BlockSpec(memory_space=pl.ANY) if name in in_hbm else _const_spec(a.shape))
    return [spec(name, a) for name, a in zip(_WEIGHT_NAMES, ws)]


def _rope_tables(pos, reps=1):
    half = ROPE_DIM // 2
    inv = np.power(ROPE_THETA, -np.arange(half, dtype=np.float64) * (2.0 / ROPE_DIM))
    ang = np.asarray(pos, np.float64)[:, None] * inv[None, :]
    cos, sin = np.cos(ang), np.sin(ang)
    n = ang.shape[0]
    pad = HEAD_DIM - ROPE_DIM
    c = np.concatenate([cos, cos, np.ones((n, pad))], 1)
    sa = np.concatenate([-sin, np.zeros((n, half + pad))], 1)
    sb = np.concatenate([np.zeros((n, half)), sin, np.zeros((n, pad))], 1)
    return [jnp.asarray(np.tile(t, (reps, LANES // HEAD_DIM)), F32) for t in (c, sa, sb)]


def _gate_params(w_s, b_s, lb):
    i = np.arange(lb)
    mask = (i[None, :] // CHUNK) <= (i[:, None] // CHUNK)
    wpair = jnp.where(mask[None], w_s[:, :lb, :lb], 0.0).astype(BF16)
    bias = jnp.repeat(jnp.transpose(b_s[:, :lb]), CMLP_GROUP_DIM, axis=1)
    return wpair, bias


def kernel(x_prompt, x_sample, cache_win_k, cache_win_v, w_in, ln_v_g, ln_v_b, attn_sinks, w_spatial, b_spatial,
           norm_attn_g, norm_cmlp_g, w_out, ln1_g, ln1_b, w_gate_up, w_down, ln2_g, ln2_b):
    B, S, _ = x_prompt.shape
    Bd, L, _ = x_sample.shape
    assert cache_win_k.shape[2] == WINDOW and L == CHUNK and w_in.shape[0] == 1

    row = lambda a: a[0].reshape(1, -1)
    w_in_b, = _to_bf16(w_in[0])

    wpair, bias = _gate_params(w_spatial[0], b_spatial[0], CMLP_BLOCK)

    def weights(w_out_, wgu, wd):
        return _Weights(w_in=w_in_b, lnv_g=row(ln_v_g), lnv_b=row(ln_v_b), sinks=attn_sinks[0],
                        wpair=wpair, bias=bias, nga=row(norm_attn_g), ngc=row(norm_cmlp_g),
                        w_out=w_out_, ln1g=row(ln1_g), ln1b=row(ln1_b),
                        wgu=wgu, wd=wd, ln2g=row(ln2_g), ln2b=row(ln2_b))

    T = PROMPT_TILE
    tps = S // T
    n_tiles = B * tps
    wp = weights(w_out[0], w_gate_up[0], w_down[0])
    tables = _rope_tables(np.arange(S))
    mix_tile = lambda s: jnp.minimum(s, n_tiles - 1)
    ffn_tile = lambda s: jnp.maximum(s - 1, 0)
    tab = pl.BlockSpec((T, LANES), lambda s: (mix_tile(s) % tps, 0))
    win = pl.BlockSpec((1, WINDOW, D_KV), lambda s: (mix_tile(s) // tps, 0, 0))
    hbm = pl.BlockSpec(memory_space=pl.ANY)
    y_p, kwin, vwin, w_out_b, wgu_b, wd_b = pl.pallas_call(
        functools.partial(_prompt_body, T, tps, n_tiles),
        grid=(n_tiles + 1,),
        in_specs=[pl.BlockSpec((T, D_MODEL), lambda s: (mix_tile(s), 0)), tab, tab, tab]
        + _weight_specs(wp, in_hbm=("w_out", "wgu", "wd")),
        out_specs=[pl.BlockSpec((T, D_MODEL), lambda s: (ffn_tile(s), 0)), win, win, hbm, hbm, hbm],
        out_shape=[jax.ShapeDtypeStruct((B * S, D_MODEL), F32),
                   jax.ShapeDtypeStruct((B, WINDOW, D_KV), F32),
                   jax.ShapeDtypeStruct((B, WINDOW, D_KV), F32),
                   jax.ShapeDtypeStruct((D_MODEL, D_MODEL), BF16),
                   jax.ShapeDtypeStruct((D_MODEL, 2 * D_FF), BF16),
                   jax.ShapeDtypeStruct((D_FF, D_MODEL), BF16)],
        scratch_shapes=[pltpu.VMEM((2, N_KV_HEADS, WINDOW, 2 * LANES), BF16),
                        pltpu.VMEM((2, N_KV_HEADS, WINDOW, 2 * LANES), BF16),
                        pltpu.VMEM((2, T, D_MODEL), F32),
                        pltpu.VMEM((D_MODEL, D_MODEL), BF16),
                        pltpu.VMEM((D_MODEL, 2 * D_FF), BF16),
                        pltpu.VMEM((D_FF, D_MODEL), BF16),
                        pltpu.VMEM((W_SLOTS, WGU_ROWS, 2 * D_FF), F32),
                        pltpu.VMEM((W_SLOTS, WD_ROWS, D_MODEL), F32),
                        pltpu.SemaphoreType.DMA((W_SLOTS,)),
                        pltpu.SemaphoreType.DMA((W_SLOTS,)),
                        pltpu.SemaphoreType.DMA((3,))],
        compiler_params=pltpu.CompilerParams(dimension_semantics=("arbitrary",), vmem_limit_bytes=VMEM_LIMIT),
        name="layer_prompt",
    )(x_prompt.reshape(B * S, D_MODEL), *tables, *wp)
    y_p = y_p.reshape(B, S, D_MODEL)

    NS = SAMPLE_SEQS
    Ts = NS * L
    ws = weights(w_out_b, wgu_b, wd_b)
    tables_s = _rope_tables(PAST_LEN + np.arange(L), reps=NS)
    n_st = Bd // NS
    mix_rows = lambda n, width: pl.BlockSpec((n, width), lambda s: (jnp.minimum(s, n_st - 1), 0))
    cache_rows = NS * D_KV
    cache_t = lambda a: jnp.transpose(a[0], (0, 2, 3, 1)).reshape(Bd * D_KV, WINDOW)
    y_s, k_s, v_s, vm_s = pl.pallas_call(
        functools.partial(_sample_body, n_st),
        grid=(n_st + 1,),
        in_specs=[mix_rows(Ts, D_MODEL)] + [_const_spec(t.shape) for t in tables_s]
        + [mix_rows(cache_rows, WINDOW), mix_rows(cache_rows, WINDOW)] + _weight_specs(ws, in_hbm=("wgu", "wd")),
        out_specs=[pl.BlockSpec((Ts, D_MODEL), lambda s: (jnp.maximum(s - 1, 0), 0)),
                   mix_rows(Ts * N_KV_HEADS, HEAD_DIM), mix_rows(Ts * N_KV_HEADS, HEAD_DIM),
                   mix_rows(Ts * CMLP_GROUPS, CMLP_GROUP_DIM)],
        out_shape=[jax.ShapeDtypeStruct((Bd * L, D_MODEL), F32),
                   jax.ShapeDtypeStruct((Bd * L * N_KV_HEADS, HEAD_DIM), F32),
                   jax.ShapeDtypeStruct((Bd * L * N_KV_HEADS, HEAD_DIM), F32),
                   jax.ShapeDtypeStruct((Bd * L * CMLP_GROUPS, CMLP_GROUP_DIM), F32)],
        scratch_shapes=[pltpu.VMEM((2, Ts, D_MODEL), F32),
                        pltpu.VMEM((D_MODEL, 2 * D_FF), BF16),
                        pltpu.VMEM((D_FF, D_MODEL), BF16),
                        pltpu.SemaphoreType.DMA((2,))],
        compiler_params=pltpu.CompilerParams(dimension_semantics=("arbitrary",), vmem_limit_bytes=VMEM_LIMIT),
        name="layer_sample",
    )(x_sample.reshape(Bd * L, D_MODEL), *tables_s,
      cache_t(cache_win_k), cache_t(cache_win_v), *ws)

    win_result = lambda a: jnp.transpose(a.reshape(B, N_KV_HEADS, HEAD_DIM, WINDOW), (0, 3, 1, 2))[None]
    return (y_p, y_s.reshape(Bd, L, D_MODEL),
            win_result(kwin), win_result(vwin),
            k_s.reshape(1, Bd, L, N_KV_HEADS, HEAD_DIM),
            v_s.reshape(1, Bd, L, N_KV_HEADS, HEAD_DIM),
            vm_s.reshape(1, Bd, L, CMLP_GROUPS, CMLP_GROUP_DIM))
```

```python
import collections
import functools

import jax
import jax.numpy as jnp
import numpy as np
from jax import lax
from jax.experimental import pallas as pl
from jax.experimental.pallas import tpu as pltpu

D_MODEL = 1024
CHUNK = 64
HEAD_DIM = 64
D_ATTN = 512
D_CMLP = 512
N_HEADS = 8
N_KV_HEADS = 2
GQA_GROUP = 4
D_KV = 128
WINDOW = 128
ROPE_DIM = 16
ROPE_THETA = 500000.0
CMLP_BLOCK = 128
CMLP_GROUPS = 8
CMLP_GROUP_DIM = 64
D_IN = 1792
D_FF = 2816
PAST_LEN = 1024
ALPHA = 2.0 ** 0.25
LN_EPS = 1e-5
NEG_INF = -1e30
LANES = 128
FF_CHUNK = 256
VMEM_LIMIT = 56 * 1024 * 1024
CAST_STEPS = 2
PROMPT_TILE = 512
SAMPLE_SEQS = 4
WGU_ROWS, WD_ROWS = 16, 64
W_SLOTS = 8
W_CHUNKS_PER_PHASE = 8

F32 = jnp.float32
BF16 = jnp.bfloat16


def _layernorm(x, g, b):
    mu = jnp.mean(x, -1, keepdims=True)
    d = x - mu
    var = jnp.mean(d * d, -1, keepdims=True)
    return d * lax.rsqrt(var + LN_EPS) * g + b


def _rmsnorm(x, g):
    ms = jnp.mean(x * x, -1, keepdims=True)
    return x * lax.rsqrt(ms + LN_EPS) * g


def _dot(a, b):
    return jnp.dot(a, b, preferred_element_type=F32)


def _rope(x, c, sa, sb):
    out = []
    for i in range(x.shape[1] // LANES):
        s = x[:, LANES * i:LANES * (i + 1)]
        up = pltpu.roll(s, LANES - ROPE_DIM // 2, 1)
        dn = pltpu.roll(s, ROPE_DIM // 2, 1)
        out.append(s * c + up * sa + dn * sb)
    return out[0] if len(out) == 1 else jnp.concatenate(out, axis=1)


def _row_halves(n):
    return (slice(0, n // 2), slice(n // 2, n))


def _rep4(a):
    ar = pltpu.roll(a, HEAD_DIM, 1)
    first = lax.broadcasted_iota(jnp.int32, a.shape, 1) < HEAD_DIM
    g0 = jnp.where(first, a, ar).astype(BF16)
    g1 = jnp.where(first, ar, a).astype(BF16)
    return [jnp.concatenate([g0, g0], axis=1), jnp.concatenate([g1, g1], axis=1)]


def _attend_windows(windows, first_valid, sinks, q, k, v):
    T = q.shape[0]
    k_rep, v_rep = _rep4(k), _rep4(v)
    head_lane = lax.broadcasted_iota(jnp.int32, (CHUNK, GQA_GROUP * HEAD_DIM), 1) // HEAD_DIM
    key_idx = lax.broadcasted_iota(jnp.int32, (GQA_GROUP * CHUNK, WINDOW + CHUNK), 1)
    ao_rows = []
    for ci in range(T // CHUNK):
        ao_g = []
        for g in range(N_KV_HEADS):
            qc = q[CHUNK * ci:CHUNK * (ci + 1), 256 * g:256 * (g + 1)]
            qm = jnp.concatenate([jnp.where(head_lane == h, qc, 0.0) for h in range(GQA_GROUP)], axis=0).astype(BF16)
            kw, vw = windows(k_rep, v_rep, g, ci)
            s = lax.dot_general(qm, kw, (((1,), (1,)), ((), ())), preferred_element_type=F32)
            fv = first_valid(ci)
            if fv is not None:
                s = jnp.where(key_idx >= fv, s, NEG_INF)
            ps = []
            for h in range(GQA_GROUP):
                sh = s[CHUNK * h:CHUNK * (h + 1)]
                sink = sinks[GQA_GROUP * g + h]
                m = jnp.maximum(jnp.max(sh, -1, keepdims=True), sink)
                p = jnp.exp(sh - m)
                denom = jnp.sum(p, -1, keepdims=True) + jnp.exp(sink - m)
                ps.append(p * (1.0 / denom))
            pm = jnp.concatenate(ps, axis=0).astype(BF16)
            r = _dot(pm, vw)
            o = jnp.where(head_lane == 0, r[0:CHUNK], 0.0)
            for h in range(1, GQA_GROUP):
                o = o + jnp.where(head_lane == h, r[CHUNK * h:CHUNK * (h + 1)], 0.0)
            ao_g.append(o)
        ao_rows.append(jnp.concatenate(ao_g, axis=1))
        yield
    return jnp.concatenate(ao_rows, axis=0), (k_rep, v_rep)


def _mixer(x, c, sa, sb, attend, LB, w):
    T = x.shape[0]
    nb = T // LB
    xb = x.astype(BF16)
    uv = jax.nn.gelu(_dot(xb, w.w_in[:, D_ATTN + 2 * D_KV:D_IN]))
    yield
    q = _dot(xb, w.w_in[:, 0:D_ATTN])
    kv = _dot(xb, w.w_in[:, D_ATTN:D_ATTN + 2 * D_KV])

    k = _rope(kv[:, :D_KV], c, sa, sb)
    v = kv[:, D_KV:]
    q = _rope(q, c, sa, sb) * (HEAD_DIM ** -0.5)

    yield
    u = uv[:, :D_CMLP]
    vm = _layernorm(uv[:, D_CMLP:], w.lnv_g[...], w.lnv_b[...])

    lo_mask = lax.broadcasted_iota(jnp.int32, (T, LANES), 1) < CMLP_GROUP_DIM
    pad = [jnp.zeros((CMLP_BLOCK - LB, LANES), BF16)] if LB < CMLP_BLOCK else []
    gate_slabs = []
    for p in range(D_CMLP // LANES):
        slab = vm[:, LANES * p:LANES * (p + 1)]
        lo = jnp.where(lo_mask, slab, 0.0).astype(BF16)
        hi = jnp.where(lo_mask, 0.0, slab).astype(BF16)
        rhs = jnp.concatenate(
            [jnp.concatenate([lo[LB * b:LB * (b + 1)]] + pad + [hi[LB * b:LB * (b + 1)]] + pad, axis=0)
             for b in range(nb)], axis=1)
        w_pair = jnp.concatenate([w.wpair[2 * p, 0:LB, :], w.wpair[2 * p + 1, 0:LB, :]], axis=1)
        o = _dot(w_pair, rhs)
        gate_slabs.append(jnp.concatenate([o[:, LANES * b:LANES * (b + 1)] for b in range(nb)], axis=0))
    s_gate = jnp.concatenate(gate_slabs, axis=1) + jnp.concatenate([w.bias[0:LB, :]] * nb, axis=0)
    co = u * s_gate
    yield

    ao, carry = yield from attend(q, k, v)

    an = _rmsnorm(ao, w.nga[...]).astype(BF16)
    cn = _rmsnorm(co, w.ngc[...]).astype(BF16)
    h1 = []
    for r in _row_halves(T):
        mix = _dot(an[r], w.w_out[0:D_ATTN, :]) + _dot(cn[r], w.w_out[D_ATTN:, :])
        yield
        h1.append(_layernorm(ALPHA * x[r] + mix, w.ln1g[...], w.ln1b[...]))
    return jnp.concatenate(h1, axis=0), k, v, vm, carry


def _ffn(h1, w):
    hb = h1.astype(BF16)
    acts = []
    for lo_c in range(0, D_FF, FF_CHUNK):
        hi_c = min(lo_c + FF_CHUNK, D_FF)
        gt = _dot(hb, w.wgu[:, lo_c:hi_c])
        up = _dot(hb, w.wgu[:, D_FF + lo_c:D_FF + hi_c])
        acts.append((gt * jax.nn.sigmoid(gt) * up).astype(BF16))
        yield
    act = jnp.concatenate(acts, axis=1)
    y = []
    for r in _row_halves(h1.shape[0]):
        f = _dot(act[r], w.wd[...])
        yield
        y.append(_layernorm(ALPHA * h1[r] + f, w.ln2g[...], w.ln2b[...]))
    return jnp.concatenate(y, axis=0)


def _run(*gens):
    results = [None] * len(gens)
    live = list(range(len(gens)))
    while live:
        for i in list(live):
            try:
                next(gens[i])
            except StopIteration as done:
                results[i] = done.value
                live.remove(i)
    return results


_WEIGHT_NAMES = ("w_in", "lnv_g", "lnv_b", "sinks", "wpair", "bias", "nga", "ngc", "w_out", "ln1g", "ln1b",
                 "wgu", "wd", "ln2g", "ln2b")
_Weights = collections.namedtuple("_Weights", _WEIGHT_NAMES)
N_W = len(_WEIGHT_NAMES)


def _pipelined_step(step, n_tiles, h1_ref, w, make_mixer, store_y, store_mixer, fill_extra=None):
    cur = step % 2
    prev = 1 - cur

    def phase(do_mixer, do_ffn, extra=None):
        gens = (([_ffn(h1_ref[prev], w)] if do_ffn else []) + ([extra()] if extra else [])
                + ([make_mixer()] if do_mixer else []))
        res = _run(*gens)
        if do_ffn:
            store_y(res[0])
        if do_mixer:
            h1_ref[cur] = res[-1][0]
            store_mixer(*res[-1][1:])

    pl.when(step == 0)(functools.partial(phase, True, False, fill_extra))
    pl.when(jnp.logical_and(step > 0, step < n_tiles))(functools.partial(phase, True, True))
    pl.when(step == n_tiles)(functools.partial(phase, False, True))


def _load_convert(jobs, then):
    plan = [(job, c) for job in jobs for c in range(job[0].shape[0] // job[4])]
    depth = W_SLOTS - 1

    def copy(i):
        (src, _, stage, sem, rows), c = plan[i]
        return pltpu.make_async_copy(src.at[pl.ds(c * rows, rows), :], stage.at[c % W_SLOTS], sem.at[c % W_SLOTS])

    for i in range(min(depth, len(plan))):
        copy(i).start(priority=i % 2)
    for i, ((_, dst, stage, _, rows), c) in enumerate(plan):
        copy(i).wait()
        dst[c * rows:(c + 1) * rows, :] = stage[c % W_SLOTS].astype(BF16)
        if i + depth < len(plan):
            copy(i + depth).start(priority=(i + depth) % 2)
        if (i + 1) % W_CHUNKS_PER_PHASE == 0:
            yield
    then()


def _prompt_body(T, tiles_per_seq, n_tiles, *refs):
    x_ref, c_ref, sa_ref, sb_ref = refs[:4]
    w = _Weights(*refs[4:4 + N_W])
    (y_ref, k_out_ref, v_out_ref, w_out_out, wgu_out, wd_out, hk_ref, hv_ref, h1_ref,
     w_out_v, wgu_v, wd_v, stage_gu, stage_d, sem_gu, sem_d, sem_out) = refs[4 + N_W:]
    step = pl.program_id(0)

    out_copies = (pltpu.make_async_copy(w_out_v, w_out_out, sem_out.at[0]),
                  pltpu.make_async_copy(wgu_v, wgu_out, sem_out.at[1]),
                  pltpu.make_async_copy(wd_v, wd_out, sem_out.at[2]))

    def start_out_copies():
        for cp in out_copies:
            cp.start()

    def convert_weights(w_hbm=w):
        jobs = [(w_hbm.w_out, w_out_v, stage_d, sem_d, WD_ROWS), (w_hbm.wgu, wgu_v, stage_gu, sem_gu, WGU_ROWS),
                (w_hbm.wd, wd_v, stage_d, sem_d, WD_ROWS)]
        return _load_convert(jobs, start_out_copies)

    @pl.when(step == n_tiles)
    def _():
        for cp in out_copies:
            cp.wait()

    w = w._replace(w_out=w_out_v, wgu=wgu_v, wd=wd_v)
    j = step % tiles_per_seq
    rd = j % 2
    wr = 1 - rd

    @pl.when(j == 0)
    def _():
        hk_ref[0] = jnp.zeros(hk_ref.shape[1:], BF16)
        hv_ref[0] = jnp.zeros(hv_ref.shape[1:], BF16)

    def first_valid(ci):
        if ci >= WINDOW // CHUNK:
            return None
        return jnp.where(j == 0, (WINDOW // CHUNK - ci) * CHUNK, 0)

    def make_mixer():
        hist_k = [hk_ref[rd, g] for g in range(N_KV_HEADS)]
        hist_v = [hv_ref[rd, g] for g in range(N_KV_HEADS)]

        def windows(k_rep, v_rep, g, ci):
            k_all = jnp.concatenate([hist_k[g], k_rep[g]], axis=0)
            v_all = jnp.concatenate([hist_v[g], v_rep[g]], axis=0)
            lo = CHUNK * ci
            return k_all[lo:lo + WINDOW + CHUNK], v_all[lo:lo + WINDOW + CHUNK]

        attend = functools.partial(_attend_windows, windows, first_valid, w.sinks)
        return _mixer(x_ref[...], c_ref[...], sa_ref[...], sb_ref[...], attend, CMLP_BLOCK, w)

    def store_y(y):
        y_ref[...] = y

    def store_mixer(k, v, _, carry):
        k_out_ref[0] = jnp.transpose(k[T - WINDOW:, :])
        v_out_ref[0] = jnp.transpose(v[T - WINDOW:, :])
        k_rep, v_rep = carry
        for g in range(N_KV_HEADS):
            hk_ref[wr, g] = k_rep[g][T - WINDOW:]
            hv_ref[wr, g] = v_rep[g][T - WINDOW:]

    _pipelined_step(step, n_tiles, h1_ref, w, make_mixer, store_y, store_mixer, fill_extra=convert_weights)


def _scatter_heads(out_ref, a):
    n = a.shape[1] // HEAD_DIM
    for i in range(n):
        out_ref[pl.ds(i, a.shape[0], stride=n), :] = a[:, HEAD_DIM * i:HEAD_DIM * (i + 1)]


def _cache_rep4(ref):
    rows = [jnp.transpose(ref[D_KV * s:D_KV * (s + 1), :]) for s in range(ref.shape[0] // D_KV)]
    return _rep4(jnp.concatenate(rows, axis=0))


def _sample_body(n_tiles, *refs):
    x_ref, c_ref, sa_ref, sb_ref, ck_ref, cv_ref = refs[:6]
    w = _Weights(*refs[6:6 + N_W])
    y_ref, k_out_ref, v_out_ref, vm_out_ref, h1_ref, wgu_v, wd_v, sem = refs[6 + N_W:]
    step = pl.program_id(0)

    fetches = (pltpu.make_async_copy(w.wgu, wgu_v, sem.at[0]), pltpu.make_async_copy(w.wd, wd_v, sem.at[1]))

    @pl.when(step == 0)
    def _():
        for cp in fetches:
            cp.start()

    @pl.when(step == 1)
    def _():
        for cp in fetches:
            cp.wait()

    w = w._replace(wgu=wgu_v, wd=wd_v)

    def make_mixer():
        ck_rep, cv_rep = _cache_rep4(ck_ref), _cache_rep4(cv_ref)

        def windows(k_rep, v_rep, g, ci):
            cat = lambda cache, new: jnp.concatenate(
                [cache[g][WINDOW * ci:WINDOW * (ci + 1)], new[g][CHUNK * ci:CHUNK * (ci + 1)]], axis=0)
            return cat(ck_rep, k_rep), cat(cv_rep, v_rep)

        attend = functools.partial(_attend_windows, windows, lambda ci: None, w.sinks)
        return _mixer(x_ref[...], c_ref[...], sa_ref[...], sb_ref[...], attend, CHUNK, w)

    def store_y(y):
        y_ref[...] = y

    def store_mixer(k, v, vm, *_):
        _scatter_heads(k_out_ref, k)
        _scatter_heads(v_out_ref, v)
        _scatter_heads(vm_out_ref, vm)

    _pipelined_step(step, n_tiles, h1_ref, w, make_mixer, store_y, store_mixer)


def _cast_body(*refs):
    n = len(refs) // 2
    for src, dst in zip(refs[:n], refs[n:]):
        dst[...] = src[...].astype(BF16)


def _to_bf16(*ws):
    specs = [pl.BlockSpec((a.shape[0] // CAST_STEPS, a.shape[1]), lambda i: (i, 0)) for a in ws]
    return pl.pallas_call(
        _cast_body,
        grid=(CAST_STEPS,),
        in_specs=specs,
        out_specs=specs,
        out_shape=[jax.ShapeDtypeStruct(a.shape, BF16) for a in ws],
        compiler_params=pltpu.CompilerParams(dimension_semantics=("arbitrary",), vmem_limit_bytes=VMEM_LIMIT),
        name="cast_weights",
    )(*ws)


def _const_spec(shape):
    nd = len(shape)
    return pl.BlockSpec(shape, lambda *_: (0,) * nd)


def _weight_specs(ws, in_hbm=()):
    spec = lambda name, a: (pl.BlockSpec(memory_space=pltpu.SMEM) if name == "sinks" else
                            pl.BlockSpec(memory_space=pl.ANY) if name in in_hbm else _const_spec(a.shape))
    return [spec(name, a) for name, a in zip(_WEIGHT_NAMES, ws)]


def _rope_tables(pos, reps=1):
    half = ROPE_DIM // 2
    inv = np.power(ROPE_THETA, -np.arange(half, dtype=np.float64) * (2.0 / ROPE_DIM))
    ang = np.asarray(pos, np.float64)[:, None] * inv[None, :]
    cos, sin = np.cos(ang), np.sin(ang)
    n = ang.shape[0]
    pad = HEAD_DIM - ROPE_DIM
    c = np.concatenate([cos, cos, np.ones((n, pad))], 1)
    sa = np.concatenate([-sin, np.zeros((n, half + pad))], 1)
    sb = np.concatenate([np.zeros((n, half)), sin, np.zeros((n, pad))], 1)
    return [jnp.asarray(np.tile(t, (reps, LANES // HEAD_DIM)), F32) for t in (c, sa, sb)]


def _gate_params(w_s, b_s, lb):
    i = np.arange(lb)
    mask = (i[None, :] // CHUNK) <= (i[:, None] // CHUNK)
    wpair = jnp.where(mask[None], w_s[:, :lb, :lb], 0.0).astype(BF16)
    bias = jnp.repeat(jnp.transpose(b_s[:, :lb]), CMLP_GROUP_DIM, axis=1)
    return wpair, bias


def kernel(x_prompt, x_sample, cache_win_k, cache_win_v, w_in, ln_v_g, ln_v_b, attn_sinks, w_spatial, b_spatial,
           norm_attn_g, norm_cmlp_g, w_out, ln1_g, ln1_b, w_gate_up, w_down, ln2_g, ln2_b):
    B, S, _ = x_prompt.shape
    Bd, L, _ = x_sample.shape
    assert cache_win_k.shape[2] == WINDOW and L == CHUNK and w_in.shape[0] == 1

    row = lambda a: a[0].reshape(1, -1)
    w_in_b, = _to_bf16(w_in[0])

    wpair, bias = _gate_params(w_spatial[0], b_spatial[0], CMLP_BLOCK)

    def weights(w_out_, wgu, wd):
        return _Weights(w_in=w_in_b, lnv_g=row(ln_v_g), lnv_b=row(ln_v_b), sinks=attn_sinks[0],
                        wpair=wpair, bias=bias, nga=row(norm_attn_g), ngc=row(norm_cmlp_g),
                        w_out=w_out_, ln1g=row(ln1_g), ln1b=row(ln1_b),
                        wgu=wgu, wd=wd, ln2g=row(ln2_g), ln2b=row(ln2_b))

    T = PROMPT_TILE
    tps = S // T
    n_tiles = B * tps
    wp = weights(w_out[0], w_gate_up[0], w_down[0])
    tables = _rope_tables(np.arange(S))
    mix_tile = lambda s: jnp.minimum(s, n_tiles - 1)
    ffn_tile = lambda s: jnp.maximum(s - 1, 0)
    tab = pl.BlockSpec((T, LANES), lambda s: (mix_tile(s) % tps, 0))
    win = pl.BlockSpec((1, WINDOW, D_KV), lambda s: (mix_tile(s) // tps, 0, 0))
    hbm = pl.BlockSpec(memory_space=pl.ANY)
    y_p, kwin, vwin, w_out_b, wgu_b, wd_b = pl.pallas_call(
        functools.partial(_prompt_body, T, tps, n_tiles),
        grid=(n_tiles + 1,),
        in_specs=[pl.BlockSpec((T, D_MODEL), lambda s: (mix_tile(s), 0)), tab, tab, tab]
        + _weight_specs(wp, in_hbm=("w_out", "wgu", "wd")),
        out_specs=[pl.BlockSpec((T, D_MODEL), lambda s: (ffn_tile(s), 0)), win, win, hbm, hbm, hbm],
        out_shape=[jax.ShapeDtypeStruct((B * S, D_MODEL), F32),
                   jax.ShapeDtypeStruct((B, WINDOW, D_KV), F32),
                   jax.ShapeDtypeStruct((B, WINDOW, D_KV), F32),
                   jax.ShapeDtypeStruct((D_MODEL, D_MODEL), BF16),
                   jax.ShapeDtypeStruct((D_MODEL, 2 * D_FF), BF16),
                   jax.ShapeDtypeStruct((D_FF, D_MODEL), BF16)],
        scratch_shapes=[pltpu.VMEM((2, N_KV_HEADS, WINDOW, 2 * LANES), BF16),
                        pltpu.VMEM((2, N_KV_HEADS, WINDOW, 2 * LANES), BF16),
                        pltpu.VMEM((2, T, D_MODEL), F32),
                        pltpu.VMEM((D_MODEL, D_MODEL), BF16),
                        pltpu.VMEM((D_MODEL, 2 * D_FF), BF16),
                        pltpu.VMEM((D_FF, D_MODEL), BF16),
                        pltpu.VMEM((W_SLOTS, WGU_ROWS, 2 * D_FF), F32),
                        pltpu.VMEM((W_SLOTS, WD_ROWS, D_MODEL), F32),
                        pltpu.SemaphoreType.DMA((W_SLOTS,)),
                        pltpu.SemaphoreType.DMA((W_SLOTS,)),
                        pltpu.SemaphoreType.DMA((3,))],
        compiler_params=pltpu.CompilerParams(dimension_semantics=("arbitrary",), vmem_limit_bytes=VMEM_LIMIT),
        name="layer_prompt",
    )(x_prompt.reshape(B * S, D_MODEL), *tables, *wp)
    y_p = y_p.reshape(B, S, D_MODEL)

    NS = SAMPLE_SEQS
    Ts = NS * L
    ws = weights(w_out_b, wgu_b, wd_b)
    tables_s = _rope_tables(PAST_LEN + np.arange(L), reps=NS)
    n_st = Bd // NS
    mix_rows = lambda n, width: pl.BlockSpec((n, width), lambda s: (jnp.minimum(s, n_st - 1), 0))
    cache_rows = NS * D_KV
    cache_t = lambda a: jnp.transpose(a[0], (0, 2, 3, 1)).reshape(Bd * D_KV, WINDOW)
    y_s, k_s, v_s, vm_s = pl.pallas_call(
        functools.partial(_sample_body, n_st),
        grid=(n_st + 1,),
        in_specs=[mix_rows(Ts, D_MODEL)] + [_const_spec(t.shape) for t in tables_s]
        + [mix_rows(cache_rows, WINDOW), mix_rows(cache_rows, WINDOW)] + _weight_specs(ws, in_hbm=("wgu", "wd")),
        out_specs=[pl.BlockSpec((Ts, D_MODEL), lambda s: (jnp.maximum(s - 1, 0), 0)),
                   mix_rows(Ts * N_KV_HEADS, HEAD_DIM), mix_rows(Ts * N_KV_HEADS, HEAD_DIM),
                   mix_rows(Ts * CMLP_GROUPS, CMLP_GROUP_DIM)],
        out_shape=[jax.ShapeDtypeStruct((Bd * L, D_MODEL), F32),
                   jax.ShapeDtypeStruct((Bd * L * N_KV_HEADS, HEAD_DIM), F32),
                   jax.ShapeDtypeStruct((Bd * L * N_KV_HEADS, HEAD_DIM), F32),
                   jax.ShapeDtypeStruct((Bd * L * CMLP_GROUPS, CMLP_GROUP_DIM), F32)],
        scratch_shapes=[pltpu.VMEM((2, Ts, D_MODEL), F32),
                        pltpu.VMEM((D_MODEL, 2 * D_FF), BF16),
                        pltpu.VMEM((D_FF, D_MODEL), BF16),
                        pltpu.SemaphoreType.DMA((2,))],
        compiler_params=pltpu.CompilerParams(dimension_semantics=("arbitrary",), vmem_limit_bytes=VMEM_LIMIT),
        name="layer_sample",
    )(x_sample.reshape(Bd * L, D_MODEL), *tables_s,
      cache_t(cache_win_k), cache_t(cache_win_v), *ws)

    win_result = lambda a: jnp.transpose(a.reshape(B, N_KV_HEADS, HEAD_DIM, WINDOW), (0, 3, 1, 2))[None]
    return (y_p, y_s.reshape(Bd, L, D_MODEL),
            win_result(kwin), win_result(vwin),
            k_s.reshape(1, Bd, L, N_KV_HEADS, HEAD_DIM),
            v_s.reshape(1, Bd, L, N_KV_HEADS, HEAD_DIM),
            vm_s.reshape(1, Bd, L, CMLP_GROUPS, CMLP_GROUP_DIM))
```

```python
import collections
import functools

import jax
import jax.numpy as jnp
import numpy as np
from jax import lax
from jax.experimental import pallas as pl
from jax.experimental.pallas import tpu as pltpu

D_MODEL = 1024
CHUNK = 64
HEAD_DIM = 64
D_ATTN = 512
D_CMLP = 512
N_HEADS = 8
N_KV_HEADS = 2
GQA_GROUP = 4
D_KV = 128
WINDOW = 128
ROPE_DIM = 16
ROPE_THETA = 500000.0
CMLP_BLOCK = 128
CMLP_GROUPS = 8
CMLP_GROUP_DIM = 64
D_IN = 1792
D_FF = 2816
PAST_LEN = 1024
ALPHA = 2.0 ** 0.25
LN_EPS = 1e-5
NEG_INF = -1e30
LANES = 128
FF_CHUNK = 256
VMEM_LIMIT = 56 * 1024 * 1024
SAMPLE_VMEM_LIMIT = 48 * 1024 * 1024
CAST_VMEM_LIMIT = 24 * 1024 * 1024
CAST_STEPS = 2
PROMPT_TILE = 512
SAMPLE_SEQS = 4
WGU_ROWS, WD_ROWS = 16, 64
W_SLOTS = 8
W_CHUNKS_PER_PHASE = 8

F32 = jnp.float32
BF16 = jnp.bfloat16


def _layernorm(x, g, b):
    mu = jnp.mean(x, -1, keepdims=True)
    d = x - mu
    var = jnp.mean(d * d, -1, keepdims=True)
    return d * lax.rsqrt(var + LN_EPS) * g + b


def _rmsnorm(x, g):
    ms = jnp.mean(x * x, -1, keepdims=True)
    return x * lax.rsqrt(ms + LN_EPS) * g


def _dot(a, b):
    return jnp.dot(a, b, preferred_element_type=F32)


def _rope(x, c, sa, sb):
    out = []
    for i in range(x.shape[1] // LANES):
        s = x[:, LANES * i:LANES * (i + 1)]
        up = pltpu.roll(s, LANES - ROPE_DIM // 2, 1)
        dn = pltpu.roll(s, ROPE_DIM // 2, 1)
        out.append(s * c + up * sa + dn * sb)
    return out[0] if len(out) == 1 else jnp.concatenate(out, axis=1)


def _row_halves(n):
    return (slice(0, n // 2), slice(n // 2, n))


def _rep4(a):
    ar = pltpu.roll(a, HEAD_DIM, 1)
    first = lax.broadcasted_iota(jnp.int32, a.shape, 1) < HEAD_DIM
    g0 = jnp.where(first, a, ar).astype(BF16)
    g1 = jnp.where(first, ar, a).astype(BF16)
    return [jnp.concatenate([g0, g0], axis=1), jnp.concatenate([g1, g1], axis=1)]


def _attend_windows(windows, first_valid, sinks, q, k, v):
    T = q.shape[0]
    k_rep, v_rep = _rep4(k), _rep4(v)
    head_lane = lax.broadcasted_iota(jnp.int32, (CHUNK, GQA_GROUP * HEAD_DIM), 1) // HEAD_DIM
    key_idx = lax.broadcasted_iota(jnp.int32, (GQA_GROUP * CHUNK, WINDOW + CHUNK), 1)
    ao_rows = []
    for ci in range(T // CHUNK):
        ao_g = []
        for g in range(N_KV_HEADS):
            qc = q[CHUNK * ci:CHUNK * (ci + 1), 256 * g:256 * (g + 1)]
            qm = jnp.concatenate([jnp.where(head_lane == h, qc, 0.0) for h in range(GQA_GROUP)], axis=0).astype(BF16)
            kw, vw = windows(k_rep, v_rep, g, ci)
            s = lax.dot_general(qm, kw, (((1,), (1,)), ((), ())), preferred_element_type=F32)
            fv = first_valid(ci)
            if fv is not None:
                s = jnp.where(key_idx >= fv, s, NEG_INF)
            ps = []
            for h in range(GQA_GROUP):
                sh = s[CHUNK * h:CHUNK * (h + 1)]
                sink = sinks[GQA_GROUP * g + h]
                m = jnp.maximum(jnp.max(sh, -1, keepdims=True), sink)
                p = jnp.exp(sh - m)
                denom = jnp.sum(p, -1, keepdims=True) + jnp.exp(sink - m)
                ps.append(p * (1.0 / denom))
            pm = jnp.concatenate(ps, axis=0).astype(BF16)
            r = _dot(pm, vw)
            o = jnp.where(head_lane == 0, r[0:CHUNK], 0.0)
            for h in range(1, GQA_GROUP):
                o = o + jnp.where(head_lane == h, r[CHUNK * h:CHUNK * (h + 1)], 0.0)
            ao_g.append(o)
        ao_rows.append(jnp.concatenate(ao_g, axis=1))
        yield
    return jnp.concatenate(ao_rows, axis=0), (k_rep, v_rep)


def _mixer(x, c, sa, sb, attend, LB, w):
    T = x.shape[0]
    nb = T // LB
    xb = x.astype(BF16)
    uv = jax.nn.gelu(_dot(xb, w.w_in[:, D_ATTN + 2 * D_KV:D_IN]))
    yield
    q = _dot(xb, w.w_in[:, 0:D_ATTN])
    kv = _dot(xb, w.w_in[:, D_ATTN:D_ATTN + 2 * D_KV])

    k = _rope(kv[:, :D_KV], c, sa, sb)
    v = kv[:, D_KV:]
    q = _rope(q, c, sa, sb) * (HEAD_DIM ** -0.5)

    yield
    u = uv[:, :D_CMLP]
    vm = _layernorm(uv[:, D_CMLP:], w.lnv_g[...], w.lnv_b[...])

    lo_mask = lax.broadcasted_iota(jnp.int32, (T, LANES), 1) < CMLP_GROUP_DIM
    pad = [jnp.zeros((CMLP_BLOCK - LB, LANES), BF16)] if LB < CMLP_BLOCK else []
    gate_slabs = []
    for p in range(D_CMLP // LANES):
        slab = vm[:, LANES * p:LANES * (p + 1)]
        lo = jnp.where(lo_mask, slab, 0.0).astype(BF16)
        hi = jnp.where(lo_mask, 0.0, slab).astype(BF16)
        rhs = jnp.concatenate(
            [jnp.concatenate([lo[LB * b:LB * (b + 1)]] + pad + [hi[LB * b:LB * (b + 1)]] + pad, axis=0)
             for b in range(nb)], axis=1)
        w_pair = jnp.concatenate([w.wpair[2 * p, 0:LB, :], w.wpair[2 * p + 1, 0:LB, :]], axis=1)
        o = _dot(w_pair, rhs)
        gate_slabs.append(jnp.concatenate([o[:, LANES * b:LANES * (b + 1)] for b in range(nb)], axis=0))
    s_gate = jnp.concatenate(gate_slabs, axis=1) + jnp.concatenate([w.bias[0:LB, :]] * nb, axis=0)
    co = u * s_gate
    yield

    ao, carry = yield from attend(q, k, v)

    an = _rmsnorm(ao, w.nga[...]).astype(BF16)
    cn = _rmsnorm(co, w.ngc[...]).astype(BF16)
    h1 = []
    for r in _row_halves(T):
        mix = _dot(an[r], w.w_out[0:D_ATTN, :]) + _dot(cn[r], w.w_out[D_ATTN:, :])
        yield
        h1.append(_layernorm(ALPHA * x[r] + mix, w.ln1g[...], w.ln1b[...]))
    return jnp.concatenate(h1, axis=0), k, v, vm, carry


def _ffn(h1, w):
    hb = h1.astype(BF16)
    acts = []
    for lo_c in range(0, D_FF, FF_CHUNK):
        hi_c = min(lo_c + FF_CHUNK, D_FF)
        gt = _dot(hb, w.wgu[:, lo_c:hi_c])
        up = _dot(hb, w.wgu[:, D_FF + lo_c:D_FF + hi_c])
        acts.append((gt * jax.nn.sigmoid(gt) * up).astype(BF16))
        yield
    act = jnp.concatenate(acts, axis=1)
    y = []
    for r in _row_halves(h1.shape[0]):
        f = _dot(act[r], w.wd[...])
        yield
        y.append(_layernorm(ALPHA * h1[r] + f, w.ln2g[...], w.ln2b[...]))
    return jnp.concatenate(y, axis=0)


def _run(*gens):
    results = [None] * len(gens)
    live = list(range(len(gens)))
    while live:
        for i in list(live):
            try:
                next(gens[i])
            except StopIteration as done:
                results[i] = done.value
                live.remove(i)
    return results


_WEIGHT_NAMES = ("w_in", "lnv_g", "lnv_b", "sinks", "wpair", "bias", "nga", "ngc", "w_out", "ln1g", "ln1b",
                 "wgu", "wd", "ln2g", "ln2b")
_Weights = collections.namedtuple("_Weights", _WEIGHT_NAMES)
N_W = len(_WEIGHT_NAMES)


def _pipelined_step(step, n_tiles, h1_ref, w, make_mixer, store_y, store_mixer, fill_extra=None):
    cur = step % 2
    prev = 1 - cur

    def phase(do_mixer, do_ffn, extra=None):
        gens = (([_ffn(h1_ref[prev], w)] if do_ffn else []) + ([extra()] if extra else [])
                + ([make_mixer()] if do_mixer else []))
        res = _run(*gens)
        if do_ffn:
            store_y(res[0])
        if do_mixer:
            h1_ref[cur] = res[-1][0]
            store_mixer(*res[-1][1:])

    pl.when(step == 0)(functools.partial(phase, True, False, fill_extra))
    pl.when(jnp.logical_and(step > 0, step < n_tiles))(functools.partial(phase, True, True))
    pl.when(step == n_tiles)(functools.partial(phase, False, True))


def _load_convert(jobs, then):
    plan = [(job, c) for job in jobs for c in range(job[0].shape[0] // job[4])]
    depth = W_SLOTS - 1

    def copy(i):
        (src, _, stage, sem, rows), c = plan[i]
        return pltpu.make_async_copy(src.at[pl.ds(c * rows, rows), :], stage.at[c % W_SLOTS], sem.at[c % W_SLOTS])

    for i in range(min(depth, len(plan))):
        copy(i).start(priority=i % 2)
    for i, ((_, dst, stage, _, rows), c) in enumerate(plan):
        copy(i).wait()
        dst[c * rows:(c + 1) * rows, :] = stage[c % W_SLOTS].astype(BF16)
        if i + depth < len(plan):
            copy(i + depth).start(priority=(i + depth) % 2)
        if (i + 1) % W_CHUNKS_PER_PHASE == 0:
            yield
    then()


def _prompt_body(T, tiles_per_seq, n_tiles, *refs):
    x_ref, c_ref, sa_ref, sb_ref = refs[:4]
    w = _Weights(*refs[4:4 + N_W])
    (y_ref, k_out_ref, v_out_ref, w_out_out, wgu_out, wd_out, hk_ref, hv_ref, h1_ref,
     w_out_v, wgu_v, wd_v, stage_gu, stage_d, sem_gu, sem_d, sem_out) = refs[4 + N_W:]
    step = pl.program_id(0)

    out_copies = (pltpu.make_async_copy(w_out_v, w_out_out, sem_out.at[0]),
                  pltpu.make_async_copy(wgu_v, wgu_out, sem_out.at[1]),
                  pltpu.make_async_copy(wd_v, wd_out, sem_out.at[2]))

    def start_out_copies():
        for cp in out_copies:
            cp.start()

    def convert_weights(w_hbm=w):
        jobs = [(w_hbm.w_out, w_out_v, stage_d, sem_d, WD_ROWS), (w_hbm.wgu, wgu_v, stage_gu, sem_gu, WGU_ROWS),
                (w_hbm.wd, wd_v, stage_d, sem_d, WD_ROWS)]
        return _load_convert(jobs, start_out_copies)

    @pl.when(step == n_tiles)
    def _():
        for cp in out_copies:
            cp.wait()

    w = w._replace(w_out=w_out_v, wgu=wgu_v, wd=wd_v)
    j = step % tiles_per_seq
    rd = j % 2
    wr = 1 - rd

    @pl.when(j == 0)
    def _():
        hk_ref[0] = jnp.zeros(hk_ref.shape[1:], BF16)
        hv_ref[0] = jnp.zeros(hv_ref.shape[1:], BF16)

    def first_valid(ci):
        if ci >= WINDOW // CHUNK:
            return None
        return jnp.where(j == 0, (WINDOW // CHUNK - ci) * CHUNK, 0)

    def make_mixer():
        hist_k = [hk_ref[rd, g] for g in range(N_KV_HEADS)]
        hist_v = [hv_ref[rd, g] for g in range(N_KV_HEADS)]

        def windows(k_rep, v_rep, g, ci):
            k_all = jnp.concatenate([hist_k[g], k_rep[g]], axis=0)
            v_all = jnp.concatenate([hist_v[g], v_rep[g]], axis=0)
            lo = CHUNK * ci
            return k_all[lo:lo + WINDOW + CHUNK], v_all[lo:lo + WINDOW + CHUNK]

        attend = functools.partial(_attend_windows, windows, first_valid, w.sinks)
        return _mixer(x_ref[...], c_ref[...], sa_ref[...], sb_ref[...], attend, CMLP_BLOCK, w)

    def store_y(y):
        y_ref[...] = y

    def store_mixer(k, v, _, carry):
        k_out_ref[0] = jnp.transpose(k[T - WINDOW:, :])
        v_out_ref[0] = jnp.transpose(v[T - WINDOW:, :])
        k_rep, v_rep = carry
        for g in range(N_KV_HEADS):
            hk_ref[wr, g] = k_rep[g][T - WINDOW:]
            hv_ref[wr, g] = v_rep[g][T - WINDOW:]

    _pipelined_step(step, n_tiles, h1_ref, w, make_mixer, store_y, store_mixer, fill_extra=convert_weights)


def _scatter_heads(out_ref, a):
    n = a.shape[1] // HEAD_DIM
    for i in range(n):
        out_ref[pl.ds(i, a.shape[0], stride=n), :] = a[:, HEAD_DIM * i:HEAD_DIM * (i + 1)]


def _cache_rep4(ref):
    rows = [jnp.transpose(ref[D_KV * s:D_KV * (s + 1), :]) for s in range(ref.shape[0] // D_KV)]
    return _rep4(jnp.concatenate(rows, axis=0))


def _sample_body(n_tiles, *refs):
    x_ref, c_ref, sa_ref, sb_ref, ck_ref, cv_ref = refs[:6]
    w = _Weights(*refs[6:6 + N_W])
    y_ref, k_out_ref, v_out_ref, vm_out_ref, h1_ref, wgu_v, wd_v, sem = refs[6 + N_W:]
    step = pl.program_id(0)

    fetches = (pltpu.make_async_copy(w.wgu, wgu_v, sem.at[0]), pltpu.make_async_copy(w.wd, wd_v, sem.at[1]))

    @pl.when(step == 0)
    def _():
        for cp in fetches:
            cp.start()

    @pl.when(step == 1)
    def _():
        for cp in fetches:
            cp.wait()

    w = w._replace(wgu=wgu_v, wd=wd_v)

    def make_mixer():
        ck_rep, cv_rep = _cache_rep4(ck_ref), _cache_rep4(cv_ref)

        def windows(k_rep, v_rep, g, ci):
            cat = lambda cache, new: jnp.concatenate(
                [cache[g][WINDOW * ci:WINDOW * (ci + 1)], new[g][CHUNK * ci:CHUNK * (ci + 1)]], axis=0)
            return cat(ck_rep, k_rep), cat(cv_rep, v_rep)

        attend = functools.partial(_attend_windows, windows, lambda ci: None, w.sinks)
        return _mixer(x_ref[...], c_ref[...], sa_ref[...], sb_ref[...], attend, CHUNK, w)

    def store_y(y):
        y_ref[...] = y

    def store_mixer(k, v, vm, *_):
        _scatter_heads(k_out_ref, k)
        _scatter_heads(v_out_ref, v)
        _scatter_heads(vm_out_ref, vm)

    _pipelined_step(step, n_tiles, h1_ref, w, make_mixer, store_y, store_mixer)


def _cast_body(*refs):
    n = len(refs) // 2
    for src, dst in zip(refs[:n], refs[n:]):
        dst[...] = src[...].astype(BF16)


def _to_bf16(*ws):
    specs = [pl.BlockSpec((a.shape[0] // CAST_STEPS, a.shape[1]), lambda i: (i, 0)) for a in ws]
    return pl.pallas_call(
        _cast_body,
        grid=(CAST_STEPS,),
        in_specs=specs,
        out_specs=specs,
        out_shape=[jax.ShapeDtypeStruct(a.shape, BF16) for a in ws],
        compiler_params=pltpu.CompilerParams(dimension_semantics=("arbitrary",), vmem_limit_bytes=CAST_VMEM_LIMIT),
        name="cast_weights",
    )(*ws)


def _const_spec(shape):
    nd = len(shape)
    return pl.BlockSpec(shape, lambda *_: (0,) * nd)


def _weight_specs(ws, in_hbm=()):
    spec = lambda name, a: (pl.BlockSpec(memory_space=pltpu.SMEM) if name == "sinks" else
                            pl.BlockSpec(memory_space=pl.ANY) if name in in_hbm else _const_spec(a.shape))
    return [spec(name, a) for name, a in zip(_WEIGHT_NAMES, ws)]


def _rope_tables(pos, reps=1):
    half = ROPE_DIM // 2
    inv = np.power(ROPE_THETA, -np.arange(half, dtype=np.float64) * (2.0 / ROPE_DIM))
    ang = np.asarray(pos, np.float64)[:, None] * inv[None, :]
    cos, sin = np.cos(ang), np.sin(ang)
    n = ang.shape[0]
    pad = HEAD_DIM - ROPE_DIM
    c = np.concatenate([cos, cos, np.ones((n, pad))], 1)
    sa = np.concatenate([-sin, np.zeros((n, half + pad))], 1)
    sb = np.concatenate([np.zeros((n, half)), sin, np.zeros((n, pad))], 1)
    return [jnp.asarray(np.tile(t, (reps, LANES // HEAD_DIM)), F32) for t in (c, sa, sb)]


def _gate_params(w_s, b_s, lb):
    i = np.arange(lb)
    mask = (i[None, :] // CHUNK) <= (i[:, None] // CHUNK)
    wpair = jnp.where(mask[None], w_s[:, :lb, :lb], 0.0).astype(BF16)
    bias = jnp.repeat(jnp.transpose(b_s[:, :lb]), CMLP_GROUP_DIM, axis=1)
    return wpair, bias


def kernel(x_prompt, x_sample, cache_win_k, cache_win_v, w_in, ln_v_g, ln_v_b, attn_sinks, w_spatial, b_spatial,
           norm_attn_g, norm_cmlp_g, w_out, ln1_g, ln1_b, w_gate_up, w_down, ln2_g, ln2_b):
    B, S, _ = x_prompt.shape
    Bd, L, _ = x_sample.shape
    assert cache_win_k.shape[2] == WINDOW and L == CHUNK and w_in.shape[0] == 1

    row = lambda a: a[0].reshape(1, -1)
    w_in_b, = _to_bf16(w_in[0])

    wpair, bias = _gate_params(w_spatial[0], b_spatial[0], CMLP_BLOCK)

    def weights(w_out_, wgu, wd):
        return _Weights(w_in=w_in_b, lnv_g=row(ln_v_g), lnv_b=row(ln_v_b), sinks=attn_sinks[0],
                        wpair=wpair, bias=bias, nga=row(norm_attn_g), ngc=row(norm_cmlp_g),
                        w_out=w_out_, ln1g=row(ln1_g), ln1b=row(ln1_b),
                        wgu=wgu, wd=wd, ln2g=row(ln2_g), ln2b=row(ln2_b))

    T = PROMPT_TILE
    tps = S // T
    n_tiles = B * tps
    wp = weights(w_out[0], w_gate_up[0], w_down[0])
    tables = _rope_tables(np.arange(S))
    mix_tile = lambda s: jnp.minimum(s, n_tiles - 1)
    ffn_tile = lambda s: jnp.maximum(s - 1, 0)
    tab = pl.BlockSpec((T, LANES), lambda s: (mix_tile(s) % tps, 0))
    win = pl.BlockSpec((1, WINDOW, D_KV), lambda s: (mix_tile(s) // tps, 0, 0))
    hbm = pl.BlockSpec(memory_space=pl.ANY)
    y_p, kwin, vwin, w_out_b, wgu_b, wd_b = pl.pallas_call(
        functools.partial(_prompt_body, T, tps, n_tiles),
        grid=(n_tiles + 1,),
        in_specs=[pl.BlockSpec((T, D_MODEL), lambda s: (mix_tile(s), 0)), tab, tab, tab]
        + _weight_specs(wp, in_hbm=("w_out", "wgu", "wd")),
        out_specs=[pl.BlockSpec((T, D_MODEL), lambda s: (ffn_tile(s), 0)), win, win, hbm, hbm, hbm],
        out_shape=[jax.ShapeDtypeStruct((B * S, D_MODEL), F32),
                   jax.ShapeDtypeStruct((B, WINDOW, D_KV), F32),
                   jax.ShapeDtypeStruct((B, WINDOW, D_KV), F32),
                   jax.ShapeDtypeStruct((D_MODEL, D_MODEL), BF16),
                   jax.ShapeDtypeStruct((D_MODEL, 2 * D_FF), BF16),
                   jax.ShapeDtypeStruct((D_FF, D_MODEL), BF16)],
        scratch_shapes=[pltpu.VMEM((2, N_KV_HEADS, WINDOW, 2 * LANES), BF16),
                        pltpu.VMEM((2, N_KV_HEADS, WINDOW, 2 * LANES), BF16),
                        pltpu.VMEM((2, T, D_MODEL), F32),
                        pltpu.VMEM((D_MODEL, D_MODEL), BF16),
                        pltpu.VMEM((D_MODEL, 2 * D_FF), BF16),
                        pltpu.VMEM((D_FF, D_MODEL), BF16),
                        pltpu.VMEM((W_SLOTS, WGU_ROWS, 2 * D_FF), F32),
                        pltpu.VMEM((W_SLOTS, WD_ROWS, D_MODEL), F32),
                        pltpu.SemaphoreType.DMA((W_SLOTS,)),
                        pltpu.SemaphoreType.DMA((W_SLOTS,)),
                        pltpu.SemaphoreType.DMA((3,))],
        compiler_params=pltpu.CompilerParams(dimension_semantics=("arbitrary",), vmem_limit_bytes=VMEM_LIMIT),
        name="layer_prompt",
    )(x_prompt.reshape(B * S, D_MODEL), *tables, *wp)
    y_p = y_p.reshape(B, S, D_MODEL)

    NS = SAMPLE_SEQS
    Ts = NS * L
    ws = weights(w_out_b, wgu_b, wd_b)
    tables_s = _rope_tables(PAST_LEN + np.arange(L), reps=NS)
    n_st = Bd // NS
    mix_rows = lambda n, width: pl.BlockSpec((n, width), lambda s: (jnp.minimum(s, n_st - 1), 0))
    cache_rows = NS * D_KV
    cache_t = lambda a: jnp.transpose(a[0], (0, 2, 3, 1)).reshape(Bd * D_KV, WINDOW)
    y_s, k_s, v_s, vm_s = pl.pallas_call(
        functools.partial(_sample_body, n_st),
        grid=(n_st + 1,),
        in_specs=[mix_rows(Ts, D_MODEL)] + [_const_spec(t.shape) for t in tables_s]
        + [mix_rows(cache_rows, WINDOW), mix_rows(cache_rows, WINDOW)] + _weight_specs(ws, in_hbm=("wgu", "wd")),
        out_specs=[pl.BlockSpec((Ts, D_MODEL), lambda s: (jnp.maximum(s - 1, 0), 0)),
                   mix_rows(Ts * N_KV_HEADS, HEAD_DIM), mix_rows(Ts * N_KV_HEADS, HEAD_DIM),
                   mix_rows(Ts * CMLP_GROUPS, CMLP_GROUP_DIM)],
        out_shape=[jax.ShapeDtypeStruct((Bd * L, D_MODEL), F32),
                   jax.ShapeDtypeStruct((Bd * L * N_KV_HEADS, HEAD_DIM), F32),
                   jax.ShapeDtypeStruct((Bd * L * N_KV_HEADS, HEAD_DIM), F32),
                   jax.ShapeDtypeStruct((Bd * L * CMLP_GROUPS, CMLP_GROUP_DIM), F32)],
        scratch_shapes=[pltpu.VMEM((2, Ts, D_MODEL), F32),
                        pltpu.VMEM((D_MODEL, 2 * D_FF), BF16),
                        pltpu.VMEM((D_FF, D_MODEL), BF16),
                        pltpu.SemaphoreType.DMA((2,))],
        compiler_params=pltpu.CompilerParams(dimension_semantics=("arbitrary",), vmem_limit_bytes=SAMPLE_VMEM_LIMIT),
        name="layer_sample",
    )(x_sample.reshape(Bd * L, D_MODEL), *tables_s,
      cache_t(cache_win_k), cache_t(cache_win_v), *ws)

    win_result = lambda a: jnp.transpose(a.reshape(B, N_KV_HEADS, HEAD_DIM, WINDOW), (0, 3, 1, 2))[None]
    return (y_p, y_s.reshape(Bd, L, D_MODEL),
            win_result(kwin), win_result(vwin),
            k_s.reshape(1, Bd, L, N_KV_HEADS, HEAD_DIM),
            v_s.reshape(1, Bd, L, N_KV_HEADS, HEAD_DIM),
            vm_s.reshape(1, Bd, L, CMLP_GROUPS, CMLP_GROUP_DIM))
```
